```python
import math
import jax
import jax.numpy as jnp
from jax import lax
import numpy as np

D_MODEL = 1024
BATCH = 16
SEQ = 2048
DEPTH = 1
DEC_BATCH = 128
DEC_SEQ = 1
PAST_LEN = 8192
PAGE_SIZE = 128

MLA_HEADS = 8
QK_NOPE = 64
QK_ROPE = 32
V_HEAD = 64
Q_RANK = 256
KV_RANK = 128
MLA_WIDTH = MLA_HEADS * V_HEAD
MLA_SCALE = (QK_NOPE + QK_ROPE) ** -0.5
ROPE_THETA = 10000.0
Q_BLOCK = 128
GM_WIDTH = D_MODEL // 2
GM_GROUPS = 4
GM_GROUP_DIM = GM_WIDTH // GM_GROUPS
GM_CHUNK = 128
MIX_WIDTH = MLA_WIDTH + GM_WIDTH
IN_COLS = Q_RANK + KV_RANK + QK_ROPE + 2 * GM_WIDTH
MEM_TOKENS = 256
X_HEADS = 4
X_HEAD_DIM = 128
PEER_HEADS = 8
N_KEYS = 128
N_EXPERTS = N_KEYS * N_KEYS
PEER_TOPK = 16
PEER_DK = 256
PEER_HALF = PEER_DK // 2
PEER_BLOCK = 128
ALPHA = (2.0 * DEPTH) ** 0.25
BETA = (8.0 * DEPTH) ** -0.25
EPS = 1e-5

kernel_name = 'hymba_mla_gmlp_peer_step'


def layer_norm(x, g, b):
    xf = x.astype(jnp.float32)
    mu = jnp.mean(xf, -1, keepdims=True)
    var = jnp.mean(jnp.square(xf - mu), -1, keepdims=True)
    return ((xf - mu) * lax.rsqrt(var + EPS) * g + b).astype(x.dtype)


def rms_norm(x, g):
    xf = x.astype(jnp.float32)
    return (xf * lax.rsqrt(jnp.mean(jnp.square(xf), -1, keepdims=True) + EPS) * g).astype(x.dtype)


def rope_tables(pos):
    inv = ROPE_THETA ** (-jnp.arange(0, QK_ROPE, 2, dtype=jnp.float32) / QK_ROPE)
    ang = pos.astype(jnp.float32)[:, None] * inv[None, :]
    return jnp.cos(ang), jnp.sin(ang)


def apply_rope(x, cos, sin):
    x1, x2 = jnp.split(x, 2, axis=-1)
    return jnp.concatenate([x1 * cos - x2 * sin, x2 * cos + x1 * sin], axis=-1).astype(x.dtype)


def mixer_inputs(x, pos, w_in, q_norm_g, kv_norm_g, w_uq, w_uk, gm_norm_g, gm_norm_b):
    h = x @ w_in
    c_q, c_kv, k_r, uv = jnp.split(h, [Q_RANK, Q_RANK + KV_RANK, Q_RANK + KV_RANK + QK_ROPE], axis=-1)
    cos, sin = rope_tables(pos)
    q = jnp.einsum('bsr,rhd->bshd', rms_norm(c_q, q_norm_g), w_uq)
    q_nope, q_rope = q[..., :QK_NOPE], q[..., QK_NOPE:]
    q_rope = apply_rope(q_rope, cos[None, :, None], sin[None, :, None])
    q_lat = jnp.einsum('bshd,rhd->bshr', q_nope, w_uk)
    ckv = rms_norm(c_kv, kv_norm_g)
    krope = apply_rope(k_r, cos[None], sin[None])
    u, v = jnp.split(jax.nn.gelu(uv), 2, axis=-1)
    v = layer_norm(v, gm_norm_g, gm_norm_b)
    return q_lat, q_rope, ckv, krope, u, v


def mla_core(q_lat, q_rope, ckv, krope, mask):
    s = (jnp.einsum('bqhr,bkr->bhqk', q_lat, ckv)
         + jnp.einsum('bqhd,bkd->bhqk', q_rope, krope)).astype(jnp.float32) * MLA_SCALE
    s = jnp.where(mask[None, None], s, -jnp.inf)
    p = jax.nn.softmax(s, axis=-1).astype(ckv.dtype)
    return jnp.einsum('bhqk,bkr->bqhr', p, ckv)


def prompt_mla(q_lat, q_rope, ckv, krope):
    b, s = q_lat.shape[:2]
    nb = s // Q_BLOCK
    ql = q_lat.reshape(b, nb, Q_BLOCK, MLA_HEADS, KV_RANK).swapaxes(0, 1)
    qr = q_rope.reshape(b, nb, Q_BLOCK, MLA_HEADS, QK_ROPE).swapaxes(0, 1)
    k_pos = jnp.arange(s)

    def block(args):
        i, ql_b, qr_b = args
        q_pos = i * Q_BLOCK + jnp.arange(Q_BLOCK)
        return mla_core(ql_b, qr_b, ckv, krope, q_pos[:, None] >= k_pos[None, :])

    o = lax.map(block, (jnp.arange(nb), ql, qr))
    return o.swapaxes(0, 1).reshape(b, s, MLA_HEADS, KV_RANK)


def gmlp_mix(u, v, gm_ws, gm_bs):
    t = v.shape[2]
    w = gm_ws[:, :t, :t] * jnp.tril(jnp.ones((t, t), gm_ws.dtype))
    s = jnp.einsum('gts,bcsgd->bctgd', w, v) + gm_bs[:, :t].T[None, None, :, :, None]
    return u * s


def mixer_output(o_lat, o_gm, w_uv, attn_out_g, gm_out_g, w_out):
    o_mla = jnp.einsum('bshr,rhd->bshd', o_lat, w_uv)
    o_mla = o_mla.reshape(o_mla.shape[0], o_mla.shape[1], MLA_WIDTH)
    y = jnp.concatenate([rms_norm(o_mla, attn_out_g), rms_norm(o_gm, gm_out_g)], axis=-1)
    return y @ w_out


def memory_kv(mem, w_mk, w_mv):
    return jnp.einsum('bmd,dhe->bmhe', mem, w_mk), jnp.einsum('bmd,dhe->bmhe', mem, w_mv)


def memory_attend(x, mk, mv, w_xq, w_xo):
    q = jnp.einsum('bsd,dhe->bshe', x, w_xq)
    s = jnp.einsum('bshe,bmhe->bhsm', q, mk).astype(jnp.float32) * (X_HEAD_DIM ** -0.5)
    p = jax.nn.softmax(s, axis=-1).astype(mv.dtype)
    o = jnp.einsum('bhsm,bmhe->bshe', p, mv)
    return jnp.einsum('bshe,hed->bsd', o, w_xo)


def peer_block(xb, w_pq, peer_keys, peer_u, peer_v):
    t = xb.shape[0]
    q = jnp.einsum('td,dhk->thk', xb, w_pq).reshape(t, PEER_HEADS, 2, PEER_HALF)
    s = jnp.einsum('thcd,hcnd->thcn', q, peer_keys).astype(jnp.float32)
    s_top, i_top = lax.top_k(s, PEER_TOPK)
    cand = (s_top[..., 0, :, None] + s_top[..., 1, None, :]).reshape(t, PEER_HEADS, PEER_TOPK * PEER_TOPK)
    c_top, c_idx = lax.top_k(cand, PEER_TOPK)
    ia = jnp.take_along_axis(i_top[..., 0, :], c_idx // PEER_TOPK, axis=-1)
    ib = jnp.take_along_axis(i_top[..., 1, :], c_idx % PEER_TOPK, axis=-1)
    expert = ia * N_KEYS + ib
    g = jax.nn.softmax(c_top, axis=-1).astype(xb.dtype)
    h = jax.nn.gelu(jnp.einsum('thkd,td->thk', peer_u[expert], xb))
    return jnp.einsum('thk,thkd->td', g * h, peer_v[expert])


def peer_ffn(x, w_pq, peer_keys, peer_u, peer_v):
    lead = x.shape[:-1]
    xt = x.reshape(-1, D_MODEL)
    n = xt.shape[0]
    pad = (-n) % PEER_BLOCK
    xt = jnp.pad(xt, ((0, pad), (0, 0))).reshape(-1, PEER_BLOCK, D_MODEL)
    out = lax.map(lambda xb: peer_block(xb, w_pq, peer_keys, peer_u, peer_v), xt)
    return out.reshape(-1, D_MODEL)[:n].reshape(lead + (D_MODEL,))


def post_mixer(x, mix, mk, mv, ln1_g, ln1_b, w_xq, w_xo, ln2_g, ln2_b,
               w_pq, peer_keys, peer_u, peer_v, ln3_g, ln3_b):
    x1 = layer_norm(ALPHA * x + mix, ln1_g, ln1_b)
    x2 = layer_norm(ALPHA * x1 + memory_attend(x1, mk, mv, w_xq, w_xo), ln2_g, ln2_b)
    return layer_norm(ALPHA * x2 + peer_ffn(x2, w_pq, peer_keys, peer_u, peer_v), ln3_g, ln3_b)


def setup_inputs(seed: int = 0) -> dict:
    key = jax.random.key(seed)
    ks = iter(jax.random.split(key, 48))

    def nrm(shape, scale):
        return jax.random.normal(next(ks), shape, jnp.float32) * scale

    def gain(shape):
        return 1.0 + 0.01 * jax.random.normal(next(ks), shape, jnp.float32)

    n_pages = PAST_LEN // PAGE_SIZE
    n_used = DEC_BATCH * n_pages
    n_pool = n_used + n_used // 4
    inp = {}
    inp['x_prompt'] = nrm((BATCH, SEQ, D_MODEL), 1.0)
    inp['x_sample'] = nrm((DEC_BATCH, DEC_SEQ, D_MODEL), 1.0)
    inp['mem_prompt'] = nrm((BATCH, MEM_TOKENS, D_MODEL), 1.0)
    inp['cache_ckv'] = nrm((n_pool, PAGE_SIZE, KV_RANK), 1.0)
    inp['cache_krope'] = nrm((n_pool, PAGE_SIZE, QK_ROPE), 1.0)
    inp['cache_mem_k'] = nrm((DEC_BATCH, MEM_TOKENS, X_HEADS, X_HEAD_DIM), 1.0)
    inp['cache_mem_v'] = nrm((DEC_BATCH, MEM_TOKENS, X_HEADS, X_HEAD_DIM), BETA)
    inp['page_table'] = jax.random.permutation(next(ks), n_pool)[:n_used].reshape(DEC_BATCH, n_pages).astype(jnp.int32)
    inp['w_in'] = nrm((D_MODEL, IN_COLS), D_MODEL ** -0.5)
    inp['q_norm_g'] = gain((Q_RANK,))
    inp['kv_norm_g'] = gain((KV_RANK,))
    inp['w_uq'] = nrm((Q_RANK, MLA_HEADS, QK_NOPE + QK_ROPE), Q_RANK ** -0.5)
    inp['w_uk'] = nrm((KV_RANK, MLA_HEADS, QK_NOPE), KV_RANK ** -0.5)
    inp['w_uv'] = nrm((KV_RANK, MLA_HEADS, V_HEAD), KV_RANK ** -0.5)
    inp['gm_norm_g'] = gain((GM_WIDTH,))
    inp['gm_norm_b'] = nrm((GM_WIDTH,), 0.01)
    inp['gm_ws'] = nrm((GM_GROUPS, GM_CHUNK, GM_CHUNK), 0.5 * GM_CHUNK ** -0.5)
    inp['gm_bs'] = gain((GM_GROUPS, GM_CHUNK))
    inp['attn_out_g'] = gain((MLA_WIDTH,))
    inp['gm_out_g'] = gain((GM_WIDTH,))
    inp['w_out'] = nrm((MIX_WIDTH, D_MODEL), BETA * MIX_WIDTH ** -0.5)
    inp['ln1_g'] = gain((D_MODEL,))
    inp['ln1_b'] = nrm((D_MODEL,), 0.01)
    inp['w_xq'] = nrm((D_MODEL, X_HEADS, X_HEAD_DIM), D_MODEL ** -0.5)
    inp['w_mk'] = nrm((D_MODEL, X_HEADS, X_HEAD_DIM), D_MODEL ** -0.5)
    inp['w_mv'] = nrm((D_MODEL, X_HEADS, X_HEAD_DIM), BETA * D_MODEL ** -0.5)
    inp['w_xo'] = nrm((X_HEADS, X_HEAD_DIM, D_MODEL), BETA * (X_HEADS * X_HEAD_DIM) ** -0.5)
    inp['ln2_g'] = gain((D_MODEL,))
    inp['ln2_b'] = nrm((D_MODEL,), 0.01)
    inp['w_pq'] = nrm((D_MODEL, PEER_HEADS, PEER_DK), D_MODEL ** -0.5)
    inp['peer_keys'] = nrm((PEER_HEADS, 2, N_KEYS, PEER_HALF), PEER_HALF ** -0.5)
    inp['peer_u'] = nrm((N_EXPERTS, D_MODEL), D_MODEL ** -0.5)
    inp['peer_v'] = nrm((N_EXPERTS, D_MODEL), BETA)
    inp['ln3_g'] = gain((D_MODEL,))
    inp['ln3_b'] = nrm((D_MODEL,), 0.01)
    return inp


def reference(x_prompt, x_sample, mem_prompt, cache_ckv, cache_krope, cache_mem_k, cache_mem_v,
              page_table, w_in, q_norm_g, kv_norm_g, w_uq, w_uk, w_uv, gm_norm_g, gm_norm_b,
              gm_ws, gm_bs, attn_out_g, gm_out_g, w_out, ln1_g, ln1_b, w_xq, w_mk, w_mv, w_xo,
              ln2_g, ln2_b, w_pq, peer_keys, peer_u, peer_v, ln3_g, ln3_b):
    for _layer in range(DEPTH):
        b_p, s_p = x_prompt.shape[:2]
        pos_p = jnp.arange(s_p)
        q_lat, q_rope, ckv_p, krope_p, u_p, v_p = mixer_inputs(
            x_prompt, pos_p, w_in, q_norm_g, kv_norm_g, w_uq, w_uk, gm_norm_g, gm_norm_b)
        o_lat_p = prompt_mla(q_lat, q_rope, ckv_p, krope_p)
        shp = (b_p, s_p // GM_CHUNK, GM_CHUNK, GM_GROUPS, GM_GROUP_DIM)
        o_gm_p = gmlp_mix(u_p.reshape(shp), v_p.reshape(shp), gm_ws, gm_bs).reshape(b_p, s_p, GM_WIDTH)
        mix_p = mixer_output(o_lat_p, o_gm_p, w_uv, attn_out_g, gm_out_g, w_out)
        mk_p, mv_p = memory_kv(mem_prompt, w_mk, w_mv)
        y_prompt = post_mixer(x_prompt, mix_p, mk_p, mv_p, ln1_g, ln1_b, w_xq, w_xo, ln2_g, ln2_b,
                              w_pq, peer_keys, peer_u, peer_v, ln3_g, ln3_b)

        b_s, s_s = x_sample.shape[:2]
        pos_s = PAST_LEN + jnp.arange(s_s)
        q_lat, q_rope, ckv_s, krope_s, u_s, v_s = mixer_inputs(
            x_sample, pos_s, w_in, q_norm_g, kv_norm_g, w_uq, w_uk, gm_norm_g, gm_norm_b)
        ckv_all = jnp.concatenate([cache_ckv[page_table].reshape(b_s, -1, KV_RANK), ckv_s], axis=1)
        krope_all = jnp.concatenate([cache_krope[page_table].reshape(b_s, -1, QK_ROPE), krope_s], axis=1)
        k_pos = jnp.arange(ckv_all.shape[1])
        o_lat_s = mla_core(q_lat, q_rope, ckv_all, krope_all, pos_s[:, None] >= k_pos[None, :])
        shs = (b_s, 1, s_s, GM_GROUPS, GM_GROUP_DIM)
        o_gm_s = gmlp_mix(u_s.reshape(shs), v_s.reshape(shs), gm_ws, gm_bs).reshape(b_s, s_s, GM_WIDTH)
        mix_s = mixer_output(o_lat_s, o_gm_s, w_uv, attn_out_g, gm_out_g, w_out)
        y_sample = post_mixer(x_sample, mix_s, cache_mem_k, cache_mem_v, ln1_g, ln1_b, w_xq, w_xo,
                              ln2_g, ln2_b, w_pq, peer_keys, peer_u, peer_v, ln3_g, ln3_b)
    return (y_prompt, y_sample, ckv_p, krope_p, mk_p, mv_p, ckv_s, krope_s, v_s)
```

```python
import functools

import jax
import jax.numpy as jnp
import numpy as np
from jax import lax
from jax.experimental import pallas as pl
from jax.experimental.pallas import tpu as pltpu

F32 = jnp.float32
BF16 = jnp.bfloat16

D_MODEL = 1024
MLA_HEADS = 8
QK_NOPE = 64
QK_ROPE = 32
V_HEAD = 64
Q_RANK = 256
KV_RANK = 128
MLA_WIDTH = MLA_HEADS * V_HEAD
MLA_SCALE = (QK_NOPE + QK_ROPE) ** -0.5
ROPE_THETA = 10000.0
GM_WIDTH = D_MODEL // 2
GM_GROUPS = 4
GM_CHUNK = 128
MEM_TOKENS = 256
X_HEADS = 4
X_HEAD_DIM = 128
X_WIDTH = X_HEADS * X_HEAD_DIM
X_SCALE = X_HEAD_DIM ** -0.5
PEER_HEADS = 8
N_KEYS = 128
N_EXPERTS = N_KEYS * N_KEYS
PEER_TOPK = 16
PEER_DK = 256
PEER_HALF = PEER_DK // 2
PAGE_SIZE = 128
DEPTH = 1
ALPHA = (2.0 * DEPTH) ** 0.25
EPS = 1e-5

LANES = 128
VMEM_LIMIT = 56 * 1024 * 1024

CAND_NB = tuple(PEER_TOPK // (ka + 1) for ka in range(PEER_TOPK))
CAND_OFF = tuple(int(sum(CAND_NB[:ka])) for ka in range(PEER_TOPK))
CAND_N = int(sum(CAND_NB))
CAND_ROWS = 56
NEG_INF = float("-inf")


def _dot(a, b):
    return jnp.dot(a, b, preferred_element_type=F32)


def _dot_nt(a, b):
    return lax.dot_general(a, b, (((1,), (1,)), ((), ())), preferred_element_type=F32)


def _dot_tn(a, b):
    return lax.dot_general(a, b, (((0,), (0,)), ((), ())), preferred_element_type=F32)


def _layer_norm(x, g, b):
    mu = jnp.mean(x, -1, keepdims=True)
    var = jnp.mean(jnp.square(x - mu), -1, keepdims=True)
    return (x - mu) * lax.rsqrt(var + EPS) * g + b


def _rms_norm(x, g):
    return x * lax.rsqrt(jnp.mean(jnp.square(x), -1, keepdims=True) + EPS) * g


def _params(*sem):
    return pltpu.CompilerParams(dimension_semantics=sem, vmem_limit_bytes=VMEM_LIMIT)


def _full(shape):
    n = len(shape)
    return pl.BlockSpec(shape, lambda *_: (0,) * n)


def _rope(x, c, s_lo, s_hi):
    width = x.shape[-1]
    return x * c + pltpu.roll(x, width - 16, 1) * s_lo + pltpu.roll(x, 16, 1) * s_hi


def _proj_kernel(x_ref, c_ref, slo_ref, shi_ref, w_in_ref, qg_ref, kvg_ref, w_uq_ref, w_uk_ref,
                 gmg_ref, gmb_ref, qlat_ref, qrope_ref, kcat_ref, ckv_ref, krope_ref, u_ref, v_ref,
                 *maybe_ckvt_ref):
    h = _dot(x_ref[...].astype(BF16), w_in_ref[...])
    c, s_lo, s_hi = c_ref[...], slo_ref[...], shi_ref[...]

    cq = _rms_norm(h[:, :Q_RANK], qg_ref[...])
    q_all = _dot(cq.astype(BF16), w_uq_ref[...])
    q_nope = q_all[:, :MLA_HEADS * QK_NOPE]
    q_lat = _dot(q_nope.astype(BF16), w_uk_ref[...])
    qlat_ref[...] = (q_lat * MLA_SCALE).astype(BF16)
    q_rope = _rope(q_all[:, MLA_HEADS * QK_NOPE:], jnp.tile(c, (1, MLA_HEADS)),
                   jnp.tile(s_lo, (1, MLA_HEADS)), jnp.tile(s_hi, (1, MLA_HEADS)))
    qrope_ref[...] = (q_rope * MLA_SCALE).astype(BF16)

    ckv = _rms_norm(h[:, Q_RANK:Q_RANK + KV_RANK], kvg_ref[...])
    ckv_ref[...] = ckv
    k_rot = _rope(h[:, Q_RANK + KV_RANK:Q_RANK + KV_RANK + LANES], c, s_lo, s_hi)
    krope_ref[...] = k_rot[:, :QK_ROPE]
    kcat_ref[...] = jnp.concatenate([ckv, k_rot], axis=1).astype(BF16)
    for ckvt_ref in maybe_ckvt_ref:
        for j in range(ckvt_ref.shape[0]):
            ckvt_ref[j] = ckv[j * 256:(j + 1) * 256, :].T.astype(BF16)

    uv = jax.nn.gelu(h[:, Q_RANK + KV_RANK + LANES:])
    u_ref[...] = uv[:, :GM_WIDTH]
    v_ref[...] = _layer_norm(uv[:, GM_WIDTH:], gmg_ref[...], gmb_ref[...])


def _proj(x, tables, wts, *, tm, seq_blocks, emit_kt):
    n = x.shape[0]
    row = lambda w: pl.BlockSpec((tm, w), lambda i: (i, 0))
    tab = pl.BlockSpec((tm, LANES), lambda i: (i % seq_blocks, 0))
    w_in, qg, kvg, w_uq, w_uk, gmg, gmb = wts
    out_shape = [
        jax.ShapeDtypeStruct((n, MLA_HEADS * KV_RANK), BF16),
        jax.ShapeDtypeStruct((n, MLA_HEADS * LANES), BF16),
        jax.ShapeDtypeStruct((n, 2 * LANES), BF16),
        jax.ShapeDtypeStruct((n, KV_RANK), F32),
        jax.ShapeDtypeStruct((n, QK_ROPE), F32),
        jax.ShapeDtypeStruct((n, GM_WIDTH), F32),
        jax.ShapeDtypeStruct((n, GM_WIDTH), F32),
    ]
    out_specs = [row(MLA_HEADS * KV_RANK), row(MLA_HEADS * LANES), row(2 * LANES), row(KV_RANK),
                 row(QK_ROPE), row(GM_WIDTH), row(GM_WIDTH)]
    if emit_kt:
        out_shape.append(jax.ShapeDtypeStruct((n // 256, KV_RANK, 256), BF16))
        out_specs.append(pl.BlockSpec((tm // 256, KV_RANK, 256), lambda i: (i, 0, 0)))
    return pl.pallas_call(
        _proj_kernel,
        grid=(n // tm,),
        in_specs=[row(D_MODEL), tab, tab, tab, _full(w_in.shape), _full(qg.shape), _full(kvg.shape),
                  _full(w_uq.shape), _full(w_uk.shape), _full(gmg.shape), _full(gmb.shape)],
        out_specs=out_specs,
        out_shape=out_shape,
        compiler_params=_params("parallel"),
        name="proj",
    )(x, *tables, w_in, qg, kvg, w_uq, w_uk, gmg, gmb)


Q_BLK = 128
KV_BLK = 256


def _attn_kernel(qlat_ref, qrope_ref, kcat_ref, ckvt_ref, wuvt_ref, g_ref, a_ref, m_s, l_s, acc_s):
    qi = pl.program_id(1)
    ql, qr = qlat_ref[...], qrope_ref[...]
    qcat = jnp.concatenate(
        [jnp.concatenate([ql[:, h * LANES:(h + 1) * LANES], qr[:, h * LANES:(h + 1) * LANES]], axis=1)
         for h in range(MLA_HEADS)], axis=0)
    cols = MLA_HEADS * Q_BLK
    m_s[...] = jnp.full((1, cols), NEG_INF, F32)
    l_s[...] = jnp.zeros((1, cols), F32)
    acc_s[...] = jnp.zeros((KV_RANK, cols), F32)
    q_pos = qi * Q_BLK + (lax.broadcasted_iota(jnp.int32, (KV_BLK, cols), 1) & (Q_BLK - 1))
    k_off = lax.broadcasted_iota(jnp.int32, (KV_BLK, cols), 0)

    def step(j, carry):
        k = kcat_ref[pl.ds(pl.multiple_of(j * KV_BLK, KV_BLK), KV_BLK), :]
        st = _dot_nt(k, qcat)
        st = jnp.where(q_pos >= k_off + j * KV_BLK, st, NEG_INF)
        m_old = m_s[...]
        m_new = jnp.maximum(m_old, jnp.max(st, axis=0, keepdims=True))
        alpha = jnp.exp(m_old - m_new)
        p = jnp.exp(st - m_new)
        l_s[...] = alpha * l_s[...] + jnp.sum(p, axis=0, keepdims=True)
        acc_s[...] = alpha * acc_s[...] + _dot(ckvt_ref[j], p.astype(BF16))
        m_s[...] = m_new
        return carry

    lax.fori_loop(0, (qi * Q_BLK) // KV_BLK + 1, step, 0)

    o_t = (acc_s[...] / l_s[...]).astype(BF16)
    om_t = jnp.concatenate(
        [_dot(wuvt_ref[h], o_t[:, h * Q_BLK:(h + 1) * Q_BLK]) for h in range(MLA_HEADS)], axis=0)
    ms = jnp.mean(jnp.square(om_t), axis=0, keepdims=True)
    a_t = om_t * lax.rsqrt(ms + EPS) * g_ref[...]
    a_ref[...] = a_t.T.astype(BF16)


def _attn_prompt(qlat, qrope, kcat, ckvt, wuvt, g_attn, *, batch, seq):
    n = batch * seq
    nq = seq // Q_BLK
    cols = MLA_HEADS * Q_BLK
    return pl.pallas_call(
        _attn_kernel,
        grid=(batch, nq),
        in_specs=[
            pl.BlockSpec((Q_BLK, MLA_HEADS * KV_RANK), lambda b, i: (b * nq + i, 0)),
            pl.BlockSpec((Q_BLK, MLA_HEADS * LANES), lambda b, i: (b * nq + i, 0)),
            pl.BlockSpec((None, seq, 2 * LANES), lambda b, i: (b, 0, 0)),
            pl.BlockSpec((None, seq // KV_BLK, KV_RANK, KV_BLK), lambda b, i: (b, 0, 0, 0)),
            _full(wuvt.shape), _full(g_attn.shape),
        ],
        out_specs=pl.BlockSpec((Q_BLK, MLA_WIDTH), lambda b, i: (b * nq + i, 0)),
        out_shape=jax.ShapeDtypeStruct((n, MLA_WIDTH), BF16),
        scratch_shapes=[pltpu.VMEM((1, cols), F32), pltpu.VMEM((1, cols), F32),
                        pltpu.VMEM((KV_RANK, cols), F32)],
        compiler_params=_params("parallel", "arbitrary"),
        name="attn_prompt",
    )(qlat, qrope, kcat.reshape(batch, seq, 2 * LANES),
      ckvt.reshape(batch, seq // KV_BLK, KV_RANK, KV_BLK), wuvt, g_attn)


PAGES_PER_STEP = 8


def _decode_kernel(pt_ref, qlat_ref, qrope_ref, *refs):
    del pt_ref
    np_ = PAGES_PER_STEP
    ckv_refs, kr_refs = refs[:np_], refs[np_:2 * np_]
    ckvn_ref, krn_ref, o_ref, m_s, l_s, acc_s = refs[2 * np_:]
    c = pl.program_id(1)

    @pl.when(c == 0)
    def _():
        m_s[...] = jnp.full(m_s.shape, NEG_INF, F32)
        l_s[...] = jnp.zeros(l_s.shape, F32)
        acc_s[...] = jnp.zeros(acc_s.shape, F32)

    ql, qr = qlat_ref[...], qrope_ref[...]

    def update(kc, kr, valid_rows):
        st = _dot_nt(kc, ql) + _dot_nt(kr, qr)
        if valid_rows is not None:
            st = jnp.where(lax.broadcasted_iota(jnp.int32, st.shape, 0) < valid_rows, st, NEG_INF)
        m_old = m_s[...]
        m_new = jnp.maximum(m_old, jnp.max(st, axis=0, keepdims=True))
        alpha = jnp.exp(m_old - m_new)
        p = jnp.exp(st - m_new)
        l_s[...] = alpha * l_s[...] + jnp.sum(p, axis=0, keepdims=True)
        acc_s[...] = alpha * acc_s[...] + _dot_tn(kc, p.astype(BF16))
        m_s[...] = m_new

    kc = jnp.concatenate([r[...] for r in ckv_refs], axis=0).astype(BF16)
    kr = jnp.concatenate([r[...] for r in kr_refs], axis=0).astype(BF16)
    update(kc, kr, None)

    @pl.when(c == pl.num_programs(1) - 1)
    def _():
        kn = jnp.broadcast_to(ckvn_ref[...], (16, KV_RANK)).astype(BF16)
        krn = jnp.broadcast_to(krn_ref[...], (16, QK_ROPE)).astype(BF16)
        update(kn, krn, 1)
        o_t = acc_s[...] / l_s[...]
        o_ref[...] = o_t.T[:MLA_HEADS, :]


def _attn_decode(page_table, qlat_pad, qrope_pad, cache_ckv, cache_krope, ckv_new, krope_new):
    nb, n_pages = page_table.shape
    steps = n_pages // PAGES_PER_STEP

    def page_spec(width, i):
        return pl.BlockSpec((None, PAGE_SIZE, width),
                            lambda b, c, pt: (pt[b, c * PAGES_PER_STEP + i], 0, 0))

    in_specs = [pl.BlockSpec((None, LANES, KV_RANK), lambda b, c, pt: (b, 0, 0)),
                pl.BlockSpec((None, LANES, QK_ROPE), lambda b, c, pt: (b, 0, 0))]
    in_specs += [page_spec(KV_RANK, i) for i in range(PAGES_PER_STEP)]
    in_specs += [page_spec(QK_ROPE, i) for i in range(PAGES_PER_STEP)]
    in_specs += [pl.BlockSpec((None, 1, KV_RANK), lambda b, c, pt: (b, 0, 0)),
                 pl.BlockSpec((None, 1, QK_ROPE), lambda b, c, pt: (b, 0, 0))]
    return pl.pallas_call(
        _decode_kernel,
        grid_spec=pltpu.PrefetchScalarGridSpec(
            num_scalar_prefetch=1,
            grid=(nb, steps),
            in_specs=in_specs,
            out_specs=pl.BlockSpec((None, MLA_HEADS, KV_RANK), lambda b, c, pt: (b, 0, 0)),
            scratch_shapes=[pltpu.VMEM((1, LANES), F32), pltpu.VMEM((1, LANES), F32),
                            pltpu.VMEM((KV_RANK, LANES), F32)],
        ),
        out_shape=jax.ShapeDtypeStruct((nb, MLA_HEADS, KV_RANK), F32),
        compiler_params=_params("parallel", "arbitrary"),
        name="attn_decode",
    )(page_table, qlat_pad, qrope_pad, *([cache_ckv] * PAGES_PER_STEP), *([cache_krope] * PAGES_PER_STEP),
      ckv_new.reshape(nb, 1, KV_RANK), krope_new.reshape(nb, 1, QK_ROPE))


def _memkv_kernel(mem_ref, wk_ref, wv_ref, mk_ref, mv_ref):
    m = mem_ref[...].astype(BF16)
    mk_ref[...] = _dot(m, wk_ref[...])
    mv_ref[...] = _dot(m, wv_ref[...])


def _memkv(mem, w_mk, w_mv):
    n = mem.shape[0]
    tm = 512
    return pl.pallas_call(
        _memkv_kernel,
        grid=(n // tm,),
        in_specs=[pl.BlockSpec((tm, D_MODEL), lambda i: (i, 0)), _full(w_mk.shape), _full(w_mv.shape)],
        out_specs=[pl.BlockSpec((tm, X_WIDTH), lambda i: (i, 0))] * 2,
        out_shape=[jax.ShapeDtypeStruct((n, X_WIDTH), F32)] * 2,
        compiler_params=_params("parallel"),
        name="memkv",
    )(mem, w_mk, w_mv)


POST_ROWS = 256


def _softmax_rows(s):
    e = jnp.exp(s - jnp.max(s, axis=-1, keepdims=True))
    return e / jnp.sum(e, axis=-1, keepdims=True)


def _mix_and_ln1(x, a_bf, o_gm, gmog, w_out, ln1g, ln1b):
    gm_n = _rms_norm(o_gm, gmog)
    y = jnp.concatenate([a_bf, gm_n.astype(BF16)], axis=1)
    return _layer_norm(ALPHA * x + _dot(y, w_out), ln1g, ln1b)


def _post_prompt_kernel(x_ref, a_ref, u_ref, v_ref, ws_ref, bias_ref, gmog_ref, wout_ref, ln1g_ref, ln1b_ref,
                        wxq_ref, mk_ref, mv_ref, wxo_ref, ln2g_ref, ln2b_ref, x2_ref, x2t_ref):
    tril = (lax.broadcasted_iota(jnp.int32, (GM_CHUNK, GM_CHUNK), 0)
            >= lax.broadcasted_iota(jnp.int32, (GM_CHUNK, GM_CHUNK), 1))
    w_s = [jnp.where(tril, ws_ref[g], 0.0).astype(BF16) for g in range(GM_GROUPS)]
    chunks = []
    for c in range(POST_ROWS // GM_CHUNK):
        rows = slice(c * GM_CHUNK, (c + 1) * GM_CHUNK)
        v_c = v_ref[rows, :].astype(BF16)
        s = jnp.concatenate([_dot(w_s[g], v_c[:, g * LANES:(g + 1) * LANES]) for g in range(GM_GROUPS)],
                            axis=1) + bias_ref[...]
        chunks.append(u_ref[rows, :] * s)
    o_gm = jnp.concatenate(chunks, axis=0)
    x1 = _mix_and_ln1(x_ref[...], a_ref[...], o_gm, gmog_ref[...], wout_ref[...], ln1g_ref[...], ln1b_ref[...])

    q = _dot(x1.astype(BF16), wxq_ref[...]).astype(BF16)
    mk, mv = mk_ref[...].astype(BF16), mv_ref[...].astype(BF16)
    heads = []
    for h in range(X_HEADS):
        cs = slice(h * X_HEAD_DIM, (h + 1) * X_HEAD_DIM)
        p = _softmax_rows(_dot_nt(q[:, cs], mk[:, cs]) * X_SCALE)
        heads.append(_dot(p.astype(BF16), mv[:, cs]))
    o = jnp.concatenate(heads, axis=1).astype(BF16)
    x2 = _layer_norm(ALPHA * x1 + _dot(o, wxo_ref[...]), ln2g_ref[...], ln2b_ref[...])
    x2_ref[...] = x2
    x2t_ref[...] = x2.T.astype(BF16)


def _post_prompt(x, a, u, v, mk, mv, wts, *, batch, seq):
    n = batch * seq
    nb = seq // POST_ROWS
    row = lambda w: pl.BlockSpec((POST_ROWS, w), lambda b, i: (b * nb + i, 0))
    mem = pl.BlockSpec((MEM_TOKENS, X_WIDTH), lambda b, i: (b, 0))
    return pl.pallas_call(
        _post_prompt_kernel,
        grid=(batch, nb),
        in_specs=[row(D_MODEL), row(MLA_WIDTH), row(GM_WIDTH), row(GM_WIDTH)]
                 + [_full(w.shape) for w in wts[:6]] + [_full(wts[6].shape), mem, mem]
                 + [_full(w.shape) for w in wts[7:]],
        out_specs=[row(D_MODEL), pl.BlockSpec((D_MODEL, POST_ROWS), lambda b, i: (0, b * nb + i))],
        out_shape=[jax.ShapeDtypeStruct((n, D_MODEL), F32), jax.ShapeDtypeStruct((D_MODEL, n), BF16)],
        compiler_params=_params("parallel", "parallel"),
        name="post_prompt",
    )(x, a, u, v, *wts[:7], mk, mv, *wts[7:])


SAMPLE_ROWS = 8


def _post_sample_kernel(x_ref, o_ref, u_ref, v_ref, wuv_ref, ag_ref, ws0_ref, bs0_ref, gmog_ref, wout_ref,
                        ln1g_ref, ln1b_ref, wxq_ref, mk_ref, mv_ref, wxo_ref, ln2g_ref, ln2b_ref, x2_ref):
    o_mla = _dot(o_ref[...].astype(BF16), wuv_ref[...])
    a = _rms_norm(o_mla, ag_ref[...]).astype(BF16)
    o_gm = u_ref[...] * (ws0_ref[...] * v_ref[...] + bs0_ref[...])
    x1 = _mix_and_ln1(x_ref[...], a, o_gm, gmog_ref[...], wout_ref[...], ln1g_ref[...], ln1b_ref[...])

    q = _dot(x1.astype(BF16), wxq_ref[...])
    lane_head = lax.broadcasted_iota(jnp.int32, (LANES, X_WIDTH), 1) // X_HEAD_DIM
    on_head = lane_head == lax.broadcasted_iota(jnp.int32, (LANES, X_WIDTH), 0)
    rows = []
    for j in range(SAMPLE_ROWS):
        q_bd = jnp.where(on_head, q[j:j + 1, :], 0.0).astype(BF16)
        s = _dot_nt(mk_ref[j].astype(BF16), q_bd) * X_SCALE
        e = jnp.exp(s - jnp.max(s, axis=0, keepdims=True))
        p = e / jnp.sum(e, axis=0, keepdims=True)
        o_all = _dot_tn(p.astype(BF16), mv_ref[j].astype(BF16))
        rows.append(jnp.sum(jnp.where(on_head, o_all, 0.0), axis=0, keepdims=True))
    o = jnp.concatenate(rows, axis=0).astype(BF16)
    x2_ref[...] = _layer_norm(ALPHA * x1 + _dot(o, wxo_ref[...]), ln2g_ref[...], ln2b_ref[...])


def _post_sample(x, o_lat, u, v, mk, mv, wts):
    n = x.shape[0]
    row = lambda w: pl.BlockSpec((SAMPLE_ROWS, w), lambda i: (i, 0))
    mem = pl.BlockSpec((SAMPLE_ROWS, MEM_TOKENS, X_WIDTH), lambda i: (i, 0, 0))
    return pl.pallas_call(
        _post_sample_kernel,
        grid=(n // SAMPLE_ROWS,),
        in_specs=[row(D_MODEL), row(MLA_HEADS * KV_RANK), row(GM_WIDTH), row(GM_WIDTH)]
                 + [_full(w.shape) for w in wts[:9]] + [mem, mem] + [_full(w.shape) for w in wts[9:]],
        out_specs=row(D_MODEL),
        out_shape=jax.ShapeDtypeStruct((n, D_MODEL), F32),
        compiler_params=_params("parallel"),
        name="post_sample",
    )(x, o_lat, u, v, *wts[:9], mk, mv, *wts[9:])


def _top16(val, row_id):
    rank = jnp.full(val.shape, 127.0, F32)
    tops = []
    for k in range(PEER_TOPK):
        m = jnp.max(val, axis=0, keepdims=True)
        first = jnp.min(jnp.where(val == m, row_id, 1e9), axis=0, keepdims=True)
        hit = row_id == first
        val = jnp.where(hit, NEG_INF, val)
        rank = jnp.where(hit, float(k), rank)
        tops.append(m)
    return tops, rank


def _peer_topk_kernel(x2t_ref, wpqt_ref, keys_ref, flat_ref, rb_ref, na_ref, ea_ref, eb_ref,
                      qt_s, s_s, rank_s, top_s, cand_s, sel_s):
    nsub = x2t_ref.shape[1] // LANES
    qt = _dot(wpqt_ref[...], x2t_ref[...]).astype(BF16)
    for sub in range(nsub):
        qt_s[sub] = qt[:, sub * LANES:(sub + 1) * LANES]
    key_id = lax.broadcasted_iota(jnp.int32, (N_KEYS, LANES), 0).astype(F32)

    def level1(it, carry):
        hc, sub = it // nsub, it % nsub
        q_blk = qt_s[sub, pl.ds(pl.multiple_of(hc * PEER_HALF, PEER_HALF), PEER_HALF), :]
        s = _dot(keys_ref[hc], q_blk)
        tops, rank = _top16(s, key_id)
        s_s[hc, sub] = s
        rank_s[hc, sub] = rank
        for k in range(PEER_TOPK):
            top_s[hc, sub, k:k + 1, :] = tops[k]
        return carry

    lax.fori_loop(0, 2 * PEER_HEADS * nsub, level1, 0)

    flat = flat_ref[...]

    def level2(it, carry):
        h, sub = it // nsub, it % nsub
        sa, sb = top_s[2 * h, sub], top_s[2 * h + 1, sub]
        ea_r = jnp.exp(sa - sa[0:1, :])
        eb_r = jnp.exp(sb - sb[0:1, :])
        for ka in range(PEER_TOPK):
            cand_s[CAND_OFF[ka]:CAND_OFF[ka] + CAND_NB[ka], :] = sa[ka:ka + 1, :] + sb[0:CAND_NB[ka], :]
        cand_s[CAND_N:CAND_ROWS, :] = jnp.full((CAND_ROWS - CAND_N, LANES), NEG_INF, F32)
        _, crank = _top16(cand_s[...], flat)
        sel_s[...] = jnp.where(crank < float(PEER_TOPK), 1.0, 0.0)
        n_a, z = [], jnp.zeros((1, LANES), F32)
        for ka in range(PEER_TOPK):
            sel_ka = sel_s[CAND_OFF[ka]:CAND_OFF[ka] + CAND_NB[ka], :]
            n_a.append(jnp.sum(sel_ka, axis=0, keepdims=True))
            z = z + ea_r[ka:ka + 1, :] * jnp.sum(sel_ka * eb_r[0:CAND_NB[ka], :], axis=0, keepdims=True)
        rank_a = rank_s[2 * h, sub]
        na = jnp.zeros((N_KEYS, LANES), F32)
        for ka in range(PEER_TOPK):
            na = jnp.where(rank_a == float(ka), n_a[ka], na)
        na_ref[h, sub] = na
        rb_ref[h, sub] = rank_s[2 * h + 1, sub]
        ea_ref[h, sub] = jnp.exp(s_s[2 * h, sub] - sa[0:1, :])
        eb_ref[h, sub] = jnp.exp(s_s[2 * h + 1, sub] - sb[0:1, :]) / z
        return carry

    lax.fori_loop(0, PEER_HEADS * nsub, level2, 0)


def _peer_topk(x2t, wpqt, keys, flat, *, tt):
    n = x2t.shape[1]
    nsub = tt // LANES
    key_arr = jax.ShapeDtypeStruct((PEER_HEADS, n // LANES, N_KEYS, LANES), F32)
    key_spec = pl.BlockSpec((PEER_HEADS, nsub, N_KEYS, LANES), lambda i: (0, i, 0, 0))
    hc = 2 * PEER_HEADS
    return pl.pallas_call(
        _peer_topk_kernel,
        grid=(n // tt,),
        in_specs=[pl.BlockSpec((D_MODEL, tt), lambda i: (0, i)), _full(wpqt.shape), _full(keys.shape),
                  _full(flat.shape)],
        out_specs=[key_spec] * 4,
        out_shape=[key_arr] * 4,
        scratch_shapes=[pltpu.VMEM((nsub, PEER_HEADS * PEER_DK, LANES), BF16),
                        pltpu.VMEM((hc, nsub, N_KEYS, LANES), F32),
                        pltpu.VMEM((hc, nsub, N_KEYS, LANES), F32),
                        pltpu.VMEM((hc, nsub, PEER_TOPK, LANES), F32),
                        pltpu.VMEM((CAND_ROWS, LANES), F32),
                        pltpu.VMEM((CAND_ROWS, LANES), F32)],
        compiler_params=_params("parallel"),
        name="peer_topk",
    )(x2t, wpqt, keys, flat)


EXP_BLK = 512


def _peer_experts_kernel(x2t_ref, x2_ref, rb_ref, na_ref, ea_ref, eb_ref, u_ref, vt_ref, ln3g_ref, ln3b_ref,
                         y_ref, acc_s, w_s):
    e = pl.program_id(1)
    nsub = x2t_ref.shape[1] // LANES

    @pl.when(e == 0)
    def _():
        acc_s[...] = jnp.zeros(acc_s.shape, F32)

    h_t = jax.nn.gelu(_dot(u_ref[...], x2t_ref[...]))
    for g in range(EXP_BLK // N_KEYS):
        ia = e * (EXP_BLK // N_KEYS) + g
        for sub in range(nsub):
            def head(h, gate):
                na = na_ref[h, sub, pl.ds(ia, 1), :]
                ea = ea_ref[h, sub, pl.ds(ia, 1), :]
                return gate + jnp.where(rb_ref[h, sub] < na, eb_ref[h, sub], 0.0) * ea
            gate = lax.fori_loop(0, PEER_HEADS, head, jnp.zeros((N_KEYS, LANES), F32))
            blk = h_t[g * N_KEYS:(g + 1) * N_KEYS, sub * LANES:(sub + 1) * LANES] * gate
            w_s[g * N_KEYS:(g + 1) * N_KEYS, sub * LANES:(sub + 1) * LANES] = blk.astype(BF16)
    acc_s[...] += _dot(vt_ref[...], w_s[...])

    @pl.when(e == pl.num_programs(1) - 1)
    def _():
        y_ref[...] = _layer_norm(ALPHA * x2_ref[...] + acc_s[...].T, ln3g_ref[...], ln3b_ref[...])


def _peer_experts(x2t, x2, key_arrs, u_bf, vt_bf, ln3g, ln3b, *, tt):
    n = x2.shape[0]
    nsub = tt // LANES
    key_spec = pl.BlockSpec((PEER_HEADS, nsub, N_KEYS, LANES), lambda i, e: (0, i, 0, 0))
    return pl.pallas_call(
        _peer_experts_kernel,
        grid=(n // tt, N_EXPERTS // EXP_BLK),
        in_specs=[pl.BlockSpec((D_MODEL, tt), lambda i, e: (0, i)),
                  pl.BlockSpec((tt, D_MODEL), lambda i, e: (i, 0)),
                  key_spec, key_spec, key_spec, key_spec,
                  pl.BlockSpec((EXP_BLK, D_MODEL), lambda i, e: (e, 0)),
                  pl.BlockSpec((D_MODEL, EXP_BLK), lambda i, e: (0, e)),
                  _full(ln3g.shape), _full(ln3b.shape)],
        out_specs=pl.BlockSpec((tt, D_MODEL), lambda i, e: (i, 0)),
        out_shape=jax.ShapeDtypeStruct((n, D_MODEL), F32),
        scratch_shapes=[pltpu.VMEM((D_MODEL, tt), F32), pltpu.VMEM((EXP_BLK, tt), BF16)],
        compiler_params=_params("parallel", "arbitrary"),
        name="peer_experts",
    )(x2t, x2, *key_arrs, u_bf, vt_bf, ln3g, ln3b)


def _peer(x2t, x2, peer_wts, *, tt):
    wpqt, keys, flat, u_bf, vt_bf, ln3g, ln3b = peer_wts
    key_arrs = _peer_topk(x2t, wpqt, keys, flat, tt=tt)
    return _peer_experts(x2t, x2, key_arrs, u_bf, vt_bf, ln3g, ln3b, tt=tt)


def _rope_tables(pos):
    inv = ROPE_THETA ** (-jnp.arange(0, QK_ROPE, 2, dtype=F32) / QK_ROPE)
    ang = pos.astype(F32)[:, None] * inv[None, :]
    cos, sin, zero = jnp.cos(ang), jnp.sin(ang), jnp.zeros_like(ang)
    pad = jnp.zeros((pos.shape[0], LANES - QK_ROPE), F32)
    return (jnp.concatenate([cos, cos, pad], axis=1),
            jnp.concatenate([-sin, zero, pad], axis=1),
            jnp.concatenate([zero, sin, pad], axis=1))


def _row(v):
    return v.reshape(1, -1).astype(F32)


def kernel(x_prompt, x_sample, mem_prompt, cache_ckv, cache_krope, cache_mem_k, cache_mem_v, page_table,
           w_in, q_norm_g, kv_norm_g, w_uq, w_uk, w_uv, gm_norm_g, gm_norm_b, gm_ws, gm_bs, attn_out_g,
           gm_out_g, w_out, ln1_g, ln1_b, w_xq, w_mk, w_mv, w_xo, ln2_g, ln2_b, w_pq, peer_keys, peer_u,
           peer_v, ln3_g, ln3_b):
    batch, seq = x_prompt.shape[:2]
    nb = x_sample.shape[0]
    past_len = page_table.shape[1] * PAGE_SIZE

    kr_pad = jnp.zeros((D_MODEL, LANES - QK_ROPE), F32)
    w_in_x = jnp.concatenate([w_in[:, :Q_RANK + KV_RANK + QK_ROPE], kr_pad,
                              w_in[:, Q_RANK + KV_RANK + QK_ROPE:]], axis=1).astype(BF16)
    uq_nope = w_uq[:, :, :QK_NOPE].reshape(Q_RANK, MLA_HEADS * QK_NOPE)
    uq_rope = jnp.pad(w_uq[:, :, QK_NOPE:], ((0, 0), (0, 0), (0, LANES - QK_ROPE)))
    w_uq_x = jnp.concatenate([uq_nope, uq_rope.reshape(Q_RANK, MLA_HEADS * LANES)], axis=1).astype(BF16)
    eye = jnp.eye(MLA_HEADS, dtype=F32)
    w_uk_bd = jnp.einsum('rhd,hg->hdgr', w_uk, eye).reshape(MLA_HEADS * QK_NOPE, MLA_HEADS * KV_RANK).astype(BF16)
    w_uv_bd = jnp.einsum('rhd,hg->hrgd', w_uv, eye).reshape(MLA_HEADS * KV_RANK, MLA_WIDTH).astype(BF16)
    w_uv_t = jnp.transpose(w_uv, (1, 2, 0)).astype(BF16)
    proj_wts = (w_in_x, _row(q_norm_g), _row(kv_norm_g), w_uq_x, w_uk_bd, _row(gm_norm_g), _row(gm_norm_b))
    g_attn_col = jnp.broadcast_to(attn_out_g.astype(F32)[:, None], (MLA_WIDTH, Q_BLK))
    bias_tile = jnp.repeat(gm_bs.T, GM_WIDTH // GM_GROUPS, axis=1).astype(F32)
    w_out_bf, w_xq_bf = w_out.astype(BF16), w_xq.reshape(D_MODEL, X_WIDTH).astype(BF16)
    w_xo_bf = w_xo.reshape(X_WIDTH, D_MODEL).astype(BF16)
    post_tail = (w_xo_bf, _row(ln2_g), _row(ln2_b))
    post_wts = (gm_ws.astype(F32), bias_tile, _row(gm_out_g), w_out_bf, _row(ln1_g), _row(ln1_b), w_xq_bf) + post_tail
    ws0 = jnp.repeat(gm_ws[:, 0, 0], GM_WIDTH // GM_GROUPS)
    bs0 = jnp.repeat(gm_bs[:, 0], GM_WIDTH // GM_GROUPS)
    sample_wts = (w_uv_bd, _row(attn_out_g), _row(ws0), _row(bs0), _row(gm_out_g), w_out_bf, _row(ln1_g),
                  _row(ln1_b), w_xq_bf) + post_tail
    flat = np.full((CAND_ROWS,), 1e8, np.float32)
    for ka in range(PEER_TOPK):
        flat[CAND_OFF[ka]:CAND_OFF[ka] + CAND_NB[ka]] = ka * PEER_TOPK + np.arange(CAND_NB[ka])
    flat = jnp.asarray(np.broadcast_to(flat[:, None], (CAND_ROWS, LANES)))
    peer_wts = (w_pq.reshape(D_MODEL, PEER_HEADS * PEER_DK).T.astype(BF16),
                peer_keys.reshape(2 * PEER_HEADS, N_KEYS, PEER_HALF).astype(BF16), flat,
                peer_u.astype(BF16), peer_v.T.astype(BF16), _row(ln3_g), _row(ln3_b))

    n_p = batch * seq
    xp = x_prompt.reshape(n_p, D_MODEL)
    tm = 512
    qlat, qrope, kcat, ckv_p, krope_p, u_p, v_p, ckvt = _proj(
        xp, _rope_tables(jnp.arange(seq)), proj_wts, tm=tm, seq_blocks=seq // tm, emit_kt=True)
    a_p = _attn_prompt(qlat, qrope, kcat, ckvt, w_uv_t, g_attn_col, batch=batch, seq=seq)
    mk_p, mv_p = _memkv(mem_prompt.reshape(batch * MEM_TOKENS, D_MODEL),
                        w_mk.reshape(D_MODEL, X_WIDTH).astype(BF16), w_mv.reshape(D_MODEL, X_WIDTH).astype(BF16))
    x2_p, x2t_p = _post_prompt(xp, a_p, u_p, v_p, mk_p, mv_p, post_wts, batch=batch, seq=seq)
    y_p = _peer(x2t_p, x2_p, peer_wts, tt=512)

    xs = x_sample.reshape(nb, D_MODEL)
    pos_s = jnp.full((nb,), past_len, jnp.int32)
    qlat_s, qrope_s, _, ckv_s, krope_s, u_s, v_s = _proj(xs, _rope_tables(pos_s), proj_wts, tm=nb, seq_blocks=1,
                                                          emit_kt=False)
    head_pad = ((0, 0), (0, LANES - MLA_HEADS), (0, 0))
    qlat_pad = jnp.pad(qlat_s.reshape(nb, MLA_HEADS, KV_RANK), head_pad)
    qrope_pad = jnp.pad(qrope_s.reshape(nb, MLA_HEADS, LANES)[:, :, :QK_ROPE], head_pad)
    o_lat_s = _attn_decode(page_table, qlat_pad, qrope_pad, cache_ckv, cache_krope, ckv_s, krope_s)
    x2_s = _post_sample(xs, o_lat_s.reshape(nb, MLA_HEADS * KV_RANK), u_s, v_s,
                        cache_mem_k.reshape(nb, MEM_TOKENS, X_WIDTH), cache_mem_v.reshape(nb, MEM_TOKENS, X_WIDTH),
                        sample_wts)
    y_s = _peer(x2_s.T.astype(BF16), x2_s, peer_wts, tt=nb)

    return (y_p.reshape(batch, seq, D_MODEL), y_s.reshape(nb, 1, D_MODEL),
            ckv_p.reshape(batch, seq, KV_RANK), krope_p.reshape(batch, seq, QK_ROPE),
            mk_p.reshape(batch, MEM_TOKENS, X_HEADS, X_HEAD_DIM), mv_p.reshape(batch, MEM_TOKENS, X_HEADS, X_HEAD_DIM),
            ckv_s.reshape(nb, 1, KV_RANK), krope_s.reshape(nb, 1, QK_ROPE), v_s.reshape(nb, 1, GM_WIDTH))
```

```python
import functools

import jax
import jax.numpy as jnp
import numpy as np
from jax import lax
from jax.experimental import pallas as pl
from jax.experimental.pallas import tpu as pltpu

F32 = jnp.float32
BF16 = jnp.bfloat16

D_MODEL = 1024
MLA_HEADS = 8
QK_NOPE = 64
QK_ROPE = 32
V_HEAD = 64
Q_RANK = 256
KV_RANK = 128
MLA_WIDTH = MLA_HEADS * V_HEAD
MLA_SCALE = (QK_NOPE + QK_ROPE) ** -0.5
ROPE_THETA = 10000.0
GM_WIDTH = D_MODEL // 2
GM_GROUPS = 4
GM_CHUNK = 128
MEM_TOKENS = 256
X_HEADS = 4
X_HEAD_DIM = 128
X_WIDTH = X_HEADS * X_HEAD_DIM
X_SCALE = X_HEAD_DIM ** -0.5
PEER_HEADS = 8
N_KEYS = 128
N_EXPERTS = N_KEYS * N_KEYS
PEER_TOPK = 16
PEER_DK = 256
PEER_HALF = PEER_DK // 2
PAGE_SIZE = 128
DEPTH = 1
ALPHA = (2.0 * DEPTH) ** 0.25
EPS = 1e-5

LANES = 128
VMEM_LIMIT = 56 * 1024 * 1024

CAND_NB = tuple(PEER_TOPK // (ka + 1) for ka in range(PEER_TOPK))
CAND_OFF = tuple(int(sum(CAND_NB[:ka])) for ka in range(PEER_TOPK))
CAND_N = int(sum(CAND_NB))
CAND_ROWS = 56
LEVEL2_CHAINS = 4
NEG_INF = float("-inf")


def _dot(a, b):
    return jnp.dot(a, b, preferred_element_type=F32)


def _dot_nt(a, b):
    return lax.dot_general(a, b, (((1,), (1,)), ((), ())), preferred_element_type=F32)


def _dot_tn(a, b):
    return lax.dot_general(a, b, (((0,), (0,)), ((), ())), preferred_element_type=F32)


def _layer_norm(x, g, b):
    mu = jnp.mean(x, -1, keepdims=True)
    var = jnp.mean(jnp.square(x - mu), -1, keepdims=True)
    return (x - mu) * lax.rsqrt(var + EPS) * g + b


def _rms_norm(x, g):
    return x * lax.rsqrt(jnp.mean(jnp.square(x), -1, keepdims=True) + EPS) * g


def _params(*sem):
    return pltpu.CompilerParams(dimension_semantics=sem, vmem_limit_bytes=VMEM_LIMIT)


def _full(shape):
    n = len(shape)
    return pl.BlockSpec(shape, lambda *_: (0,) * n)


def _rope(x, c, s_lo, s_hi):
    width = x.shape[-1]
    return x * c + pltpu.roll(x, width - 16, 1) * s_lo + pltpu.roll(x, 16, 1) * s_hi


def _proj_kernel(x_ref, c_ref, slo_ref, shi_ref, w_in_ref, qg_ref, kvg_ref, w_uq_ref, w_uk_ref,
                 gmg_ref, gmb_ref, qlat_ref, qrope_ref, kcat_ref, ckv_ref, krope_ref, u_ref, v_ref,
                 *maybe_ckvt_ref):
    h = _dot(x_ref[...].astype(BF16), w_in_ref[...])
    c, s_lo, s_hi = c_ref[...], slo_ref[...], shi_ref[...]

    cq = _rms_norm(h[:, :Q_RANK], qg_ref[...])
    q_all = _dot(cq.astype(BF16), w_uq_ref[...])
    q_nope = q_all[:, :MLA_HEADS * QK_NOPE]
    q_lat = _dot(q_nope.astype(BF16), w_uk_ref[...])
    qlat_ref[...] = (q_lat * MLA_SCALE).astype(BF16)
    q_rope = _rope(q_all[:, MLA_HEADS * QK_NOPE:], jnp.tile(c, (1, MLA_HEADS)),
                   jnp.tile(s_lo, (1, MLA_HEADS)), jnp.tile(s_hi, (1, MLA_HEADS)))
    qrope_ref[...] = (q_rope * MLA_SCALE).astype(BF16)

    ckv = _rms_norm(h[:, Q_RANK:Q_RANK + KV_RANK], kvg_ref[...])
    ckv_ref[...] = ckv
    k_rot = _rope(h[:, Q_RANK + KV_RANK:Q_RANK + KV_RANK + LANES], c, s_lo, s_hi)
    krope_ref[...] = k_rot[:, :QK_ROPE]
    kcat_ref[...] = jnp.concatenate([ckv, k_rot], axis=1).astype(BF16)
    for ckvt_ref in maybe_ckvt_ref:
        for j in range(ckvt_ref.shape[0]):
            ckvt_ref[j] = ckv[j * 256:(j + 1) * 256, :].T.astype(BF16)

    uv = jax.nn.gelu(h[:, Q_RANK + KV_RANK + LANES:])
    u_ref[...] = uv[:, :GM_WIDTH]
    v_ref[...] = _layer_norm(uv[:, GM_WIDTH:], gmg_ref[...], gmb_ref[...])


def _proj(x, tables, wts, *, tm, seq_blocks, emit_kt):
    n = x.shape[0]
    row = lambda w: pl.BlockSpec((tm, w), lambda i: (i, 0))
    tab = pl.BlockSpec((tm, LANES), lambda i: (i % seq_blocks, 0))
    w_in, qg, kvg, w_uq, w_uk, gmg, gmb = wts
    out_shape = [
        jax.ShapeDtypeStruct((n, MLA_HEADS * KV_RANK), BF16),
        jax.ShapeDtypeStruct((n, MLA_HEADS * LANES), BF16),
        jax.ShapeDtypeStruct((n, 2 * LANES), BF16),
        jax.ShapeDtypeStruct((n, KV_RANK), F32),
        jax.ShapeDtypeStruct((n, QK_ROPE), F32),
        jax.ShapeDtypeStruct((n, GM_WIDTH), F32),
        jax.ShapeDtypeStruct((n, GM_WIDTH), F32),
    ]
    out_specs = [row(MLA_HEADS * KV_RANK), row(MLA_HEADS * LANES), row(2 * LANES), row(KV_RANK),
                 row(QK_ROPE), row(GM_WIDTH), row(GM_WIDTH)]
    if emit_kt:
        out_shape.append(jax.ShapeDtypeStruct((n // 256, KV_RANK, 256), BF16))
        out_specs.append(pl.BlockSpec((tm // 256, KV_RANK, 256), lambda i: (i, 0, 0)))
    return pl.pallas_call(
        _proj_kernel,
        grid=(n // tm,),
        in_specs=[row(D_MODEL), tab, tab, tab, _full(w_in.shape), _full(qg.shape), _full(kvg.shape),
                  _full(w_uq.shape), _full(w_uk.shape), _full(gmg.shape), _full(gmb.shape)],
        out_specs=out_specs,
        out_shape=out_shape,
        compiler_params=_params("parallel"),
        name="proj",
    )(x, *tables, w_in, qg, kvg, w_uq, w_uk, gmg, gmb)


Q_BLK = 128
KV_BLK = 256


def _attn_kernel(qlat_ref, qrope_ref, kcat_ref, ckvt_ref, wuvt_ref, g_ref, a_ref, m_s, l_s, acc_s):
    qi = pl.program_id(1)
    ql, qr = qlat_ref[...], qrope_ref[...]
    qcat = jnp.concatenate(
        [jnp.concatenate([ql[:, h * LANES:(h + 1) * LANES], qr[:, h * LANES:(h + 1) * LANES]], axis=1)
         for h in range(MLA_HEADS)], axis=0)
    cols = MLA_HEADS * Q_BLK
    m_s[...] = jnp.full((1, cols), NEG_INF, F32)
    l_s[...] = jnp.zeros((1, cols), F32)
    acc_s[...] = jnp.zeros((KV_RANK, cols), F32)
    q_pos = qi * Q_BLK + (lax.broadcasted_iota(jnp.int32, (KV_BLK, cols), 1) & (Q_BLK - 1))
    k_off = lax.broadcasted_iota(jnp.int32, (KV_BLK, cols), 0)

    def step(j, carry):
        k = kcat_ref[pl.ds(pl.multiple_of(j * KV_BLK, KV_BLK), KV_BLK), :]
        st = _dot_nt(k, qcat)
        st = jnp.where(q_pos >= k_off + j * KV_BLK, st, NEG_INF)
        m_old = m_s[...]
        m_new = jnp.maximum(m_old, jnp.max(st, axis=0, keepdims=True))
        alpha = jnp.exp(m_old - m_new)
        p = jnp.exp(st - m_new)
        l_s[...] = alpha * l_s[...] + jnp.sum(p, axis=0, keepdims=True)
        acc_s[...] = alpha * acc_s[...] + _dot(ckvt_ref[j], p.astype(BF16))
        m_s[...] = m_new
        return carry

    lax.fori_loop(0, (qi * Q_BLK) // KV_BLK + 1, step, 0)

    o_t = (acc_s[...] / l_s[...]).astype(BF16)
    om_t = jnp.concatenate(
        [_dot(wuvt_ref[h], o_t[:, h * Q_BLK:(h + 1) * Q_BLK]) for h in range(MLA_HEADS)], axis=0)
    ms = jnp.mean(jnp.square(om_t), axis=0, keepdims=True)
    a_t = om_t * lax.rsqrt(ms + EPS) * g_ref[...]
    a_ref[...] = a_t.T.astype(BF16)


def _attn_prompt(qlat, qrope, kcat, ckvt, wuvt, g_attn, *, batch, seq):
    n = batch * seq
    nq = seq // Q_BLK
    cols = MLA_HEADS * Q_BLK
    return pl.pallas_call(
        _attn_kernel,
        grid=(batch, nq),
        in_specs=[
            pl.BlockSpec((Q_BLK, MLA_HEADS * KV_RANK), lambda b, i: (b * nq + i, 0)),
            pl.BlockSpec((Q_BLK, MLA_HEADS * LANES), lambda b, i: (b * nq + i, 0)),
            pl.BlockSpec((None, seq, 2 * LANES), lambda b, i: (b, 0, 0)),
            pl.BlockSpec((None, seq // KV_BLK, KV_RANK, KV_BLK), lambda b, i: (b, 0, 0, 0)),
            _full(wuvt.shape), _full(g_attn.shape),
        ],
        out_specs=pl.BlockSpec((Q_BLK, MLA_WIDTH), lambda b, i: (b * nq + i, 0)),
        out_shape=jax.ShapeDtypeStruct((n, MLA_WIDTH), BF16),
        scratch_shapes=[pltpu.VMEM((1, cols), F32), pltpu.VMEM((1, cols), F32),
                        pltpu.VMEM((KV_RANK, cols), F32)],
        compiler_params=_params("parallel", "arbitrary"),
        name="attn_prompt",
    )(qlat, qrope, kcat.reshape(batch, seq, 2 * LANES),
      ckvt.reshape(batch, seq // KV_BLK, KV_RANK, KV_BLK), wuvt, g_attn)


PAGES_PER_STEP = 8


def _decode_kernel(pt_ref, qlat_ref, qrope_ref, *refs):
    del pt_ref
    np_ = PAGES_PER_STEP
    ckv_refs, kr_refs = refs[:np_], refs[np_:2 * np_]
    ckvn_ref, krn_ref, o_ref, m_s, l_s, acc_s = refs[2 * np_:]
    c = pl.program_id(1)

    @pl.when(c == 0)
    def _():
        m_s[...] = jnp.full(m_s.shape, NEG_INF, F32)
        l_s[...] = jnp.zeros(l_s.shape, F32)
        acc_s[...] = jnp.zeros(acc_s.shape, F32)

    ql, qr = qlat_ref[...], qrope_ref[...]

    def update(kc, kr, valid_rows):
        st = _dot_nt(kc, ql) + _dot_nt(kr, qr)
        if valid_rows is not None:
            st = jnp.where(lax.broadcasted_iota(jnp.int32, st.shape, 0) < valid_rows, st, NEG_INF)
        m_old = m_s[...]
        m_new = jnp.maximum(m_old, jnp.max(st, axis=0, keepdims=True))
        alpha = jnp.exp(m_old - m_new)
        p = jnp.exp(st - m_new)
        l_s[...] = alpha * l_s[...] + jnp.sum(p, axis=0, keepdims=True)
        acc_s[...] = alpha * acc_s[...] + _dot_tn(kc, p.astype(BF16))
        m_s[...] = m_new

    kc = jnp.concatenate([r[...] for r in ckv_refs], axis=0).astype(BF16)
    kr = jnp.concatenate([r[...] for r in kr_refs], axis=0).astype(BF16)
    update(kc, kr, None)

    @pl.when(c == pl.num_programs(1) - 1)
    def _():
        kn = jnp.broadcast_to(ckvn_ref[...], (16, KV_RANK)).astype(BF16)
        krn = jnp.broadcast_to(krn_ref[...], (16, QK_ROPE)).astype(BF16)
        update(kn, krn, 1)
        o_t = acc_s[...] / l_s[...]
        o_ref[...] = o_t.T[:MLA_HEADS, :]


def _attn_decode(page_table, qlat_pad, qrope_pad, cache_ckv, cache_krope, ckv_new, krope_new):
    nb, n_pages = page_table.shape
    steps = n_pages // PAGES_PER_STEP

    def page_spec(width, i):
        return pl.BlockSpec((None, PAGE_SIZE, width),
                            lambda b, c, pt: (pt[b, c * PAGES_PER_STEP + i], 0, 0))

    in_specs = [pl.BlockSpec((None, LANES, KV_RANK), lambda b, c, pt: (b, 0, 0)),
                pl.BlockSpec((None, LANES, QK_ROPE), lambda b, c, pt: (b, 0, 0))]
    in_specs += [page_spec(KV_RANK, i) for i in range(PAGES_PER_STEP)]
    in_specs += [page_spec(QK_ROPE, i) for i in range(PAGES_PER_STEP)]
    in_specs += [pl.BlockSpec((None, 1, KV_RANK), lambda b, c, pt: (b, 0, 0)),
                 pl.BlockSpec((None, 1, QK_ROPE), lambda b, c, pt: (b, 0, 0))]
    return pl.pallas_call(
        _decode_kernel,
        grid_spec=pltpu.PrefetchScalarGridSpec(
            num_scalar_prefetch=1,
            grid=(nb, steps),
            in_specs=in_specs,
            out_specs=pl.BlockSpec((None, MLA_HEADS, KV_RANK), lambda b, c, pt: (b, 0, 0)),
            scratch_shapes=[pltpu.VMEM((1, LANES), F32), pltpu.VMEM((1, LANES), F32),
                            pltpu.VMEM((KV_RANK, LANES), F32)],
        ),
        out_shape=jax.ShapeDtypeStruct((nb, MLA_HEADS, KV_RANK), F32),
        compiler_params=_params("parallel", "arbitrary"),
        name="attn_decode",
    )(page_table, qlat_pad, qrope_pad, *([cache_ckv] * PAGES_PER_STEP), *([cache_krope] * PAGES_PER_STEP),
      ckv_new.reshape(nb, 1, KV_RANK), krope_new.reshape(nb, 1, QK_ROPE))


def _memkv_kernel(mem_ref, wk_ref, wv_ref, mk_ref, mv_ref):
    m = mem_ref[...].astype(BF16)
    mk_ref[...] = _dot(m, wk_ref[...])
    mv_ref[...] = _dot(m, wv_ref[...])


def _memkv(mem, w_mk, w_mv):
    n = mem.shape[0]
    tm = 512
    return pl.pallas_call(
        _memkv_kernel,
        grid=(n // tm,),
        in_specs=[pl.BlockSpec((tm, D_MODEL), lambda i: (i, 0)), _full(w_mk.shape), _full(w_mv.shape)],
        out_specs=[pl.BlockSpec((tm, X_WIDTH), lambda i: (i, 0))] * 2,
        out_shape=[jax.ShapeDtypeStruct((n, X_WIDTH), F32)] * 2,
        compiler_params=_params("parallel"),
        name="memkv",
    )(mem, w_mk, w_mv)


POST_ROWS = 256


def _softmax_rows(s):
    e = jnp.exp(s - jnp.max(s, axis=-1, keepdims=True))
    return e / jnp.sum(e, axis=-1, keepdims=True)


def _mix_and_ln1(x, a_bf, o_gm, gmog, w_out, ln1g, ln1b):
    gm_n = _rms_norm(o_gm, gmog)
    y = jnp.concatenate([a_bf, gm_n.astype(BF16)], axis=1)
    return _layer_norm(ALPHA * x + _dot(y, w_out), ln1g, ln1b)


def _post_prompt_kernel(x_ref, a_ref, u_ref, v_ref, ws_ref, bias_ref, gmog_ref, wout_ref, ln1g_ref, ln1b_ref,
                        wxq_ref, mk_ref, mv_ref, wxo_ref, ln2g_ref, ln2b_ref, x2_ref, x2t_ref):
    tril = (lax.broadcasted_iota(jnp.int32, (GM_CHUNK, GM_CHUNK), 0)
            >= lax.broadcasted_iota(jnp.int32, (GM_CHUNK, GM_CHUNK), 1))
    w_s = [jnp.where(tril, ws_ref[g], 0.0).astype(BF16) for g in range(GM_GROUPS)]
    chunks = []
    for c in range(POST_ROWS // GM_CHUNK):
        rows = slice(c * GM_CHUNK, (c + 1) * GM_CHUNK)
        v_c = v_ref[rows, :].astype(BF16)
        s = jnp.concatenate([_dot(w_s[g], v_c[:, g * LANES:(g + 1) * LANES]) for g in range(GM_GROUPS)],
                            axis=1) + bias_ref[...]
        chunks.append(u_ref[rows, :] * s)
    o_gm = jnp.concatenate(chunks, axis=0)
    x1 = _mix_and_ln1(x_ref[...], a_ref[...], o_gm, gmog_ref[...], wout_ref[...], ln1g_ref[...], ln1b_ref[...])

    q = _dot(x1.astype(BF16), wxq_ref[...]).astype(BF16)
    mk, mv = mk_ref[...].astype(BF16), mv_ref[...].astype(BF16)
    heads = []
    for h in range(X_HEADS):
        cs = slice(h * X_HEAD_DIM, (h + 1) * X_HEAD_DIM)
        p = _softmax_rows(_dot_nt(q[:, cs], mk[:, cs]) * X_SCALE)
        heads.append(_dot(p.astype(BF16), mv[:, cs]))
    o = jnp.concatenate(heads, axis=1).astype(BF16)
    x2 = _layer_norm(ALPHA * x1 + _dot(o, wxo_ref[...]), ln2g_ref[...], ln2b_ref[...])
    x2_ref[...] = x2
    x2t_ref[...] = x2.T.astype(BF16)


def _post_prompt(x, a, u, v, mk, mv, wts, *, batch, seq):
    n = batch * seq
    nb = seq // POST_ROWS
    row = lambda w: pl.BlockSpec((POST_ROWS, w), lambda b, i: (b * nb + i, 0))
    mem = pl.BlockSpec((MEM_TOKENS, X_WIDTH), lambda b, i: (b, 0))
    return pl.pallas_call(
        _post_prompt_kernel,
        grid=(batch, nb),
        in_specs=[row(D_MODEL), row(MLA_WIDTH), row(GM_WIDTH), row(GM_WIDTH)]
                 + [_full(w.shape) for w in wts[:6]] + [_full(wts[6].shape), mem, mem]
                 + [_full(w.shape) for w in wts[7:]],
        out_specs=[row(D_MODEL), pl.BlockSpec((D_MODEL, POST_ROWS), lambda b, i: (0, b * nb + i))],
        out_shape=[jax.ShapeDtypeStruct((n, D_MODEL), F32), jax.ShapeDtypeStruct((D_MODEL, n), BF16)],
        compiler_params=_params("parallel", "parallel"),
        name="post_prompt",
    )(x, a, u, v, *wts[:7], mk, mv, *wts[7:])


SAMPLE_ROWS = 8


def _post_sample_kernel(x_ref, o_ref, u_ref, v_ref, wuv_ref, ag_ref, ws0_ref, bs0_ref, gmog_ref, wout_ref,
                        ln1g_ref, ln1b_ref, wxq_ref, mk_ref, mv_ref, wxo_ref, ln2g_ref, ln2b_ref, x2_ref):
    o_mla = _dot(o_ref[...].astype(BF16), wuv_ref[...])
    a = _rms_norm(o_mla, ag_ref[...]).astype(BF16)
    o_gm = u_ref[...] * (ws0_ref[...] * v_ref[...] + bs0_ref[...])
    x1 = _mix_and_ln1(x_ref[...], a, o_gm, gmog_ref[...], wout_ref[...], ln1g_ref[...], ln1b_ref[...])

    q = _dot(x1.astype(BF16), wxq_ref[...])
    lane_head = lax.broadcasted_iota(jnp.int32, (LANES, X_WIDTH), 1) // X_HEAD_DIM
    on_head = lane_head == lax.broadcasted_iota(jnp.int32, (LANES, X_WIDTH), 0)
    rows = []
    for j in range(SAMPLE_ROWS):
        q_bd = jnp.where(on_head, q[j:j + 1, :], 0.0).astype(BF16)
        s = _dot_nt(mk_ref[j].astype(BF16), q_bd) * X_SCALE
        e = jnp.exp(s - jnp.max(s, axis=0, keepdims=True))
        p = e / jnp.sum(e, axis=0, keepdims=True)
        o_all = _dot_tn(p.astype(BF16), mv_ref[j].astype(BF16))
        rows.append(jnp.sum(jnp.where(on_head, o_all, 0.0), axis=0, keepdims=True))
    o = jnp.concatenate(rows, axis=0).astype(BF16)
    x2_ref[...] = _layer_norm(ALPHA * x1 + _dot(o, wxo_ref[...]), ln2g_ref[...], ln2b_ref[...])


def _post_sample(x, o_lat, u, v, mk, mv, wts):
    n = x.shape[0]
    row = lambda w: pl.BlockSpec((SAMPLE_ROWS, w), lambda i: (i, 0))
    mem = pl.BlockSpec((SAMPLE_ROWS, MEM_TOKENS, X_WIDTH), lambda i: (i, 0, 0))
    return pl.pallas_call(
        _post_sample_kernel,
        grid=(n // SAMPLE_ROWS,),
        in_specs=[row(D_MODEL), row(MLA_HEADS * KV_RANK), row(GM_WIDTH), row(GM_WIDTH)]
                 + [_full(w.shape) for w in wts[:9]] + [mem, mem] + [_full(w.shape) for w in wts[9:]],
        out_specs=row(D_MODEL),
        out_shape=jax.ShapeDtypeStruct((n, D_MODEL), F32),
        compiler_params=_params("parallel"),
        name="post_sample",
    )(x, o_lat, u, v, *wts[:9], mk, mv, *wts[9:])


def _top16(val, row_id, *, break_ties):
    rank = jnp.full(val.shape, 127.0, F32)
    tops = []
    for k in range(PEER_TOPK):
        m = jnp.max(val, axis=0, keepdims=True)
        hit = val == m
        if break_ties:
            hit = row_id == jnp.min(jnp.where(hit, row_id, 1e9), axis=0, keepdims=True)
        val = jnp.where(hit, NEG_INF, val)
        rank = jnp.where(hit, float(k), rank)
        tops.append(m)
    return tops, rank


def _tied(rank):
    marked = jnp.sum(jnp.where(rank < float(PEER_TOPK), 1.0, 0.0), axis=0, keepdims=True)
    return marked - float(PEER_TOPK)


def _packed(x):
    return pltpu.bitcast(x.astype(BF16), jnp.int32)


def _unpacked(w):
    return pltpu.bitcast(w, BF16)


def _bf16_pair(x):
    u = pltpu.bitcast(x, jnp.int32)
    hi = lax.shift_right_logical(u + 0x7FFF + (lax.shift_right_logical(u, 16) & 1), 16)
    return hi | lax.shift_left(hi, 16)


def _peer_topk_kernel(x2t_ref, wpqt_ref, keys_ref, flat_ref, rb_ref, na_ref, ea_ref, eb_ref,
                      qt_s, s_s, rank_s, top_s, cand_s, sel_s):
    nsub = x2t_ref.shape[1] // LANES
    qt = _dot(wpqt_ref[...], x2t_ref[...]).astype(BF16)
    for sub in range(nsub):
        qt_s[sub] = qt[:, sub * LANES:(sub + 1) * LANES]
    key_id = lax.broadcasted_iota(jnp.int32, (N_KEYS, LANES), 0).astype(F32)
    flat = flat_ref[...]

    def keep_level1(hc, sub, tops, rank):
        rank_s[hc, sub] = rank
        for k in range(PEER_TOPK):
            top_s[hc, sub, k:k + 1, :] = tops[k]

    def level2(h, sub, u):
        sa, sb = top_s[2 * h, sub], top_s[2 * h + 1, sub]
        ea_r = jnp.exp(sa - sa[0:1, :])
        eb_r = jnp.exp(sb - sb[0:1, :])
        for ka in range(PEER_TOPK):
            cand_s[u, CAND_OFF[ka]:CAND_OFF[ka] + CAND_NB[ka], :] = sa[ka:ka + 1, :] + sb[0:CAND_NB[ka], :]
        cand_s[u, CAND_N:CAND_ROWS, :] = jnp.full((CAND_ROWS - CAND_N, LANES), NEG_INF, F32)
        _, crank = _top16(cand_s[u], flat, break_ties=True)
        sel_s[u] = jnp.where(crank < float(PEER_TOPK), 1.0, 0.0)
        n_a, z = [], jnp.zeros((1, LANES), F32)
        for ka in range(PEER_TOPK):
            sel_ka = sel_s[u, CAND_OFF[ka]:CAND_OFF[ka] + CAND_NB[ka], :]
            n_a.append(jnp.sum(sel_ka, axis=0, keepdims=True))
            z = z + ea_r[ka:ka + 1, :] * jnp.sum(sel_ka * eb_r[0:CAND_NB[ka], :], axis=0, keepdims=True)
        rank_a = rank_s[2 * h, sub]
        na = jnp.zeros((N_KEYS, LANES), F32)
        for ka in range(PEER_TOPK):
            na = jnp.where(rank_a == float(ka), n_a[ka], na)
        na_ref[h, sub] = _bf16_pair(na)
        rb_ref[h, sub] = _packed(rank_s[2 * h + 1, sub])
        ea_ref[h, sub] = _bf16_pair(jnp.exp(s_s[2 * h, sub] - sa[0:1, :]))
        eb_ref[h, sub] = _packed(jnp.exp(s_s[2 * h + 1, sub] - sb[0:1, :]) / z)

    def per_subtile(sub, carry):
        def quick(h, tied):
            for hc in (2 * h, 2 * h + 1):
                q_blk = qt_s[sub, pl.ds(pl.multiple_of(hc * PEER_HALF, PEER_HALF), PEER_HALF), :]
                s = _dot(keys_ref[hc], q_blk)
                s_s[hc, sub] = s
                tops, rank = _top16(s, key_id, break_ties=False)
                keep_level1(hc, sub, tops, rank)
                tied = jnp.maximum(tied, _tied(rank))
            return tied

        tied = lax.fori_loop(0, PEER_HEADS, quick, jnp.zeros((1, LANES), F32))

        @pl.when(jnp.max(tied) > 0.0)
        def _():
            def careful(hc, c):
                keep_level1(hc, sub, *_top16(s_s[hc, sub], key_id, break_ties=True))
                return c
            lax.fori_loop(0, 2 * PEER_HEADS, careful, 0)

        def heads(q, c):
            for u in range(LEVEL2_CHAINS):
                level2(q * LEVEL2_CHAINS + u, sub, u)
            return c

        lax.fori_loop(0, PEER_HEADS // LEVEL2_CHAINS, heads, 0)
        return carry

    lax.fori_loop(0, nsub, per_subtile, 0)


KEY_TABLE_ROWS = (N_KEYS // 2, N_KEYS, N_KEYS, N_KEYS // 2)


def _key_spec(nsub, rows, index_map):
    return pl.BlockSpec((PEER_HEADS, nsub, rows, LANES), index_map)


def _peer_topk(x2t, wpqt, keys, flat, *, tt):
    n = x2t.shape[1]
    nsub = tt // LANES
    hc = 2 * PEER_HEADS
    return pl.pallas_call(
        _peer_topk_kernel,
        grid=(n // tt,),
        in_specs=[pl.BlockSpec((D_MODEL, tt), lambda i: (0, i)), _full(wpqt.shape), _full(keys.shape),
                  _full(flat.shape)],
        out_specs=[_key_spec(nsub, rows, lambda i: (0, i, 0, 0)) for rows in KEY_TABLE_ROWS],
        out_shape=[jax.ShapeDtypeStruct((PEER_HEADS, n // LANES, rows, LANES), jnp.int32)
                   for rows in KEY_TABLE_ROWS],
        scratch_shapes=[pltpu.VMEM((nsub, PEER_HEADS * PEER_DK, LANES), BF16),
                        pltpu.VMEM((hc, nsub, N_KEYS, LANES), F32),
                        pltpu.VMEM((hc, nsub, N_KEYS, LANES), F32),
                        pltpu.VMEM((hc, nsub, PEER_TOPK, LANES), F32),
                        pltpu.VMEM((LEVEL2_CHAINS, CAND_ROWS, LANES), F32),
                        pltpu.VMEM((LEVEL2_CHAINS, CAND_ROWS, LANES), F32)],
        compiler_params=_params("parallel"),
        name="peer_topk",
    )(x2t, wpqt, keys, flat)


EXP_BLK = 512
EXP_GROUPS = EXP_BLK // N_KEYS
EXP_STEPS = N_EXPERTS // (2 * EXP_BLK)


def _peer_experts_kernel(x2t_ref, x2_ref, rb_ref, na_ref, ea_ref, eb_ref, u_ref, vt_ref, ln3g_ref, ln3b_ref,
                         y_ref, acc_s, h0_s, h1_s, w0_s, w1_s):
    j = pl.program_id(1)
    nsub = x2t_ref.shape[1] // LANES

    @pl.when(j == 0)
    def _():
        acc_s[...] = jnp.zeros(acc_s.shape, F32)
        h1_s[...] = jnp.zeros(h1_s.shape, F32)
        w0_s[...] = jnp.zeros(w0_s.shape, jnp.int32)

    def key_row(ref, h, sub, ia):
        word = jnp.broadcast_to(ref[h, sub, pl.ds(ia, 1), :], (8, LANES))
        return jnp.tile(pltpu.bitcast(word, BF16), (N_KEYS // 16, 1))

    def gate_piece(h_s, w_s, blk, sub):
        blk = jnp.clip(blk, 0, N_EXPERTS // EXP_BLK - 1)
        lanes = slice(sub * LANES, (sub + 1) * LANES)
        for g in range(EXP_GROUPS):
            ia = blk * EXP_GROUPS + g
            gate = jnp.zeros((N_KEYS, LANES), BF16)
            for h in range(PEER_HEADS):
                eb = _unpacked(eb_ref[h, sub])
                keep = _unpacked(rb_ref[h, sub]) < key_row(na_ref, h, sub, ia)
                gate = gate + jnp.where(keep, eb, jnp.zeros_like(eb)) * key_row(ea_ref, h, sub, ia)
            act = jax.nn.gelu(h_s[g * N_KEYS:(g + 1) * N_KEYS, lanes]).astype(BF16) * gate
            w_s[g * (N_KEYS // 2):(g + 1) * (N_KEYS // 2), lanes] = pltpu.bitcast(act, jnp.int32)

    acc_s[...] += _dot(vt_ref[:, :EXP_BLK], _unpacked(w0_s[...]))
    for sub in range(nsub):
        gate_piece(h1_s, w1_s, 2 * j - 1, sub)
    h0_s[...] = _dot(u_ref[:EXP_BLK, :], x2t_ref[...])
    acc_s[...] += _dot(vt_ref[:, EXP_BLK:], _unpacked(w1_s[...]))
    for sub in range(nsub):
        gate_piece(h0_s, w0_s, 2 * j, sub)
    h1_s[...] = _dot(u_ref[EXP_BLK:, :], x2t_ref[...])

    @pl.when(j == pl.num_programs(1) - 1)
    def _():
        y_ref[...] = _layer_norm(ALPHA * x2_ref[...] + acc_s[...].T, ln3g_ref[...], ln3b_ref[...])


def _peer_experts(x2t, x2, key_arrs, u_bf, vt_bf, ln3g, ln3b, *, tt):
    n = x2.shape[0]
    nsub = tt // LANES
    return pl.pallas_call(
        _peer_experts_kernel,
        grid=(n // tt, EXP_STEPS + 1),
        in_specs=[pl.BlockSpec((D_MODEL, tt), lambda i, j: (0, i)),
                  pl.BlockSpec((tt, D_MODEL), lambda i, j: (i, 0)),
                  *[_key_spec(nsub, rows, lambda i, j: (0, i, 0, 0)) for rows in KEY_TABLE_ROWS],
                  pl.BlockSpec((2 * EXP_BLK, D_MODEL), lambda i, j: (jnp.minimum(j, EXP_STEPS - 1), 0)),
                  pl.BlockSpec((D_MODEL, 2 * EXP_BLK), lambda i, j: (0, jnp.maximum(j - 1, 0))),
                  _full(ln3g.shape), _full(ln3b.shape)],
        out_specs=pl.BlockSpec((tt, D_MODEL), lambda i, j: (i, 0)),
        out_shape=jax.ShapeDtypeStruct((n, D_MODEL), F32),
        scratch_shapes=[pltpu.VMEM((D_MODEL, tt), F32),
                        pltpu.VMEM((EXP_BLK, tt), F32), pltpu.VMEM((EXP_BLK, tt), F32),
                        pltpu.VMEM((EXP_BLK // 2, tt), jnp.int32), pltpu.VMEM((EXP_BLK // 2, tt), jnp.int32)],
        compiler_params=_params("parallel", "arbitrary"),
        name="peer_experts",
    )(x2t, x2, *key_arrs, u_bf, vt_bf, ln3g, ln3b)


def _peer(x2t, x2, peer_wts, *, tt):
    wpqt, keys, flat, u_bf, vt_bf, ln3g, ln3b = peer_wts
    key_arrs = _peer_topk(x2t, wpqt, keys, flat, tt=tt)
    return _peer_experts(x2t, x2, key_arrs, u_bf, vt_bf, ln3g, ln3b, tt=tt)


def _rope_tables(pos):
    inv = ROPE_THETA ** (-jnp.arange(0, QK_ROPE, 2, dtype=F32) / QK_ROPE)
    ang = pos.astype(F32)[:, None] * inv[None, :]
    cos, sin, zero = jnp.cos(ang), jnp.sin(ang), jnp.zeros_like(ang)
    pad = jnp.zeros((pos.shape[0], LANES - QK_ROPE), F32)
    return (jnp.concatenate([cos, cos, pad], axis=1),
            jnp.concatenate([-sin, zero, pad], axis=1),
            jnp.concatenate([zero, sin, pad], axis=1))


def _row(v):
    return v.reshape(1, -1).astype(F32)


def kernel(x_prompt, x_sample, mem_prompt, cache_ckv, cache_krope, cache_mem_k, cache_mem_v, page_table,
           w_in, q_norm_g, kv_norm_g, w_uq, w_uk, w_uv, gm_norm_g, gm_norm_b, gm_ws, gm_bs, attn_out_g,
           gm_out_g, w_out, ln1_g, ln1_b, w_xq, w_mk, w_mv, w_xo, ln2_g, ln2_b, w_pq, peer_keys, peer_u,
           peer_v, ln3_g, ln3_b):
    batch, seq = x_prompt.shape[:2]
    nb = x_sample.shape[0]
    past_len = page_table.shape[1] * PAGE_SIZE

    kr_pad = jnp.zeros((D_MODEL, LANES - QK_ROPE), F32)
    w_in_x = jnp.concatenate([w_in[:, :Q_RANK + KV_RANK + QK_ROPE], kr_pad,
                              w_in[:, Q_RANK + KV_RANK + QK_ROPE:]], axis=1).astype(BF16)
    uq_nope = w_uq[:, :, :QK_NOPE].reshape(Q_RANK, MLA_HEADS * QK_NOPE)
    uq_rope = jnp.pad(w_uq[:, :, QK_NOPE:], ((0, 0), (0, 0), (0, LANES - QK_ROPE)))
    w_uq_x = jnp.concatenate([uq_nope, uq_rope.reshape(Q_RANK, MLA_HEADS * LANES)], axis=1).astype(BF16)
    eye = jnp.eye(MLA_HEADS, dtype=F32)
    w_uk_bd = jnp.einsum('rhd,hg->hdgr', w_uk, eye).reshape(MLA_HEADS * QK_NOPE, MLA_HEADS * KV_RANK).astype(BF16)
    w_uv_bd = jnp.einsum('rhd,hg->hrgd', w_uv, eye).reshape(MLA_HEADS * KV_RANK, MLA_WIDTH).astype(BF16)
    w_uv_t = jnp.transpose(w_uv, (1, 2, 0)).astype(BF16)
    proj_wts = (w_in_x, _row(q_norm_g), _row(kv_norm_g), w_uq_x, w_uk_bd, _row(gm_norm_g), _row(gm_norm_b))
    g_attn_col = jnp.broadcast_to(attn_out_g.astype(F32)[:, None], (MLA_WIDTH, Q_BLK))
    bias_tile = jnp.repeat(gm_bs.T, GM_WIDTH // GM_GROUPS, axis=1).astype(F32)
    w_out_bf, w_xq_bf = w_out.astype(BF16), w_xq.reshape(D_MODEL, X_WIDTH).astype(BF16)
    w_xo_bf = w_xo.reshape(X_WIDTH, D_MODEL).astype(BF16)
    post_tail = (w_xo_bf, _row(ln2_g), _row(ln2_b))
    post_wts = (gm_ws.astype(F32), bias_tile, _row(gm_out_g), w_out_bf, _row(ln1_g), _row(ln1_b), w_xq_bf) + post_tail
    ws0 = jnp.repeat(gm_ws[:, 0, 0], GM_WIDTH // GM_GROUPS)
    bs0 = jnp.repeat(gm_bs[:, 0], GM_WIDTH // GM_GROUPS)
    sample_wts = (w_uv_bd, _row(attn_out_g), _row(ws0), _row(bs0), _row(gm_out_g), w_out_bf, _row(ln1_g),
                  _row(ln1_b), w_xq_bf) + post_tail
    flat = np.full((CAND_ROWS,), 1e8, np.float32)
    for ka in range(PEER_TOPK):
        flat[CAND_OFF[ka]:CAND_OFF[ka] + CAND_NB[ka]] = ka * PEER_TOPK + np.arange(CAND_NB[ka])
    flat = jnp.asarray(np.broadcast_to(flat[:, None], (CAND_ROWS, LANES)))
    peer_wts = (w_pq.reshape(D_MODEL, PEER_HEADS * PEER_DK).T.astype(BF16),
                peer_keys.reshape(2 * PEER_HEADS, N_KEYS, PEER_HALF).astype(BF16), flat,
                peer_u.astype(BF16), peer_v.T.astype(BF16), _row(ln3_g), _row(ln3_b))

    n_p = batch * seq
    xp = x_prompt.reshape(n_p, D_MODEL)
    tm = 512
    qlat, qrope, kcat, ckv_p, krope_p, u_p, v_p, ckvt = _proj(
        xp, _rope_tables(jnp.arange(seq)), proj_wts, tm=tm, seq_blocks=seq // tm, emit_kt=True)
    a_p = _attn_prompt(qlat, qrope, kcat, ckvt, w_uv_t, g_attn_col, batch=batch, seq=seq)
    mk_p, mv_p = _memkv(mem_prompt.reshape(batch * MEM_TOKENS, D_MODEL),
                        w_mk.reshape(D_MODEL, X_WIDTH).astype(BF16), w_mv.reshape(D_MODEL, X_WIDTH).astype(BF16))
    x2_p, x2t_p = _post_prompt(xp, a_p, u_p, v_p, mk_p, mv_p, post_wts, batch=batch, seq=seq)
    y_p = _peer(x2t_p, x2_p, peer_wts, tt=512)

    xs = x_sample.reshape(nb, D_MODEL)
    pos_s = jnp.full((nb,), past_len, jnp.int32)
    qlat_s, qrope_s, _, ckv_s, krope_s, u_s, v_s = _proj(xs, _rope_tables(pos_s), proj_wts, tm=nb, seq_blocks=1,
                                                          emit_kt=False)
    head_pad = ((0, 0), (0, LANES - MLA_HEADS), (0, 0))
    qlat_pad = jnp.pad(qlat_s.reshape(nb, MLA_HEADS, KV_RANK), head_pad)
    qrope_pad = jnp.pad(qrope_s.reshape(nb, MLA_HEADS, LANES)[:, :, :QK_ROPE], head_pad)
    o_lat_s = _attn_decode(page_table, qlat_pad, qrope_pad, cache_ckv, cache_krope, ckv_s, krope_s)
    x2_s = _post_sample(xs, o_lat_s.reshape(nb, MLA_HEADS * KV_RANK), u_s, v_s,
                        cache_mem_k.reshape(nb, MEM_TOKENS, X_WIDTH), cache_mem_v.reshape(nb, MEM_TOKENS, X_WIDTH),
                        sample_wts)
    y_s = _peer(x2_s.T.astype(BF16), x2_s, peer_wts, tt=nb)

    return (y_p.reshape(batch, seq, D_MODEL), y_s.reshape(nb, 1, D_MODEL),
            ckv_p.reshape(batch, seq, KV_RANK), krope_p.reshape(batch, seq, QK_ROPE),
            mk_p.reshape(batch, MEM_TOKENS, X_HEADS, X_HEAD_DIM), mv_p.reshape(batch, MEM_TOKENS, X_HEADS, X_HEAD_DIM),
            ckv_s.reshape(nb, 1, KV_RANK), krope_s.reshape(nb, 1, QK_ROPE), v_s.reshape(nb, 1, GM_WIDTH))
```

```python
import functools

import jax
import jax.numpy as jnp
import numpy as np
from jax import lax
from jax.experimental import pallas as pl
from jax.experimental.pallas import tpu as pltpu

F32 = jnp.float32
BF16 = jnp.bfloat16

D_MODEL = 1024
MLA_HEADS = 8
QK_NOPE = 64
QK_ROPE = 32
V_HEAD = 64
Q_RANK = 256
KV_RANK = 128
MLA_WIDTH = MLA_HEADS * V_HEAD
MLA_SCALE = (QK_NOPE + QK_ROPE) ** -0.5
Q_SCALE = float(MLA_SCALE * np.log2(np.e))
ROPE_THETA = 10000.0
GM_WIDTH = D_MODEL // 2
GM_GROUPS = 4
GM_CHUNK = 128
MEM_TOKENS = 256
X_HEADS = 4
X_HEAD_DIM = 128
X_WIDTH = X_HEADS * X_HEAD_DIM
X_SCALE = X_HEAD_DIM ** -0.5
PEER_HEADS = 8
N_KEYS = 128
N_EXPERTS = N_KEYS * N_KEYS
PEER_TOPK = 16
PEER_DK = 256
PEER_HALF = PEER_DK // 2
PAGE_SIZE = 128
DEPTH = 1
ALPHA = (2.0 * DEPTH) ** 0.25
EPS = 1e-5

LANES = 128
VMEM_LIMIT = 56 * 1024 * 1024

CAND_NB = tuple(PEER_TOPK // (ka + 1) for ka in range(PEER_TOPK))
CAND_OFF = tuple(int(sum(CAND_NB[:ka])) for ka in range(PEER_TOPK))
CAND_N = int(sum(CAND_NB))
CAND_ROWS = 56
LEVEL2_CHAINS = 4
NEG_INF = float("-inf")


def _dot(a, b):
    return jnp.dot(a, b, preferred_element_type=F32)


def _dot_nt(a, b):
    return lax.dot_general(a, b, (((1,), (1,)), ((), ())), preferred_element_type=F32)


def _dot_tn(a, b):
    return lax.dot_general(a, b, (((0,), (0,)), ((), ())), preferred_element_type=F32)


def _layer_norm(x, g, b):
    mu = jnp.mean(x, -1, keepdims=True)
    var = jnp.mean(jnp.square(x - mu), -1, keepdims=True)
    return (x - mu) * lax.rsqrt(var + EPS) * g + b


def _rms_norm(x, g):
    return x * lax.rsqrt(jnp.mean(jnp.square(x), -1, keepdims=True) + EPS) * g


_GELU_K1 = float(-2.0 * np.sqrt(2.0 / np.pi) * np.log2(np.e))
_GELU_K2 = float(0.044715 * _GELU_K1)


def _gelu_tanh(x):
    return x / (1.0 + jnp.exp2(x * (_GELU_K1 + _GELU_K2 * (x * x))))


def _params(*sem):
    return pltpu.CompilerParams(dimension_semantics=sem, vmem_limit_bytes=VMEM_LIMIT)


def _full(shape):
    n = len(shape)
    return pl.BlockSpec(shape, lambda *_: (0,) * n)


def _rope(x, c, s_lo, s_hi):
    width = x.shape[-1]
    return x * c + pltpu.roll(x, width - 16, 1) * s_lo + pltpu.roll(x, 16, 1) * s_hi


def _proj_kernel(x_ref, c_ref, slo_ref, shi_ref, w_in_ref, qg_ref, kvg_ref, w_uq_ref, w_uk_ref,
                 gmg_ref, gmb_ref, qlat_ref, qrope_ref, kcat_ref, ckv_ref, krope_ref, u_ref, v_ref,
                 *maybe_ckvt_ref):
    h = _dot(x_ref[...].astype(BF16), w_in_ref[...])
    c, s_lo, s_hi = c_ref[...], slo_ref[...], shi_ref[...]

    cq = _rms_norm(h[:, :Q_RANK], qg_ref[...])
    q_all = _dot(cq.astype(BF16), w_uq_ref[...])
    q_nope = q_all[:, :MLA_HEADS * QK_NOPE]
    q_lat = _dot(q_nope.astype(BF16), w_uk_ref[...])
    qlat_ref[...] = (q_lat * Q_SCALE).astype(BF16)
    q_rope = _rope(q_all[:, MLA_HEADS * QK_NOPE:], jnp.tile(c, (1, MLA_HEADS)),
                   jnp.tile(s_lo, (1, MLA_HEADS)), jnp.tile(s_hi, (1, MLA_HEADS)))
    qrope_ref[...] = (q_rope * Q_SCALE).astype(BF16)

    ckv = _rms_norm(h[:, Q_RANK:Q_RANK + KV_RANK], kvg_ref[...])
    ckv_ref[...] = ckv
    k_rot = _rope(h[:, Q_RANK + KV_RANK:Q_RANK + KV_RANK + LANES], c, s_lo, s_hi)
    krope_ref[...] = k_rot[:, :QK_ROPE]
    kcat_ref[...] = jnp.concatenate([ckv, k_rot], axis=1).astype(BF16)
    for ckvt_ref in maybe_ckvt_ref:
        for j in range(ckvt_ref.shape[0]):
            ckvt_ref[j] = ckv[j * 256:(j + 1) * 256, :].T.astype(BF16)

    uv = jax.nn.gelu(h[:, Q_RANK + KV_RANK + LANES:])
    u_ref[...] = uv[:, :GM_WIDTH]
    v_ref[...] = _layer_norm(uv[:, GM_WIDTH:], gmg_ref[...], gmb_ref[...])


def _proj(x, tables, wts, *, tm, seq_blocks, emit_kt):
    n = x.shape[0]
    row = lambda w: pl.BlockSpec((tm, w), lambda i: (i, 0))
    tab = pl.BlockSpec((tm, LANES), lambda i: (i % seq_blocks, 0))
    w_in, qg, kvg, w_uq, w_uk, gmg, gmb = wts
    out_shape = [
        jax.ShapeDtypeStruct((n, MLA_HEADS * KV_RANK), BF16),
        jax.ShapeDtypeStruct((n, MLA_HEADS * LANES), BF16),
        jax.ShapeDtypeStruct((n, 2 * LANES), BF16),
        jax.ShapeDtypeStruct((n, KV_RANK), F32),
        jax.ShapeDtypeStruct((n, QK_ROPE), F32),
        jax.ShapeDtypeStruct((n, GM_WIDTH), F32),
        jax.ShapeDtypeStruct((n, GM_WIDTH), F32),
    ]
    out_specs = [row(MLA_HEADS * KV_RANK), row(MLA_HEADS * LANES), row(2 * LANES), row(KV_RANK),
                 row(QK_ROPE), row(GM_WIDTH), row(GM_WIDTH)]
    if emit_kt:
        out_shape.append(jax.ShapeDtypeStruct((n // 256, KV_RANK, 256), BF16))
        out_specs.append(pl.BlockSpec((tm // 256, KV_RANK, 256), lambda i: (i, 0, 0)))
    return pl.pallas_call(
        _proj_kernel,
        grid=(n // tm,),
        in_specs=[row(D_MODEL), tab, tab, tab, _full(w_in.shape), _full(qg.shape), _full(kvg.shape),
                  _full(w_uq.shape), _full(w_uk.shape), _full(gmg.shape), _full(gmb.shape)],
        out_specs=out_specs,
        out_shape=out_shape,
        compiler_params=_params("parallel"),
        name="proj",
    )(x, *tables, w_in, qg, kvg, w_uq, w_uk, gmg, gmb)


Q_BLK = 256
KV_BLK = 256


ATTN_COLS = MLA_HEADS * Q_BLK


def _attn_kernel(qlat_ref, qrope_ref, kcat_ref, ckvt_ref, wuvt_ref, g_ref, a_ref, q_s, m_s, l_s, acc_s):
    qi = pl.program_id(1)
    ql, qr = qlat_ref[...], qrope_ref[...]
    hpc = ATTN_COLS // Q_BLK
    for h in range(MLA_HEADS):
        q_s[h // hpc, (h % hpc) * Q_BLK:(h % hpc + 1) * Q_BLK, :] = jnp.concatenate(
            [ql[:, h * LANES:(h + 1) * LANES], qr[:, h * LANES:(h + 1) * LANES]], axis=1)
    m_s[...] = jnp.full(m_s.shape, NEG_INF, F32)
    l_s[...] = jnp.zeros(l_s.shape, F32)
    acc_s[...] = jnp.zeros(acc_s.shape, F32)
    q_pos = qi * Q_BLK + (lax.broadcasted_iota(jnp.int32, (KV_BLK, ATTN_COLS), 1) & (Q_BLK - 1))
    k_off = lax.broadcasted_iota(jnp.int32, (KV_BLK, ATTN_COLS), 0)

    def step(j, diagonal):
        k = kcat_ref[pl.ds(pl.multiple_of(j * KV_BLK, KV_BLK), KV_BLK), :]
        v_t = ckvt_ref[j]
        for c in range(MLA_HEADS // hpc):
            st = _dot_nt(k, q_s[c])
            if diagonal:
                st = jnp.where(q_pos >= k_off + j * KV_BLK, st, NEG_INF)
            m_old = m_s[c]
            m_new = jnp.maximum(m_old, jnp.max(st, axis=0, keepdims=True))
            alpha = jnp.exp2(m_old - m_new)
            p = jnp.exp2(st - m_new)
            l_s[c] = alpha * l_s[c] + jnp.sum(p, axis=0, keepdims=True)
            acc_s[c] = alpha * acc_s[c] + _dot(v_t, p.astype(BF16))
            m_s[c] = m_new

    last = (qi * Q_BLK) // KV_BLK

    def visible(j, carry):
        step(j, False)
        return carry

    lax.fori_loop(0, last, visible, 0)
    step(last, True)

    o_t = (acc_s[...] / l_s[...]).astype(BF16)
    om_t = jnp.concatenate(
        [_dot(wuvt_ref[h], o_t[h // hpc, :, (h % hpc) * Q_BLK:(h % hpc + 1) * Q_BLK])
         for h in range(MLA_HEADS)], axis=0)
    ms = jnp.mean(jnp.square(om_t), axis=0, keepdims=True)
    a_t = om_t * lax.rsqrt(ms + EPS) * g_ref[...]
    a_ref[...] = a_t.T.astype(BF16)


def _attn_prompt(qlat, qrope, kcat, ckvt, wuvt, g_attn, *, batch, seq):
    n = batch * seq
    nq = seq // Q_BLK
    chains = MLA_HEADS * Q_BLK // ATTN_COLS
    return pl.pallas_call(
        _attn_kernel,
        grid=(batch, nq),
        in_specs=[
            pl.BlockSpec((Q_BLK, MLA_HEADS * KV_RANK), lambda b, i: (b * nq + i, 0)),
            pl.BlockSpec((Q_BLK, MLA_HEADS * LANES), lambda b, i: (b * nq + i, 0)),
            pl.BlockSpec((None, seq, 2 * LANES), lambda b, i: (b, 0, 0)),
            pl.BlockSpec((None, seq // KV_BLK, KV_RANK, KV_BLK), lambda b, i: (b, 0, 0, 0)),
            _full(wuvt.shape), _full(g_attn.shape),
        ],
        out_specs=pl.BlockSpec((Q_BLK, MLA_WIDTH), lambda b, i: (b * nq + i, 0)),
        out_shape=jax.ShapeDtypeStruct((n, MLA_WIDTH), BF16),
        scratch_shapes=[pltpu.VMEM((chains, ATTN_COLS, 2 * LANES), BF16), pltpu.VMEM((chains, 1, ATTN_COLS), F32),
                        pltpu.VMEM((chains, 1, ATTN_COLS), F32), pltpu.VMEM((chains, KV_RANK, ATTN_COLS), F32)],
        compiler_params=_params("parallel", "arbitrary"),
        name="attn_prompt",
    )(qlat, qrope, kcat.reshape(batch, seq, 2 * LANES),
      ckvt.reshape(batch, seq // KV_BLK, KV_RANK, KV_BLK), wuvt, g_attn)


PAGES_PER_STEP = 8


SEQS_PER_STEP = 2


def _decode_kernel(pt_ref, q_ref, *refs):
    del pt_ref
    n_pg = SEQS_PER_STEP * PAGES_PER_STEP
    ckv_refs, kr_refs = refs[:n_pg], refs[n_pg:2 * n_pg]
    ckvn_ref, krn_ref, o_ref, m_s, l_s, acc_s = refs[2 * n_pg:]
    c = pl.program_id(1)

    @pl.when(c == 0)
    def _():
        m_s[...] = jnp.full(m_s.shape, NEG_INF, F32)
        l_s[...] = jnp.zeros(l_s.shape, F32)
        acc_s[...] = jnp.zeros(acc_s.shape, F32)

    def keys(ckv, kr):
        pad = jnp.zeros((kr.shape[0], LANES - QK_ROPE), F32)
        return jnp.concatenate([ckv, kr, pad], axis=1).astype(BF16)

    def update(b, kcat, valid_rows):
        st = _dot_nt(kcat, q_ref[b])
        if valid_rows is not None:
            st = jnp.where(lax.broadcasted_iota(jnp.int32, st.shape, 0) < valid_rows, st, NEG_INF)
        m_old = m_s[b]
        m_new = jnp.maximum(m_old, jnp.max(st, axis=0, keepdims=True))
        alpha = jnp.exp2(m_old - m_new)
        p = jnp.exp2(st - m_new)
        l_s[b] = alpha * l_s[b] + jnp.sum(p, axis=0, keepdims=True)
        acc_s[b] = alpha * acc_s[b] + _dot_tn(kcat[:, :KV_RANK], p.astype(BF16))
        m_s[b] = m_new

    for b in range(SEQS_PER_STEP):
        pages = slice(b * PAGES_PER_STEP, (b + 1) * PAGES_PER_STEP)
        update(b, keys(jnp.concatenate([r[...] for r in ckv_refs[pages]], axis=0),
                       jnp.concatenate([r[...] for r in kr_refs[pages]], axis=0)), None)

    @pl.when(c == pl.num_programs(1) - 1)
    def _():
        for b in range(SEQS_PER_STEP):
            update(b, keys(jnp.broadcast_to(ckvn_ref[b], (16, KV_RANK)),
                           jnp.broadcast_to(krn_ref[b], (16, QK_ROPE))), 1)
            o_t = acc_s[b] / l_s[b]
            o_ref[b] = o_t.T[:MLA_HEADS, :]


def _attn_decode(page_table, q_pad, cache_ckv, cache_krope, ckv_new, krope_new):
    nb, n_pages = page_table.shape
    steps = n_pages // PAGES_PER_STEP
    sq = SEQS_PER_STEP

    def page_spec(width, b, i):
        return pl.BlockSpec((None, PAGE_SIZE, width),
                            lambda g, c, pt: (pt[g * sq + b, c * PAGES_PER_STEP + i], 0, 0))

    def seq_spec(*dims):
        return pl.BlockSpec((sq,) + dims, lambda g, c, pt: (g,) + (0,) * len(dims))

    in_specs = [seq_spec(LANES, 2 * LANES)]
    in_specs += [page_spec(KV_RANK, b, i) for b in range(sq) for i in range(PAGES_PER_STEP)]
    in_specs += [page_spec(QK_ROPE, b, i) for b in range(sq) for i in range(PAGES_PER_STEP)]
    in_specs += [seq_spec(1, KV_RANK), seq_spec(1, QK_ROPE)]
    n_pg = sq * PAGES_PER_STEP
    return pl.pallas_call(
        _decode_kernel,
        grid_spec=pltpu.PrefetchScalarGridSpec(
            num_scalar_prefetch=1,
            grid=(nb // sq, steps),
            in_specs=in_specs,
            out_specs=seq_spec(MLA_HEADS, KV_RANK),
            scratch_shapes=[pltpu.VMEM((sq, 1, LANES), F32), pltpu.VMEM((sq, 1, LANES), F32),
                            pltpu.VMEM((sq, KV_RANK, LANES), F32)],
        ),
        out_shape=jax.ShapeDtypeStruct((nb, MLA_HEADS, KV_RANK), F32),
        compiler_params=_params("parallel", "arbitrary"),
        name="attn_decode",
    )(page_table, q_pad, *([cache_ckv] * n_pg), *([cache_krope] * n_pg),
      ckv_new.reshape(nb, 1, KV_RANK), krope_new.reshape(nb, 1, QK_ROPE))


def _memkv_kernel(mem_ref, wk_ref, wv_ref, mk_ref, mv_ref):
    m = mem_ref[...].astype(BF16)
    mk_ref[...] = _dot(m, wk_ref[...])
    mv_ref[...] = _dot(m, wv_ref[...])


def _memkv(mem, w_mk, w_mv):
    n = mem.shape[0]
    tm = 512
    return pl.pallas_call(
        _memkv_kernel,
        grid=(n // tm,),
        in_specs=[pl.BlockSpec((tm, D_MODEL), lambda i: (i, 0)), _full(w_mk.shape), _full(w_mv.shape)],
        out_specs=[pl.BlockSpec((tm, X_WIDTH), lambda i: (i, 0))] * 2,
        out_shape=[jax.ShapeDtypeStruct((n, X_WIDTH), F32)] * 2,
        compiler_params=_params("parallel"),
        name="memkv",
    )(mem, w_mk, w_mv)


POST_ROWS = 256


def _softmax_rows(s):
    e = jnp.exp(s - jnp.max(s, axis=-1, keepdims=True))
    return e / jnp.sum(e, axis=-1, keepdims=True)


def _mix_and_ln1(x, a_bf, o_gm, gmog, w_out, ln1g, ln1b):
    gm_n = _rms_norm(o_gm, gmog)
    y = jnp.concatenate([a_bf, gm_n.astype(BF16)], axis=1)
    return _layer_norm(ALPHA * x + _dot(y, w_out), ln1g, ln1b)


def _post_prompt_kernel(x_ref, a_ref, u_ref, v_ref, ws_ref, bias_ref, gmog_ref, wout_ref, ln1g_ref, ln1b_ref,
                        wxq_ref, mk_ref, mv_ref, wxo_ref, ln2g_ref, ln2b_ref, x2_ref, x2t_ref):
    tril = (lax.broadcasted_iota(jnp.int32, (GM_CHUNK, GM_CHUNK), 0)
            >= lax.broadcasted_iota(jnp.int32, (GM_CHUNK, GM_CHUNK), 1))
    w_s = [jnp.where(tril, ws_ref[g], 0.0).astype(BF16) for g in range(GM_GROUPS)]
    chunks = []
    for c in range(POST_ROWS // GM_CHUNK):
        rows = slice(c * GM_CHUNK, (c + 1) * GM_CHUNK)
        v_c = v_ref[rows, :].astype(BF16)
        s = jnp.concatenate([_dot(w_s[g], v_c[:, g * LANES:(g + 1) * LANES]) for g in range(GM_GROUPS)],
                            axis=1) + bias_ref[...]
        chunks.append(u_ref[rows, :] * s)
    o_gm = jnp.concatenate(chunks, axis=0)
    x1 = _mix_and_ln1(x_ref[...], a_ref[...], o_gm, gmog_ref[...], wout_ref[...], ln1g_ref[...], ln1b_ref[...])

    q = _dot(x1.astype(BF16), wxq_ref[...]).astype(BF16)
    mk, mv = mk_ref[...].astype(BF16), mv_ref[...].astype(BF16)
    heads = []
    for h in range(X_HEADS):
        cs = slice(h * X_HEAD_DIM, (h + 1) * X_HEAD_DIM)
        p = _softmax_rows(_dot_nt(q[:, cs], mk[:, cs]) * X_SCALE)
        heads.append(_dot(p.astype(BF16), mv[:, cs]))
    o = jnp.concatenate(heads, axis=1).astype(BF16)
    x2 = _layer_norm(ALPHA * x1 + _dot(o, wxo_ref[...]), ln2g_ref[...], ln2b_ref[...])
    x2_ref[...] = x2
    x2t_ref[...] = x2.T.astype(BF16)


def _post_prompt(x, a, u, v, mk, mv, wts, *, batch, seq):
    n = batch * seq
    nb = seq // POST_ROWS
    row = lambda w: pl.BlockSpec((POST_ROWS, w), lambda b, i: (b * nb + i, 0))
    mem = pl.BlockSpec((MEM_TOKENS, X_WIDTH), lambda b, i: (b, 0))
    return pl.pallas_call(
        _post_prompt_kernel,
        grid=(batch, nb),
        in_specs=[row(D_MODEL), row(MLA_WIDTH), row(GM_WIDTH), row(GM_WIDTH)]
                 + [_full(w.shape) for w in wts[:6]] + [_full(wts[6].shape), mem, mem]
                 + [_full(w.shape) for w in wts[7:]],
        out_specs=[row(D_MODEL), pl.BlockSpec((D_MODEL, POST_ROWS), lambda b, i: (0, b * nb + i))],
        out_shape=[jax.ShapeDtypeStruct((n, D_MODEL), F32), jax.ShapeDtypeStruct((D_MODEL, n), BF16)],
        compiler_params=_params("parallel", "parallel"),
        name="post_prompt",
    )(x, a, u, v, *wts[:7], mk, mv, *wts[7:])


SAMPLE_ROWS = 8


def _post_sample_kernel(x_ref, o_ref, u_ref, v_ref, wuv_ref, ag_ref, ws0_ref, bs0_ref, gmog_ref, wout_ref,
                        ln1g_ref, ln1b_ref, wxq_ref, mk_ref, mv_ref, wxo_ref, ln2g_ref, ln2b_ref, x2_ref):
    o_mla = _dot(o_ref[...].astype(BF16), wuv_ref[...])
    a = _rms_norm(o_mla, ag_ref[...]).astype(BF16)
    o_gm = u_ref[...] * (ws0_ref[...] * v_ref[...] + bs0_ref[...])
    x1 = _mix_and_ln1(x_ref[...], a, o_gm, gmog_ref[...], wout_ref[...], ln1g_ref[...], ln1b_ref[...])

    q = _dot(x1.astype(BF16), wxq_ref[...])
    lane_head = lax.broadcasted_iota(jnp.int32, (LANES, X_WIDTH), 1) // X_HEAD_DIM
    on_head = lane_head == lax.broadcasted_iota(jnp.int32, (LANES, X_WIDTH), 0)
    rows = []
    for j in range(SAMPLE_ROWS):
        q_bd = jnp.where(on_head, q[j:j + 1, :], 0.0).astype(BF16)
        s = _dot_nt(mk_ref[j].astype(BF16), q_bd) * X_SCALE
        e = jnp.exp(s - jnp.max(s, axis=0, keepdims=True))
        p = e / jnp.sum(e, axis=0, keepdims=True)
        o_all = _dot_tn(p.astype(BF16), mv_ref[j].astype(BF16))
        rows.append(jnp.sum(jnp.where(on_head, o_all, 0.0), axis=0, keepdims=True))
    o = jnp.concatenate(rows, axis=0).astype(BF16)
    x2_ref[...] = _layer_norm(ALPHA * x1 + _dot(o, wxo_ref[...]), ln2g_ref[...], ln2b_ref[...])


def _post_sample(x, o_lat, u, v, mk, mv, wts):
    n = x.shape[0]
    row = lambda w: pl.BlockSpec((SAMPLE_ROWS, w), lambda i: (i, 0))
    mem = pl.BlockSpec((SAMPLE_ROWS, MEM_TOKENS, X_WIDTH), lambda i: (i, 0, 0))
    return pl.pallas_call(
        _post_sample_kernel,
        grid=(n // SAMPLE_ROWS,),
        in_specs=[row(D_MODEL), row(MLA_HEADS * KV_RANK), row(GM_WIDTH), row(GM_WIDTH)]
                 + [_full(w.shape) for w in wts[:9]] + [mem, mem] + [_full(w.shape) for w in wts[9:]],
        out_specs=row(D_MODEL),
        out_shape=jax.ShapeDtypeStruct((n, D_MODEL), F32),
        compiler_params=_params("parallel"),
        name="post_sample",
    )(x, o_lat, u, v, *wts[:9], mk, mv, *wts[9:])


def _top16(val, row_id, *, break_ties):
    rank = jnp.full(val.shape, 127.0, F32)
    tops = []
    for k in range(PEER_TOPK):
        m = jnp.max(val, axis=0, keepdims=True)
        hit = val == m
        if break_ties:
            hit = row_id == jnp.min(jnp.where(hit, row_id, 1e9), axis=0, keepdims=True)
        val = jnp.where(hit, NEG_INF, val)
        rank = jnp.where(hit, float(k), rank)
        tops.append(m)
    return tops, rank


def _tied(rank):
    marked = jnp.sum(jnp.where(rank < float(PEER_TOPK), 1.0, 0.0), axis=0, keepdims=True)
    return marked - float(PEER_TOPK)


def _packed(x):
    return pltpu.bitcast(x.astype(BF16), jnp.int32)


def _unpacked(w):
    return pltpu.bitcast(w, BF16)


def _bf16_pair(x):
    u = pltpu.bitcast(x, jnp.int32)
    hi = lax.shift_right_logical(u + 0x7FFF + (lax.shift_right_logical(u, 16) & 1), 16)
    return hi | lax.shift_left(hi, 16)


def _peer_topk_kernel(x2t_ref, wpqt_ref, keys_ref, flat_ref, rb_ref, na_ref, ea_ref, eb_ref,
                      qt_s, s_s, rank_s, top_s, cand_s, sel_s):
    nsub = x2t_ref.shape[1] // LANES
    qt = _dot(wpqt_ref[...], x2t_ref[...]).astype(BF16)
    for sub in range(nsub):
        qt_s[sub] = qt[:, sub * LANES:(sub + 1) * LANES]
    key_id = lax.broadcasted_iota(jnp.int32, (N_KEYS, LANES), 0).astype(F32)
    flat = flat_ref[...]

    def keep_level1(hc, sub, tops, rank):
        rank_s[hc, sub] = rank
        for k in range(PEER_TOPK):
            top_s[hc, sub, k:k + 1, :] = tops[k]

    def level2(h, sub, u):
        sa, sb = top_s[2 * h, sub], top_s[2 * h + 1, sub]
        ea_r = jnp.exp(sa - sa[0:1, :])
        eb_r = jnp.exp(sb - sb[0:1, :])
        for ka in range(PEER_TOPK):
            cand_s[u, CAND_OFF[ka]:CAND_OFF[ka] + CAND_NB[ka], :] = sa[ka:ka + 1, :] + sb[0:CAND_NB[ka], :]
        cand_s[u, CAND_N:CAND_ROWS, :] = jnp.full((CAND_ROWS - CAND_N, LANES), NEG_INF, F32)
        _, crank = _top16(cand_s[u], flat, break_ties=True)
        sel_s[u] = jnp.where(crank < float(PEER_TOPK), 1.0, 0.0)
        n_a, z = [], jnp.zeros((1, LANES), F32)
        for ka in range(PEER_TOPK):
            sel_ka = sel_s[u, CAND_OFF[ka]:CAND_OFF[ka] + CAND_NB[ka], :]
            n_a.append(jnp.sum(sel_ka, axis=0, keepdims=True))
            z = z + ea_r[ka:ka + 1, :] * jnp.sum(sel_ka * eb_r[0:CAND_NB[ka], :], axis=0, keepdims=True)
        rank_a = rank_s[2 * h, sub]
        na = jnp.zeros((N_KEYS, LANES), F32)
        for ka in range(PEER_TOPK):
            na = jnp.where(rank_a == float(ka), n_a[ka], na)
        na_ref[h, sub] = _bf16_pair(na)
        rb_ref[h, sub] = _packed(rank_s[2 * h + 1, sub])
        ea_ref[h, sub] = _bf16_pair(jnp.exp(s_s[2 * h, sub] - sa[0:1, :]))
        eb_ref[h, sub] = _packed(jnp.exp(s_s[2 * h + 1, sub] - sb[0:1, :]) / z)

    def per_subtile(sub, carry):
        def quick(h, tied):
            for hc in (2 * h, 2 * h + 1):
                q_blk = qt_s[sub, pl.ds(pl.multiple_of(hc * PEER_HALF, PEER_HALF), PEER_HALF), :]
                s = _dot(keys_ref[hc], q_blk)
                s_s[hc, sub] = s
                tops, rank = _top16(s, key_id, break_ties=False)
                keep_level1(hc, sub, tops, rank)
                tied = jnp.maximum(tied, _tied(rank))
            return tied

        tied = lax.fori_loop(0, PEER_HEADS, quick, jnp.zeros((1, LANES), F32))

        @pl.when(jnp.max(tied) > 0.0)
        def _():
            def careful(hc, c):
                keep_level1(hc, sub, *_top16(s_s[hc, sub], key_id, break_ties=True))
                return c
            lax.fori_loop(0, 2 * PEER_HEADS, careful, 0)

        def heads(q, c):
            for u in range(LEVEL2_CHAINS):
                level2(q * LEVEL2_CHAINS + u, sub, u)
            return c

        lax.fori_loop(0, PEER_HEADS // LEVEL2_CHAINS, heads, 0)
        return carry

    lax.fori_loop(0, nsub, per_subtile, 0)


KEY_TABLE_ROWS = (N_KEYS // 2, N_KEYS, N_KEYS, N_KEYS // 2)


def _key_spec(nsub, rows, index_map):
    return pl.BlockSpec((PEER_HEADS, nsub, rows, LANES), index_map)


def _peer_topk(x2t, wpqt, keys, flat, *, tt):
    n = x2t.shape[1]
    nsub = tt // LANES
    hc = 2 * PEER_HEADS
    return pl.pallas_call(
        _peer_topk_kernel,
        grid=(n // tt,),
        in_specs=[pl.BlockSpec((D_MODEL, tt), lambda i: (0, i)), _full(wpqt.shape), _full(keys.shape),
                  _full(flat.shape)],
        out_specs=[_key_spec(nsub, rows, lambda i: (0, i, 0, 0)) for rows in KEY_TABLE_ROWS],
        out_shape=[jax.ShapeDtypeStruct((PEER_HEADS, n // LANES, rows, LANES), jnp.int32)
                   for rows in KEY_TABLE_ROWS],
        scratch_shapes=[pltpu.VMEM((nsub, PEER_HEADS * PEER_DK, LANES), BF16),
                        pltpu.VMEM((hc, nsub, N_KEYS, LANES), F32),
                        pltpu.VMEM((hc, nsub, N_KEYS, LANES), F32),
                        pltpu.VMEM((hc, nsub, PEER_TOPK, LANES), F32),
                        pltpu.VMEM((LEVEL2_CHAINS, CAND_ROWS, LANES), F32),
                        pltpu.VMEM((LEVEL2_CHAINS, CAND_ROWS, LANES), F32)],
        compiler_params=_params("parallel"),
        name="peer_topk",
    )(x2t, wpqt, keys, flat)


EXP_BLK = 512
EXP_GROUPS = EXP_BLK // N_KEYS
SUBS_PER_PIECE = 2
EXP_STEPS = N_EXPERTS // (2 * EXP_BLK)


def _peer_experts_kernel(x2t_ref, x2_ref, rb_ref, na_ref, ea_ref, eb_ref, u_ref, vt_ref, ln3g_ref, ln3b_ref,
                         y_ref, acc_s, h0_s, h1_s, w0_s, w1_s):
    j = pl.program_id(1)
    nsub = x2t_ref.shape[1] // LANES
    pieces = max(nsub // SUBS_PER_PIECE, 1)
    piece_subs = nsub // pieces
    d_rows = D_MODEL // pieces
    e_rows = EXP_BLK // pieces

    @pl.when(j == 0)
    def _():
        acc_s[...] = jnp.zeros(acc_s.shape, F32)
        h1_s[...] = jnp.zeros(h1_s.shape, F32)
        w0_s[...] = jnp.zeros(w0_s.shape, jnp.int32)

    def key_row(ref, h, sub, ia):
        word = jnp.broadcast_to(ref[h, sub, pl.ds(ia, 1), :], (8, LANES))
        return jnp.tile(pltpu.bitcast(word, BF16), (N_KEYS // 16, 1))

    def gate_piece(h_s, w_s, blk, sub):
        blk = jnp.clip(blk, 0, N_EXPERTS // EXP_BLK - 1)
        for g in range(EXP_GROUPS):
            ia = blk * EXP_GROUPS + g
            gate = jnp.zeros((N_KEYS, LANES), BF16)
            for h in range(PEER_HEADS):
                eb = _unpacked(eb_ref[h, sub])
                keep = _unpacked(rb_ref[h, sub]) < key_row(na_ref, h, sub, ia)
                gate = gate + jnp.where(keep, eb, jnp.zeros_like(eb)) * key_row(ea_ref, h, sub, ia)
            act = _gelu_tanh(h_s[sub, g * N_KEYS:(g + 1) * N_KEYS, :]).astype(BF16) * gate
            w_s[sub, g * (N_KEYS // 2):(g + 1) * (N_KEYS // 2), :] = pltpu.bitcast(act, jnp.int32)

    def half_step(w_done, h_done, w_next, h_next, half, blk):
        experts = slice(half * EXP_BLK, (half + 1) * EXP_BLK)

        def piece(i, carry):
            u_rows = pl.ds(pl.multiple_of((half * EXP_BLK + i * e_rows) // 2, e_rows // 2), e_rows // 2)
            h_new = _dot(_unpacked(u_ref[u_rows, :]), x2t_ref[...])
            for s in range(nsub):
                h_next[s, pl.ds(pl.multiple_of(i * e_rows, e_rows), e_rows), :] = h_new[:, s * LANES:(s + 1) * LANES]
            rows = pl.ds(pl.multiple_of(i * d_rows, d_rows), d_rows)
            w = jnp.concatenate([_unpacked(w_done[s]) for s in range(nsub)], axis=1)
            v_rows = pl.ds(pl.multiple_of(i * (d_rows // 2), d_rows // 2), d_rows // 2)
            acc_s[rows, :] += _dot(_unpacked(vt_ref[v_rows, experts]), w)
            for k in range(piece_subs):
                gate_piece(h_done, w_next, blk, i * piece_subs + k)
            return carry

        lax.fori_loop(0, pieces, piece, 0)

    half_step(w0_s, h1_s, w1_s, h0_s, 0, 2 * j - 1)
    half_step(w1_s, h0_s, w0_s, h1_s, 1, 2 * j)

    @pl.when(j == pl.num_programs(1) - 1)
    def _():
        y_ref[...] = _layer_norm(ALPHA * x2_ref[...] + acc_s[...].T, ln3g_ref[...], ln3b_ref[...])


def _peer_experts(x2t, x2, key_arrs, u_bf, vt_bf, ln3g, ln3b, *, tt):
    n = x2.shape[0]
    nsub = tt // LANES
    return pl.pallas_call(
        _peer_experts_kernel,
        grid=(n // tt, EXP_STEPS + 1),
        in_specs=[pl.BlockSpec((D_MODEL, tt), lambda i, j: (0, i)),
                  pl.BlockSpec((tt, D_MODEL), lambda i, j: (i, 0)),
                  *[_key_spec(nsub, rows, lambda i, j: (0, i, 0, 0)) for rows in KEY_TABLE_ROWS],
                  pl.BlockSpec((EXP_BLK, D_MODEL), lambda i, j: (jnp.minimum(j, EXP_STEPS - 1), 0)),
                  pl.BlockSpec((D_MODEL // 2, 2 * EXP_BLK), lambda i, j: (0, jnp.maximum(j - 1, 0))),
                  _full(ln3g.shape), _full(ln3b.shape)],
        out_specs=pl.BlockSpec((tt, D_MODEL), lambda i, j: (i, 0)),
        out_shape=jax.ShapeDtypeStruct((n, D_MODEL), F32),
        scratch_shapes=[pltpu.VMEM((D_MODEL, tt), F32),
                        pltpu.VMEM((nsub, EXP_BLK, LANES), F32), pltpu.VMEM((nsub, EXP_BLK, LANES), F32),
                        pltpu.VMEM((nsub, EXP_BLK // 2, LANES), jnp.int32),
                        pltpu.VMEM((nsub, EXP_BLK // 2, LANES), jnp.int32)],
        compiler_params=_params("parallel", "arbitrary"),
        name="peer_experts",
    )(x2t, x2, *key_arrs, u_bf, vt_bf, ln3g, ln3b)


def _peer(x2t, x2, peer_wts, *, tt):
    wpqt, keys, flat, u_bf, vt_bf, ln3g, ln3b = peer_wts
    key_arrs = _peer_topk(x2t, wpqt, keys, flat, tt=tt)
    return _peer_experts(x2t, x2, key_arrs, u_bf, vt_bf, ln3g, ln3b, tt=tt)


def _rope_tables(pos):
    inv = ROPE_THETA ** (-jnp.arange(0, QK_ROPE, 2, dtype=F32) / QK_ROPE)
    ang = pos.astype(F32)[:, None] * inv[None, :]
    cos, sin, zero = jnp.cos(ang), jnp.sin(ang), jnp.zeros_like(ang)
    pad = jnp.zeros((pos.shape[0], LANES - QK_ROPE), F32)
    return (jnp.concatenate([cos, cos, pad], axis=1),
            jnp.concatenate([-sin, zero, pad], axis=1),
            jnp.concatenate([zero, sin, pad], axis=1))


def _row(v):
    return v.reshape(1, -1).astype(F32)


def _pack_rows(x):
    b = lax.bitcast_convert_type(x.astype(BF16), jnp.uint16).astype(jnp.uint32)
    return lax.bitcast_convert_type(b[0::2] | (b[1::2] << 16), jnp.int32)


def kernel(x_prompt, x_sample, mem_prompt, cache_ckv, cache_krope, cache_mem_k, cache_mem_v, page_table,
           w_in, q_norm_g, kv_norm_g, w_uq, w_uk, w_uv, gm_norm_g, gm_norm_b, gm_ws, gm_bs, attn_out_g,
           gm_out_g, w_out, ln1_g, ln1_b, w_xq, w_mk, w_mv, w_xo, ln2_g, ln2_b, w_pq, peer_keys, peer_u,
           peer_v, ln3_g, ln3_b):
    batch, seq = x_prompt.shape[:2]
    nb = x_sample.shape[0]
    past_len = page_table.shape[1] * PAGE_SIZE

    kr_pad = jnp.zeros((D_MODEL, LANES - QK_ROPE), F32)
    w_in_x = jnp.concatenate([w_in[:, :Q_RANK + KV_RANK + QK_ROPE], kr_pad,
                              w_in[:, Q_RANK + KV_RANK + QK_ROPE:]], axis=1).astype(BF16)
    uq_nope = w_uq[:, :, :QK_NOPE].reshape(Q_RANK, MLA_HEADS * QK_NOPE)
    uq_rope = jnp.pad(w_uq[:, :, QK_NOPE:], ((0, 0), (0, 0), (0, LANES - QK_ROPE)))
    w_uq_x = jnp.concatenate([uq_nope, uq_rope.reshape(Q_RANK, MLA_HEADS * LANES)], axis=1).astype(BF16)
    eye = jnp.eye(MLA_HEADS, dtype=F32)
    w_uk_bd = jnp.einsum('rhd,hg->hdgr', w_uk, eye).reshape(MLA_HEADS * QK_NOPE, MLA_HEADS * KV_RANK).astype(BF16)
    w_uv_bd = jnp.einsum('rhd,hg->hrgd', w_uv, eye).reshape(MLA_HEADS * KV_RANK, MLA_WIDTH).astype(BF16)
    w_uv_t = jnp.transpose(w_uv, (1, 2, 0)).astype(BF16)
    proj_wts = (w_in_x, _row(q_norm_g), _row(kv_norm_g), w_uq_x, w_uk_bd, _row(gm_norm_g), _row(gm_norm_b))
    g_attn_col = jnp.broadcast_to(attn_out_g.astype(F32)[:, None], (MLA_WIDTH, Q_BLK))
    bias_tile = jnp.repeat(gm_bs.T, GM_WIDTH // GM_GROUPS, axis=1).astype(F32)
    w_out_bf, w_xq_bf = w_out.astype(BF16), w_xq.reshape(D_MODEL, X_WIDTH).astype(BF16)
    w_xo_bf = w_xo.reshape(X_WIDTH, D_MODEL).astype(BF16)
    post_tail = (w_xo_bf, _row(ln2_g), _row(ln2_b))
    post_wts = (gm_ws.astype(F32), bias_tile, _row(gm_out_g), w_out_bf, _row(ln1_g), _row(ln1_b), w_xq_bf) + post_tail
    ws0 = jnp.repeat(gm_ws[:, 0, 0], GM_WIDTH // GM_GROUPS)
    bs0 = jnp.repeat(gm_bs[:, 0], GM_WIDTH // GM_GROUPS)
    sample_wts = (w_uv_bd, _row(attn_out_g), _row(ws0), _row(bs0), _row(gm_out_g), w_out_bf, _row(ln1_g),
                  _row(ln1_b), w_xq_bf) + post_tail
    flat = np.full((CAND_ROWS,), 1e8, np.float32)
    for ka in range(PEER_TOPK):
        flat[CAND_OFF[ka]:CAND_OFF[ka] + CAND_NB[ka]] = ka * PEER_TOPK + np.arange(CAND_NB[ka])
    flat = jnp.asarray(np.broadcast_to(flat[:, None], (CAND_ROWS, LANES)))
    peer_wts = (w_pq.reshape(D_MODEL, PEER_HEADS * PEER_DK).T.astype(BF16),
                peer_keys.reshape(2 * PEER_HEADS, N_KEYS, PEER_HALF).astype(BF16), flat,
                _pack_rows(peer_u), _pack_rows(peer_v.T), _row(ln3_g), _row(ln3_b))

    n_p = batch * seq
    xp = x_prompt.reshape(n_p, D_MODEL)
    tm = 512
    qlat, qrope, kcat, ckv_p, krope_p, u_p, v_p, ckvt = _proj(
        xp, _rope_tables(jnp.arange(seq)), proj_wts, tm=tm, seq_blocks=seq // tm, emit_kt=True)
    a_p = _attn_prompt(qlat, qrope, kcat, ckvt, w_uv_t, g_attn_col, batch=batch, seq=seq)
    mk_p, mv_p = _memkv(mem_prompt.reshape(batch * MEM_TOKENS, D_MODEL),
                        w_mk.reshape(D_MODEL, X_WIDTH).astype(BF16), w_mv.reshape(D_MODEL, X_WIDTH).astype(BF16))
    x2_p, x2t_p = _post_prompt(xp, a_p, u_p, v_p, mk_p, mv_p, post_wts, batch=batch, seq=seq)
    y_p = _peer(x2t_p, x2_p, peer_wts, tt=512)

    xs = x_sample.reshape(nb, D_MODEL)
    pos_s = jnp.full((nb,), past_len, jnp.int32)
    qlat_s, qrope_s, _, ckv_s, krope_s, u_s, v_s = _proj(xs, _rope_tables(pos_s), proj_wts, tm=nb, seq_blocks=1,
                                                          emit_kt=False)
    q_cat = jnp.concatenate([qlat_s.reshape(nb, MLA_HEADS, KV_RANK), qrope_s.reshape(nb, MLA_HEADS, LANES)], axis=2)
    q_pad = jnp.pad(q_cat, ((0, 0), (0, LANES - MLA_HEADS), (0, 0)))
    o_lat_s = _attn_decode(page_table, q_pad, cache_ckv, cache_krope, ckv_s, krope_s)
    x2_s = _post_sample(xs, o_lat_s.reshape(nb, MLA_HEADS * KV_RANK), u_s, v_s,
                        cache_mem_k.reshape(nb, MEM_TOKENS, X_WIDTH), cache_mem_v.reshape(nb, MEM_TOKENS, X_WIDTH),
                        sample_wts)
    y_s = _peer(x2_s.T.astype(BF16), x2_s, peer_wts, tt=nb)

    return (y_p.reshape(batch, seq, D_MODEL), y_s.reshape(nb, 1, D_MODEL),
            ckv_p.reshape(batch, seq, KV_RANK), krope_p.reshape(batch, seq, QK_ROPE),
            mk_p.reshape(batch, MEM_TOKENS, X_HEADS, X_HEAD_DIM), mv_p.reshape(batch, MEM_TOKENS, X_HEADS, X_HEAD_DIM),
            ckv_s.reshape(nb, 1, KV_RANK), krope_s.reshape(nb, 1, QK_ROPE), v_s.reshape(nb, 1, GM_WIDTH))
```

```python
import functools

import jax
import jax.numpy as jnp
import numpy as np
from jax import lax
from jax.experimental import pallas as pl
from jax.experimental.pallas import tpu as pltpu

F32 = jnp.float32
BF16 = jnp.bfloat16

D_MODEL = 1024
MLA_HEADS = 8
QK_NOPE = 64
QK_ROPE = 32
V_HEAD = 64
Q_RANK = 256
KV_RANK = 128
MLA_WIDTH = MLA_HEADS * V_HEAD
MLA_SCALE = (QK_NOPE + QK_ROPE) ** -0.5
Q_SCALE = float(MLA_SCALE * np.log2(np.e))
ROPE_THETA = 10000.0
GM_WIDTH = D_MODEL // 2
GM_GROUPS = 4
GM_CHUNK = 128
MEM_TOKENS = 256
X_HEADS = 4
X_HEAD_DIM = 128
X_WIDTH = X_HEADS * X_HEAD_DIM
X_SCALE = X_HEAD_DIM ** -0.5
PEER_HEADS = 8
N_KEYS = 128
N_EXPERTS = N_KEYS * N_KEYS
PEER_TOPK = 16
PEER_DK = 256
PEER_HALF = PEER_DK // 2
PAGE_SIZE = 128
DEPTH = 1
ALPHA = (2.0 * DEPTH) ** 0.25
EPS = 1e-5

LANES = 128
VMEM_LIMIT = 56 * 1024 * 1024

CAND_NB = tuple(PEER_TOPK // (ka + 1) for ka in range(PEER_TOPK))
CAND_OFF = tuple(int(sum(CAND_NB[:ka])) for ka in range(PEER_TOPK))
CAND_N = int(sum(CAND_NB))
CAND_ROWS = 56
LEVEL2_CHAINS = 4
NEG_INF = float("-inf")


def _dot(a, b):
    return jnp.dot(a, b, preferred_element_type=F32)


def _dot_nt(a, b):
    return lax.dot_general(a, b, (((1,), (1,)), ((), ())), preferred_element_type=F32)


def _dot_tn(a, b):
    return lax.dot_general(a, b, (((0,), (0,)), ((), ())), preferred_element_type=F32)


def _layer_norm(x, g, b):
    mu = jnp.mean(x, -1, keepdims=True)
    var = jnp.mean(jnp.square(x - mu), -1, keepdims=True)
    return (x - mu) * lax.rsqrt(var + EPS) * g + b


def _rms_norm(x, g):
    return x * lax.rsqrt(jnp.mean(jnp.square(x), -1, keepdims=True) + EPS) * g


_GELU_K1 = float(-2.0 * np.sqrt(2.0 / np.pi) * np.log2(np.e))
_GELU_K2 = float(0.044715 * _GELU_K1)


def _gelu_tanh(x):
    return x / (1.0 + jnp.exp2(x * (_GELU_K1 + _GELU_K2 * (x * x))))


def _params(*sem):
    return pltpu.CompilerParams(dimension_semantics=sem, vmem_limit_bytes=VMEM_LIMIT)


def _full(shape):
    n = len(shape)
    return pl.BlockSpec(shape, lambda *_: (0,) * n)


def _rope(x, c, s_lo, s_hi):
    width = x.shape[-1]
    return x * c + pltpu.roll(x, width - 16, 1) * s_lo + pltpu.roll(x, 16, 1) * s_hi


def _proj_kernel(x_ref, c_ref, slo_ref, shi_ref, w_in_ref, qg_ref, kvg_ref, w_uq_ref, w_uk_ref,
                 gmg_ref, gmb_ref, qlat_ref, qrope_ref, kcat_ref, ckv_ref, krope_ref, u_ref, v_ref,
                 *maybe_ckvt_ref):
    h = _dot(x_ref[...].astype(BF16), w_in_ref[...])
    c, s_lo, s_hi = c_ref[...], slo_ref[...], shi_ref[...]

    cq = _rms_norm(h[:, :Q_RANK], qg_ref[...])
    q_all = _dot(cq.astype(BF16), w_uq_ref[...])
    q_nope = q_all[:, :MLA_HEADS * QK_NOPE]
    q_lat = _dot(q_nope.astype(BF16), w_uk_ref[...])
    qlat_ref[...] = (q_lat * Q_SCALE).astype(BF16)
    q_rope = _rope(q_all[:, MLA_HEADS * QK_NOPE:], jnp.tile(c, (1, MLA_HEADS)),
                   jnp.tile(s_lo, (1, MLA_HEADS)), jnp.tile(s_hi, (1, MLA_HEADS)))
    qrope_ref[...] = (q_rope * Q_SCALE).astype(BF16)

    ckv = _rms_norm(h[:, Q_RANK:Q_RANK + KV_RANK], kvg_ref[...])
    ckv_ref[...] = ckv
    k_rot = _rope(h[:, Q_RANK + KV_RANK:Q_RANK + KV_RANK + LANES], c, s_lo, s_hi)
    krope_ref[...] = k_rot[:, :QK_ROPE]
    kcat_ref[...] = jnp.concatenate([ckv, k_rot], axis=1).astype(BF16)
    for ckvt_ref in maybe_ckvt_ref:
        for j in range(ckvt_ref.shape[0]):
            ckvt_ref[j] = ckv[j * 256:(j + 1) * 256, :].T.astype(BF16)

    uv = jax.nn.gelu(h[:, Q_RANK + KV_RANK + LANES:])
    u_ref[...] = uv[:, :GM_WIDTH]
    v_ref[...] = _layer_norm(uv[:, GM_WIDTH:], gmg_ref[...], gmb_ref[...])


def _proj(x, tables, wts, *, tm, seq_blocks, emit_kt):
    n = x.shape[0]
    row = lambda w: pl.BlockSpec((tm, w), lambda i: (i, 0))
    tab = pl.BlockSpec((tm, LANES), lambda i: (i % seq_blocks, 0))
    w_in, qg, kvg, w_uq, w_uk, gmg, gmb = wts
    out_shape = [
        jax.ShapeDtypeStruct((n, MLA_HEADS * KV_RANK), BF16),
        jax.ShapeDtypeStruct((n, MLA_HEADS * LANES), BF16),
        jax.ShapeDtypeStruct((n, 2 * LANES), BF16),
        jax.ShapeDtypeStruct((n, KV_RANK), F32),
        jax.ShapeDtypeStruct((n, QK_ROPE), F32),
        jax.ShapeDtypeStruct((n, GM_WIDTH), F32),
        jax.ShapeDtypeStruct((n, GM_WIDTH), F32),
    ]
    out_specs = [row(MLA_HEADS * KV_RANK), row(MLA_HEADS * LANES), row(2 * LANES), row(KV_RANK),
                 row(QK_ROPE), row(GM_WIDTH), row(GM_WIDTH)]
    if emit_kt:
        out_shape.append(jax.ShapeDtypeStruct((n // 256, KV_RANK, 256), BF16))
        out_specs.append(pl.BlockSpec((tm // 256, KV_RANK, 256), lambda i: (i, 0, 0)))
    return pl.pallas_call(
        _proj_kernel,
        grid=(n // tm,),
        in_specs=[row(D_MODEL), tab, tab, tab, _full(w_in.shape), _full(qg.shape), _full(kvg.shape),
                  _full(w_uq.shape), _full(w_uk.shape), _full(gmg.shape), _full(gmb.shape)],
        out_specs=out_specs,
        out_shape=out_shape,
        compiler_params=_params("parallel"),
        name="proj",
    )(x, *tables, w_in, qg, kvg, w_uq, w_uk, gmg, gmb)


Q_BLK = 256
KV_BLK = 256


ATTN_COLS = MLA_HEADS * Q_BLK


def _attn_kernel(qlat_ref, qrope_ref, kcat_ref, ckvt_ref, wuvt_ref, g_ref, a_ref, q_s, m_s, l_s, acc_s):
    qi = pl.program_id(1)
    ql, qr = qlat_ref[...], qrope_ref[...]
    hpc = ATTN_COLS // Q_BLK
    for h in range(MLA_HEADS):
        q_s[h // hpc, (h % hpc) * Q_BLK:(h % hpc + 1) * Q_BLK, :] = jnp.concatenate(
            [ql[:, h * LANES:(h + 1) * LANES], qr[:, h * LANES:(h + 1) * LANES]], axis=1)
    m_s[...] = jnp.full(m_s.shape, NEG_INF, F32)
    l_s[...] = jnp.zeros(l_s.shape, F32)
    acc_s[...] = jnp.zeros(acc_s.shape, F32)
    q_pos = qi * Q_BLK + (lax.broadcasted_iota(jnp.int32, (KV_BLK, ATTN_COLS), 1) & (Q_BLK - 1))
    k_off = lax.broadcasted_iota(jnp.int32, (KV_BLK, ATTN_COLS), 0)

    def step(j, diagonal):
        k = kcat_ref[pl.ds(pl.multiple_of(j * KV_BLK, KV_BLK), KV_BLK), :]
        v_t = ckvt_ref[j]
        for c in range(MLA_HEADS // hpc):
            st = _dot_nt(k, q_s[c])
            if diagonal:
                st = jnp.where(q_pos >= k_off + j * KV_BLK, st, NEG_INF)
            m_old = m_s[c]
            m_new = jnp.maximum(m_old, jnp.max(st, axis=0, keepdims=True))
            alpha = jnp.exp2(m_old - m_new)
            p = jnp.exp2(st - m_new)
            l_s[c] = alpha * l_s[c] + jnp.sum(p, axis=0, keepdims=True)
            acc_s[c] = alpha * acc_s[c] + _dot(v_t, p.astype(BF16))
            m_s[c] = m_new

    last = (qi * Q_BLK) // KV_BLK

    def visible(j, carry):
        step(j, False)
        return carry

    lax.fori_loop(0, last, visible, 0)
    step(last, True)

    o_t = (acc_s[...] / l_s[...]).astype(BF16)
    om_t = jnp.concatenate(
        [_dot(wuvt_ref[h], o_t[h // hpc, :, (h % hpc) * Q_BLK:(h % hpc + 1) * Q_BLK])
         for h in range(MLA_HEADS)], axis=0)
    ms = jnp.mean(jnp.square(om_t), axis=0, keepdims=True)
    a_t = om_t * lax.rsqrt(ms + EPS) * g_ref[...]
    a_ref[...] = a_t.T.astype(BF16)


def _attn_prompt(qlat, qrope, kcat, ckvt, wuvt, g_attn, *, batch, seq):
    n = batch * seq
    nq = seq // Q_BLK
    chains = MLA_HEADS * Q_BLK // ATTN_COLS
    return pl.pallas_call(
        _attn_kernel,
        grid=(batch, nq),
        in_specs=[
            pl.BlockSpec((Q_BLK, MLA_HEADS * KV_RANK), lambda b, i: (b * nq + i, 0)),
            pl.BlockSpec((Q_BLK, MLA_HEADS * LANES), lambda b, i: (b * nq + i, 0)),
            pl.BlockSpec((None, seq, 2 * LANES), lambda b, i: (b, 0, 0)),
            pl.BlockSpec((None, seq // KV_BLK, KV_RANK, KV_BLK), lambda b, i: (b, 0, 0, 0)),
            _full(wuvt.shape), _full(g_attn.shape),
        ],
        out_specs=pl.BlockSpec((Q_BLK, MLA_WIDTH), lambda b, i: (b * nq + i, 0)),
        out_shape=jax.ShapeDtypeStruct((n, MLA_WIDTH), BF16),
        scratch_shapes=[pltpu.VMEM((chains, ATTN_COLS, 2 * LANES), BF16), pltpu.VMEM((chains, 1, ATTN_COLS), F32),
                        pltpu.VMEM((chains, 1, ATTN_COLS), F32), pltpu.VMEM((chains, KV_RANK, ATTN_COLS), F32)],
        compiler_params=_params("parallel", "arbitrary"),
        name="attn_prompt",
    )(qlat, qrope, kcat.reshape(batch, seq, 2 * LANES),
      ckvt.reshape(batch, seq // KV_BLK, KV_RANK, KV_BLK), wuvt, g_attn)


PAGES_PER_STEP = 8


SEQS_PER_STEP = 2


def _decode_kernel(pt_ref, q_ref, *refs):
    del pt_ref
    n_pg = SEQS_PER_STEP * PAGES_PER_STEP
    ckv_refs, kr_refs = refs[:n_pg], refs[n_pg:2 * n_pg]
    ckvn_ref, krn_ref, o_ref, m_s, l_s, acc_s = refs[2 * n_pg:]
    c = pl.program_id(1)

    @pl.when(c == 0)
    def _():
        m_s[...] = jnp.full(m_s.shape, NEG_INF, F32)
        l_s[...] = jnp.zeros(l_s.shape, F32)
        acc_s[...] = jnp.zeros(acc_s.shape, F32)

    def keys(ckv, kr):
        pad = jnp.zeros((kr.shape[0], LANES - QK_ROPE), F32)
        return jnp.concatenate([ckv, kr, pad], axis=1).astype(BF16)

    def update(b, kcat, valid_rows):
        st = _dot_nt(kcat, q_ref[b])
        if valid_rows is not None:
            st = jnp.where(lax.broadcasted_iota(jnp.int32, st.shape, 0) < valid_rows, st, NEG_INF)
        m_old = m_s[b]
        m_new = jnp.maximum(m_old, jnp.max(st, axis=0, keepdims=True))
        alpha = jnp.exp2(m_old - m_new)
        p = jnp.exp2(st - m_new)
        l_s[b] = alpha * l_s[b] + jnp.sum(p, axis=0, keepdims=True)
        acc_s[b] = alpha * acc_s[b] + _dot_tn(kcat[:, :KV_RANK], p.astype(BF16))
        m_s[b] = m_new

    for b in range(SEQS_PER_STEP):
        pages = slice(b * PAGES_PER_STEP, (b + 1) * PAGES_PER_STEP)
        update(b, keys(jnp.concatenate([r[...] for r in ckv_refs[pages]], axis=0),
                       jnp.concatenate([r[...] for r in kr_refs[pages]], axis=0)), None)

    @pl.when(c == pl.num_programs(1) - 1)
    def _():
        for b in range(SEQS_PER_STEP):
            update(b, keys(jnp.broadcast_to(ckvn_ref[b], (16, KV_RANK)),
                           jnp.broadcast_to(krn_ref[b], (16, QK_ROPE))), 1)
            o_t = acc_s[b] / l_s[b]
            o_ref[b] = o_t.T[:MLA_HEADS, :]


def _attn_decode(page_table, q_pad, cache_ckv, cache_krope, ckv_new, krope_new):
    nb, n_pages = page_table.shape
    steps = n_pages // PAGES_PER_STEP
    sq = SEQS_PER_STEP

    def page_spec(width, b, i):
        return pl.BlockSpec((None, PAGE_SIZE, width),
                            lambda g, c, pt: (pt[g * sq + b, c * PAGES_PER_STEP + i], 0, 0))

    def seq_spec(*dims):
        return pl.BlockSpec((sq,) + dims, lambda g, c, pt: (g,) + (0,) * len(dims))

    in_specs = [seq_spec(LANES, 2 * LANES)]
    in_specs += [page_spec(KV_RANK, b, i) for b in range(sq) for i in range(PAGES_PER_STEP)]
    in_specs += [page_spec(QK_ROPE, b, i) for b in range(sq) for i in range(PAGES_PER_STEP)]
    in_specs += [seq_spec(1, KV_RANK), seq_spec(1, QK_ROPE)]
    n_pg = sq * PAGES_PER_STEP
    return pl.pallas_call(
        _decode_kernel,
        grid_spec=pltpu.PrefetchScalarGridSpec(
            num_scalar_prefetch=1,
            grid=(nb // sq, steps),
            in_specs=in_specs,
            out_specs=seq_spec(MLA_HEADS, KV_RANK),
            scratch_shapes=[pltpu.VMEM((sq, 1, LANES), F32), pltpu.VMEM((sq, 1, LANES), F32),
                            pltpu.VMEM((sq, KV_RANK, LANES), F32)],
        ),
        out_shape=jax.ShapeDtypeStruct((nb, MLA_HEADS, KV_RANK), F32),
        compiler_params=_params("parallel", "arbitrary"),
        name="attn_decode",
    )(page_table, q_pad, *([cache_ckv] * n_pg), *([cache_krope] * n_pg),
      ckv_new.reshape(nb, 1, KV_RANK), krope_new.reshape(nb, 1, QK_ROPE))


def _memkv_kernel(mem_ref, wk_ref, wv_ref, mk_ref, mv_ref):
    m = mem_ref[...].astype(BF16)
    mk_ref[...] = _dot(m, wk_ref[...])
    mv_ref[...] = _dot(m, wv_ref[...])


def _memkv(mem, w_mk, w_mv):
    n = mem.shape[0]
    tm = 512
    return pl.pallas_call(
        _memkv_kernel,
        grid=(n // tm,),
        in_specs=[pl.BlockSpec((tm, D_MODEL), lambda i: (i, 0)), _full(w_mk.shape), _full(w_mv.shape)],
        out_specs=[pl.BlockSpec((tm, X_WIDTH), lambda i: (i, 0))] * 2,
        out_shape=[jax.ShapeDtypeStruct((n, X_WIDTH), F32)] * 2,
        compiler_params=_params("parallel"),
        name="memkv",
    )(mem, w_mk, w_mv)


POST_ROWS = 256


def _softmax_rows(s):
    e = jnp.exp(s - jnp.max(s, axis=-1, keepdims=True))
    return e / jnp.sum(e, axis=-1, keepdims=True)


def _mix_and_ln1(x, a_bf, o_gm, gmog, w_out, ln1g, ln1b):
    gm_n = _rms_norm(o_gm, gmog)
    y = jnp.concatenate([a_bf, gm_n.astype(BF16)], axis=1)
    return _layer_norm(ALPHA * x + _dot(y, w_out), ln1g, ln1b)


def _post_prompt_kernel(x_ref, a_ref, u_ref, v_ref, ws_ref, bias_ref, gmog_ref, wout_ref, ln1g_ref, ln1b_ref,
                        wxq_ref, mk_ref, mv_ref, wxo_ref, ln2g_ref, ln2b_ref, x2_ref, x2t_ref):
    tril = (lax.broadcasted_iota(jnp.int32, (GM_CHUNK, GM_CHUNK), 0)
            >= lax.broadcasted_iota(jnp.int32, (GM_CHUNK, GM_CHUNK), 1))
    w_s = [jnp.where(tril, ws_ref[g], 0.0).astype(BF16) for g in range(GM_GROUPS)]
    chunks = []
    for c in range(POST_ROWS // GM_CHUNK):
        rows = slice(c * GM_CHUNK, (c + 1) * GM_CHUNK)
        v_c = v_ref[rows, :].astype(BF16)
        s = jnp.concatenate([_dot(w_s[g], v_c[:, g * LANES:(g + 1) * LANES]) for g in range(GM_GROUPS)],
                            axis=1) + bias_ref[...]
        chunks.append(u_ref[rows, :] * s)
    o_gm = jnp.concatenate(chunks, axis=0)
    x1 = _mix_and_ln1(x_ref[...], a_ref[...], o_gm, gmog_ref[...], wout_ref[...], ln1g_ref[...], ln1b_ref[...])

    q = _dot(x1.astype(BF16), wxq_ref[...]).astype(BF16)
    mk, mv = mk_ref[...].astype(BF16), mv_ref[...].astype(BF16)
    heads = []
    for h in range(X_HEADS):
        cs = slice(h * X_HEAD_DIM, (h + 1) * X_HEAD_DIM)
        p = _softmax_rows(_dot_nt(q[:, cs], mk[:, cs]) * X_SCALE)
        heads.append(_dot(p.astype(BF16), mv[:, cs]))
    o = jnp.concatenate(heads, axis=1).astype(BF16)
    x2 = _layer_norm(ALPHA * x1 + _dot(o, wxo_ref[...]), ln2g_ref[...], ln2b_ref[...])
    x2_ref[...] = x2
    x2t_ref[...] = x2.T.astype(BF16)


def _post_prompt(x, a, u, v, mk, mv, wts, *, batch, seq):
    n = batch * seq
    nb = seq // POST_ROWS
    row = lambda w: pl.BlockSpec((POST_ROWS, w), lambda b, i: (b * nb + i, 0))
    mem = pl.BlockSpec((MEM_TOKENS, X_WIDTH), lambda b, i: (b, 0))
    return pl.pallas_call(
        _post_prompt_kernel,
        grid=(batch, nb),
        in_specs=[row(D_MODEL), row(MLA_WIDTH), row(GM_WIDTH), row(GM_WIDTH)]
                 + [_full(w.shape) for w in wts[:6]] + [_full(wts[6].shape), mem, mem]
                 + [_full(w.shape) for w in wts[7:]],
        out_specs=[row(D_MODEL), pl.BlockSpec((None, D_MODEL, POST_ROWS), lambda b, i: (b * nb + i, 0, 0))],
        out_shape=[jax.ShapeDtypeStruct((n, D_MODEL), F32),
                   jax.ShapeDtypeStruct((n // POST_ROWS, D_MODEL, POST_ROWS), BF16)],
        compiler_params=_params("parallel", "parallel"),
        name="post_prompt",
    )(x, a, u, v, *wts[:7], mk, mv, *wts[7:])


SAMPLE_ROWS = 8


def _post_sample_kernel(x_ref, o_ref, u_ref, v_ref, wuv_ref, ag_ref, ws0_ref, bs0_ref, gmog_ref, wout_ref,
                        ln1g_ref, ln1b_ref, wxq_ref, mk_ref, mv_ref, wxo_ref, ln2g_ref, ln2b_ref, x2_ref):
    o_mla = _dot(o_ref[...].astype(BF16), wuv_ref[...])
    a = _rms_norm(o_mla, ag_ref[...]).astype(BF16)
    o_gm = u_ref[...] * (ws0_ref[...] * v_ref[...] + bs0_ref[...])
    x1 = _mix_and_ln1(x_ref[...], a, o_gm, gmog_ref[...], wout_ref[...], ln1g_ref[...], ln1b_ref[...])

    q = _dot(x1.astype(BF16), wxq_ref[...])
    lane_head = lax.broadcasted_iota(jnp.int32, (LANES, X_WIDTH), 1) // X_HEAD_DIM
    on_head = lane_head == lax.broadcasted_iota(jnp.int32, (LANES, X_WIDTH), 0)
    rows = []
    for j in range(SAMPLE_ROWS):
        q_bd = jnp.where(on_head, q[j:j + 1, :], 0.0).astype(BF16)
        s = _dot_nt(mk_ref[j].astype(BF16), q_bd) * X_SCALE
        e = jnp.exp(s - jnp.max(s, axis=0, keepdims=True))
        p = e / jnp.sum(e, axis=0, keepdims=True)
        o_all = _dot_tn(p.astype(BF16), mv_ref[j].astype(BF16))
        rows.append(jnp.sum(jnp.where(on_head, o_all, 0.0), axis=0, keepdims=True))
    o = jnp.concatenate(rows, axis=0).astype(BF16)
    x2_ref[...] = _layer_norm(ALPHA * x1 + _dot(o, wxo_ref[...]), ln2g_ref[...], ln2b_ref[...])


def _post_sample(x, o_lat, u, v, mk, mv, wts):
    n = x.shape[0]
    row = lambda w: pl.BlockSpec((SAMPLE_ROWS, w), lambda i: (i, 0))
    mem = pl.BlockSpec((SAMPLE_ROWS, MEM_TOKENS, X_WIDTH), lambda i: (i, 0, 0))
    return pl.pallas_call(
        _post_sample_kernel,
        grid=(n // SAMPLE_ROWS,),
        in_specs=[row(D_MODEL), row(MLA_HEADS * KV_RANK), row(GM_WIDTH), row(GM_WIDTH)]
                 + [_full(w.shape) for w in wts[:9]] + [mem, mem] + [_full(w.shape) for w in wts[9:]],
        out_specs=row(D_MODEL),
        out_shape=jax.ShapeDtypeStruct((n, D_MODEL), F32),
        compiler_params=_params("parallel"),
        name="post_sample",
    )(x, o_lat, u, v, *wts[:9], mk, mv, *wts[9:])


def _top16(val, row_id, *, break_ties):
    rank = jnp.full(val.shape, 127.0, F32)
    tops = []
    for k in range(PEER_TOPK):
        m = jnp.max(val, axis=0, keepdims=True)
        hit = val == m
        if break_ties:
            hit = row_id == jnp.min(jnp.where(hit, row_id, 1e9), axis=0, keepdims=True)
        val = jnp.where(hit, NEG_INF, val)
        rank = jnp.where(hit, float(k), rank)
        tops.append(m)
    return tops, rank


def _tied(rank):
    marked = jnp.sum(jnp.where(rank < float(PEER_TOPK), 1.0, 0.0), axis=0, keepdims=True)
    return marked - float(PEER_TOPK)


def _packed(x):
    return pltpu.bitcast(x.astype(BF16), jnp.int32)


def _unpacked(w):
    return pltpu.bitcast(w, BF16)


def _bf16_pair(x):
    u = pltpu.bitcast(x, jnp.int32)
    hi = lax.shift_right_logical(u + 0x7FFF + (lax.shift_right_logical(u, 16) & 1), 16)
    return hi | lax.shift_left(hi, 16)


def _peer_topk_kernel(x2t_ref, wpqt_ref, keys_ref, flat_ref, rb_ref, na_ref, ea_ref, eb_ref,
                      qt_s, s_s, rank_s, top_s, cand_s, sel_s):
    nsub = x2t_ref.shape[0] * x2t_ref.shape[2] // LANES
    qt = _dot(wpqt_ref[...], _x2t_block(x2t_ref)).astype(BF16)
    for sub in range(nsub):
        qt_s[sub] = qt[:, sub * LANES:(sub + 1) * LANES]
    key_id = lax.broadcasted_iota(jnp.int32, (N_KEYS, LANES), 0).astype(F32)
    flat = flat_ref[...]

    def keep_level1(hc, sub, tops, rank):
        rank_s[hc, sub] = rank
        for k in range(PEER_TOPK):
            top_s[hc, sub, k:k + 1, :] = tops[k]

    def level2(h, sub, u):
        sa, sb = top_s[2 * h, sub], top_s[2 * h + 1, sub]
        ea_r = jnp.exp(sa - sa[0:1, :])
        eb_r = jnp.exp(sb - sb[0:1, :])
        for ka in range(PEER_TOPK):
            cand_s[u, CAND_OFF[ka]:CAND_OFF[ka] + CAND_NB[ka], :] = sa[ka:ka + 1, :] + sb[0:CAND_NB[ka], :]
        cand_s[u, CAND_N:CAND_ROWS, :] = jnp.full((CAND_ROWS - CAND_N, LANES), NEG_INF, F32)
        _, crank = _top16(cand_s[u], flat, break_ties=True)
        sel_s[u] = jnp.where(crank < float(PEER_TOPK), 1.0, 0.0)
        n_a, z = [], jnp.zeros((1, LANES), F32)
        for ka in range(PEER_TOPK):
            sel_ka = sel_s[u, CAND_OFF[ka]:CAND_OFF[ka] + CAND_NB[ka], :]
            n_a.append(jnp.sum(sel_ka, axis=0, keepdims=True))
            z = z + ea_r[ka:ka + 1, :] * jnp.sum(sel_ka * eb_r[0:CAND_NB[ka], :], axis=0, keepdims=True)
        rank_a = rank_s[2 * h, sub]
        na = jnp.zeros((N_KEYS, LANES), F32)
        for ka in range(PEER_TOPK):
            na = jnp.where(rank_a == float(ka), n_a[ka], na)
        na_ref[h, sub] = _bf16_pair(na)
        rb_ref[h, sub] = _packed(rank_s[2 * h + 1, sub])
        ea_ref[h, sub] = _bf16_pair(jnp.exp(s_s[2 * h, sub] - sa[0:1, :]))
        eb_ref[h, sub] = _packed(jnp.exp(s_s[2 * h + 1, sub] - sb[0:1, :]) / z)

    def per_subtile(sub, carry):
        def quick(h, tied):
            for hc in (2 * h, 2 * h + 1):
                q_blk = qt_s[sub, pl.ds(pl.multiple_of(hc * PEER_HALF, PEER_HALF), PEER_HALF), :]
                s = _dot(keys_ref[hc], q_blk)
                s_s[hc, sub] = s
                tops, rank = _top16(s, key_id, break_ties=False)
                keep_level1(hc, sub, tops, rank)
                tied = jnp.maximum(tied, _tied(rank))
            return tied

        tied = lax.fori_loop(0, PEER_HEADS, quick, jnp.zeros((1, LANES), F32))

        @pl.when(jnp.max(tied) > 0.0)
        def _():
            def careful(hc, c):
                keep_level1(hc, sub, *_top16(s_s[hc, sub], key_id, break_ties=True))
                return c
            lax.fori_loop(0, 2 * PEER_HEADS, careful, 0)

        def heads(q, c):
            for u in range(LEVEL2_CHAINS):
                level2(q * LEVEL2_CHAINS + u, sub, u)
            return c

        lax.fori_loop(0, PEER_HEADS // LEVEL2_CHAINS, heads, 0)
        return carry

    lax.fori_loop(0, nsub, per_subtile, 0)


KEY_TABLE_ROWS = (N_KEYS // 2, N_KEYS, N_KEYS, N_KEYS // 2)


def _key_spec(nsub, rows, index_map):
    return pl.BlockSpec((PEER_HEADS, nsub, rows, LANES), index_map)


def _x2t_spec(x2t, tt, index_map):
    return pl.BlockSpec((tt // x2t.shape[2], D_MODEL, x2t.shape[2]), index_map)


def _x2t_block(x2t_ref):
    return jnp.concatenate([x2t_ref[t] for t in range(x2t_ref.shape[0])], axis=1)


def _peer_topk(x2t, wpqt, keys, flat, *, tt):
    n = x2t.shape[0] * x2t.shape[2]
    nsub = tt // LANES
    hc = 2 * PEER_HEADS
    return pl.pallas_call(
        _peer_topk_kernel,
        grid=(n // tt,),
        in_specs=[_x2t_spec(x2t, tt, lambda i: (i, 0, 0)), _full(wpqt.shape), _full(keys.shape),
                  _full(flat.shape)],
        out_specs=[_key_spec(nsub, rows, lambda i: (0, i, 0, 0)) for rows in KEY_TABLE_ROWS],
        out_shape=[jax.ShapeDtypeStruct((PEER_HEADS, n // LANES, rows, LANES), jnp.int32)
                   for rows in KEY_TABLE_ROWS],
        scratch_shapes=[pltpu.VMEM((nsub, PEER_HEADS * PEER_DK, LANES), BF16),
                        pltpu.VMEM((hc, nsub, N_KEYS, LANES), F32),
                        pltpu.VMEM((hc, nsub, N_KEYS, LANES), F32),
                        pltpu.VMEM((hc, nsub, PEER_TOPK, LANES), F32),
                        pltpu.VMEM((LEVEL2_CHAINS, CAND_ROWS, LANES), F32),
                        pltpu.VMEM((LEVEL2_CHAINS, CAND_ROWS, LANES), F32)],
        compiler_params=_params("parallel"),
        name="peer_topk",
    )(x2t, wpqt, keys, flat)


EXP_BLK = 512
EXP_GROUPS = EXP_BLK // N_KEYS
SUBS_PER_PIECE = 2
EXP_STEPS = N_EXPERTS // (2 * EXP_BLK)


def _peer_experts_kernel(x2t_ref, x2_ref, rb_ref, na_ref, ea_ref, eb_ref, u_ref, vt_ref, ln3g_ref, ln3b_ref,
                         y_ref, acc_s, h0_s, h1_s, w0_s, w1_s):
    j = pl.program_id(1)
    nsub = x2t_ref.shape[0] * x2t_ref.shape[2] // LANES
    pieces = max(nsub // SUBS_PER_PIECE, 1)
    piece_subs = nsub // pieces
    d_rows = D_MODEL // pieces
    e_rows = EXP_BLK // pieces

    @pl.when(j == 0)
    def _():
        acc_s[...] = jnp.zeros(acc_s.shape, F32)
        h1_s[...] = jnp.zeros(h1_s.shape, F32)
        w0_s[...] = jnp.zeros(w0_s.shape, jnp.int32)

    def key_row(ref, h, sub, ia):
        word = jnp.broadcast_to(ref[h, sub, pl.ds(ia, 1), :], (8, LANES))
        return jnp.tile(pltpu.bitcast(word, BF16), (N_KEYS // 16, 1))

    def gate_piece(h_s, w_s, blk, sub):
        blk = jnp.clip(blk, 0, N_EXPERTS // EXP_BLK - 1)
        for g in range(EXP_GROUPS):
            ia = blk * EXP_GROUPS + g
            gate = jnp.zeros((N_KEYS, LANES), BF16)
            for h in range(PEER_HEADS):
                eb = _unpacked(eb_ref[h, sub])
                keep = _unpacked(rb_ref[h, sub]) < key_row(na_ref, h, sub, ia)
                gate = gate + jnp.where(keep, eb, jnp.zeros_like(eb)) * key_row(ea_ref, h, sub, ia)
            act = _gelu_tanh(h_s[sub, g * N_KEYS:(g + 1) * N_KEYS, :]).astype(BF16) * gate
            w_s[sub, g * (N_KEYS // 2):(g + 1) * (N_KEYS // 2), :] = pltpu.bitcast(act, jnp.int32)

    def half_step(w_done, h_done, w_next, h_next, half, blk):
        experts = slice(half * EXP_BLK, (half + 1) * EXP_BLK)

        def piece(i, carry):
            u_rows = pl.ds(pl.multiple_of((half * EXP_BLK + i * e_rows) // 2, e_rows // 2), e_rows // 2)
            h_new = _dot(_unpacked(u_ref[u_rows, :]), _x2t_block(x2t_ref))
            for s in range(nsub):
                h_next[s, pl.ds(pl.multiple_of(i * e_rows, e_rows), e_rows), :] = h_new[:, s * LANES:(s + 1) * LANES]
            rows = pl.ds(pl.multiple_of(i * d_rows, d_rows), d_rows)
            w = jnp.concatenate([_unpacked(w_done[s]) for s in range(nsub)], axis=1)
            v_rows = pl.ds(pl.multiple_of(i * (d_rows // 2), d_rows // 2), d_rows // 2)
            acc_s[rows, :] += _dot(_unpacked(vt_ref[v_rows, experts]), w)
            for k in range(piece_subs):
                gate_piece(h_done, w_next, blk, i * piece_subs + k)
            return carry

        lax.fori_loop(0, pieces, piece, 0)

    half_step(w0_s, h1_s, w1_s, h0_s, 0, 2 * j - 1)
    half_step(w1_s, h0_s, w0_s, h1_s, 1, 2 * j)

    @pl.when(j == pl.num_programs(1) - 1)
    def _():
        y_ref[...] = _layer_norm(ALPHA * x2_ref[...] + acc_s[...].T, ln3g_ref[...], ln3b_ref[...])


def _peer_experts(x2t, x2, key_arrs, u_bf, vt_bf, ln3g, ln3b, *, tt):
    n = x2.shape[0]
    nsub = tt // LANES
    return pl.pallas_call(
        _peer_experts_kernel,
        grid=(n // tt, EXP_STEPS + 1),
        in_specs=[_x2t_spec(x2t, tt, lambda i, j: (i, 0, 0)),
                  pl.BlockSpec((tt, D_MODEL), lambda i, j: (i, 0)),
                  *[_key_spec(nsub, rows, lambda i, j: (0, i, 0, 0)) for rows in KEY_TABLE_ROWS],
                  pl.BlockSpec((EXP_BLK, D_MODEL), lambda i, j: (jnp.minimum(j, EXP_STEPS - 1), 0)),
                  pl.BlockSpec((None, D_MODEL // 2, 2 * EXP_BLK), lambda i, j: (jnp.maximum(j - 1, 0), 0, 0)),
                  _full(ln3g.shape), _full(ln3b.shape)],
        out_specs=pl.BlockSpec((tt, D_MODEL), lambda i, j: (i, 0)),
        out_shape=jax.ShapeDtypeStruct((n, D_MODEL), F32),
        scratch_shapes=[pltpu.VMEM((D_MODEL, tt), F32),
                        pltpu.VMEM((nsub, EXP_BLK, LANES), F32), pltpu.VMEM((nsub, EXP_BLK, LANES), F32),
                        pltpu.VMEM((nsub, EXP_BLK // 2, LANES), jnp.int32),
                        pltpu.VMEM((nsub, EXP_BLK // 2, LANES), jnp.int32)],
        compiler_params=_params("parallel", "arbitrary"),
        name="peer_experts",
    )(x2t, x2, *key_arrs, u_bf, vt_bf, ln3g, ln3b)


def _peer(x2t, x2, peer_wts, *, tt):
    wpqt, keys, flat, u_bf, vt_bf, ln3g, ln3b = peer_wts
    key_arrs = _peer_topk(x2t, wpqt, keys, flat, tt=tt)
    return _peer_experts(x2t, x2, key_arrs, u_bf, vt_bf, ln3g, ln3b, tt=tt)


def _rope_tables(pos):
    inv = ROPE_THETA ** (-jnp.arange(0, QK_ROPE, 2, dtype=F32) / QK_ROPE)
    ang = pos.astype(F32)[:, None] * inv[None, :]
    cos, sin, zero = jnp.cos(ang), jnp.sin(ang), jnp.zeros_like(ang)
    pad = jnp.zeros((pos.shape[0], LANES - QK_ROPE), F32)
    return (jnp.concatenate([cos, cos, pad], axis=1),
            jnp.concatenate([-sin, zero, pad], axis=1),
            jnp.concatenate([zero, sin, pad], axis=1))


def _row(v):
    return v.reshape(1, -1).astype(F32)


def _pack_experts_kernel(u_ref, v_ref, up_ref, vtp_ref):
    up_ref[...] = _packed(u_ref[...])
    vtp_ref[...] = _packed(v_ref[...].T)


def _pack_experts(peer_u, peer_v):
    blk = 2 * EXP_BLK
    return pl.pallas_call(
        _pack_experts_kernel,
        grid=(EXP_STEPS,),
        in_specs=[pl.BlockSpec((blk, D_MODEL), lambda i: (i, 0))] * 2,
        out_specs=[pl.BlockSpec((blk // 2, D_MODEL), lambda i: (i, 0)),
                   pl.BlockSpec((None, D_MODEL // 2, blk), lambda i: (i, 0, 0))],
        out_shape=[jax.ShapeDtypeStruct((N_EXPERTS // 2, D_MODEL), jnp.int32),
                   jax.ShapeDtypeStruct((EXP_STEPS, D_MODEL // 2, blk), jnp.int32)],
        compiler_params=_params("parallel"),
        name="pack_experts",
    )(peer_u, peer_v)


def kernel(x_prompt, x_sample, mem_prompt, cache_ckv, cache_krope, cache_mem_k, cache_mem_v, page_table,
           w_in, q_norm_g, kv_norm_g, w_uq, w_uk, w_uv, gm_norm_g, gm_norm_b, gm_ws, gm_bs, attn_out_g,
           gm_out_g, w_out, ln1_g, ln1_b, w_xq, w_mk, w_mv, w_xo, ln2_g, ln2_b, w_pq, peer_keys, peer_u,
           peer_v, ln3_g, ln3_b):
    batch, seq = x_prompt.shape[:2]
    nb = x_sample.shape[0]
    past_len = page_table.shape[1] * PAGE_SIZE

    kr_pad = jnp.zeros((D_MODEL, LANES - QK_ROPE), F32)
    w_in_x = jnp.concatenate([w_in[:, :Q_RANK + KV_RANK + QK_ROPE], kr_pad,
                              w_in[:, Q_RANK + KV_RANK + QK_ROPE:]], axis=1).astype(BF16)
    uq_nope = w_uq[:, :, :QK_NOPE].reshape(Q_RANK, MLA_HEADS * QK_NOPE)
    uq_rope = jnp.pad(w_uq[:, :, QK_NOPE:], ((0, 0), (0, 0), (0, LANES - QK_ROPE)))
    w_uq_x = jnp.concatenate([uq_nope, uq_rope.reshape(Q_RANK, MLA_HEADS * LANES)], axis=1).astype(BF16)
    eye = jnp.eye(MLA_HEADS, dtype=F32)
    w_uk_bd = jnp.einsum('rhd,hg->hdgr', w_uk, eye).reshape(MLA_HEADS * QK_NOPE, MLA_HEADS * KV_RANK).astype(BF16)
    w_uv_bd = jnp.einsum('rhd,hg->hrgd', w_uv, eye).reshape(MLA_HEADS * KV_RANK, MLA_WIDTH).astype(BF16)
    w_uv_t = jnp.transpose(w_uv, (1, 2, 0)).astype(BF16)
    proj_wts = (w_in_x, _row(q_norm_g), _row(kv_norm_g), w_uq_x, w_uk_bd, _row(gm_norm_g), _row(gm_norm_b))
    g_attn_col = jnp.broadcast_to(attn_out_g.astype(F32)[:, None], (MLA_WIDTH, Q_BLK))
    bias_tile = jnp.repeat(gm_bs.T, GM_WIDTH // GM_GROUPS, axis=1).astype(F32)
    w_out_bf, w_xq_bf = w_out.astype(BF16), w_xq.reshape(D_MODEL, X_WIDTH).astype(BF16)
    w_xo_bf = w_xo.reshape(X_WIDTH, D_MODEL).astype(BF16)
    post_tail = (w_xo_bf, _row(ln2_g), _row(ln2_b))
    post_wts = (gm_ws.astype(F32), bias_tile, _row(gm_out_g), w_out_bf, _row(ln1_g), _row(ln1_b), w_xq_bf) + post_tail
    ws0 = jnp.repeat(gm_ws[:, 0, 0], GM_WIDTH // GM_GROUPS)
    bs0 = jnp.repeat(gm_bs[:, 0], GM_WIDTH // GM_GROUPS)
    sample_wts = (w_uv_bd, _row(attn_out_g), _row(ws0), _row(bs0), _row(gm_out_g), w_out_bf, _row(ln1_g),
                  _row(ln1_b), w_xq_bf) + post_tail
    flat = np.full((CAND_ROWS,), 1e8, np.float32)
    for ka in range(PEER_TOPK):
        flat[CAND_OFF[ka]:CAND_OFF[ka] + CAND_NB[ka]] = ka * PEER_TOPK + np.arange(CAND_NB[ka])
    flat = jnp.asarray(np.broadcast_to(flat[:, None], (CAND_ROWS, LANES)))
    peer_wts = (w_pq.reshape(D_MODEL, PEER_HEADS * PEER_DK).T.astype(BF16),
                peer_keys.reshape(2 * PEER_HEADS, N_KEYS, PEER_HALF).astype(BF16), flat,
                *_pack_experts(peer_u, peer_v), _row(ln3_g), _row(ln3_b))

    n_p = batch * seq
    xp = x_prompt.reshape(n_p, D_MODEL)
    tm = 512
    qlat, qrope, kcat, ckv_p, krope_p, u_p, v_p, ckvt = _proj(
        xp, _rope_tables(jnp.arange(seq)), proj_wts, tm=tm, seq_blocks=seq // tm, emit_kt=True)
    a_p = _attn_prompt(qlat, qrope, kcat, ckvt, w_uv_t, g_attn_col, batch=batch, seq=seq)
    mk_p, mv_p = _memkv(mem_prompt.reshape(batch * MEM_TOKENS, D_MODEL),
                        w_mk.reshape(D_MODEL, X_WIDTH).astype(BF16), w_mv.reshape(D_MODEL, X_WIDTH).astype(BF16))
    x2_p, x2t_p = _post_prompt(xp, a_p, u_p, v_p, mk_p, mv_p, post_wts, batch=batch, seq=seq)
    y_p = _peer(x2t_p, x2_p, peer_wts, tt=512)

    xs = x_sample.reshape(nb, D_MODEL)
    pos_s = jnp.full((nb,), past_len, jnp.int32)
    qlat_s, qrope_s, _, ckv_s, krope_s, u_s, v_s = _proj(xs, _rope_tables(pos_s), proj_wts, tm=nb, seq_blocks=1,
                                                          emit_kt=False)
    q_cat = jnp.concatenate([qlat_s.reshape(nb, MLA_HEADS, KV_RANK), qrope_s.reshape(nb, MLA_HEADS, LANES)], axis=2)
    q_pad = jnp.pad(q_cat, ((0, 0), (0, LANES - MLA_HEADS), (0, 0)))
    o_lat_s = _attn_decode(page_table, q_pad, cache_ckv, cache_krope, ckv_s, krope_s)
    x2_s = _post_sample(xs, o_lat_s.reshape(nb, MLA_HEADS * KV_RANK), u_s, v_s,
                        cache_mem_k.reshape(nb, MEM_TOKENS, X_WIDTH), cache_mem_v.reshape(nb, MEM_TOKENS, X_WIDTH),
                        sample_wts)
    y_s = _peer(x2_s.T.astype(BF16).reshape(1, D_MODEL, nb), x2_s, peer_wts, tt=nb)

    return (y_p.reshape(batch, seq, D_MODEL), y_s.reshape(nb, 1, D_MODEL),
            ckv_p.reshape(batch, seq, KV_RANK), krope_p.reshape(batch, seq, QK_ROPE),
            mk_p.reshape(batch, MEM_TOKENS, X_HEADS, X_HEAD_DIM), mv_p.reshape(batch, MEM_TOKENS, X_HEADS, X_HEAD_DIM),
            ckv_s.reshape(nb, 1, KV_RANK), krope_s.reshape(nb, 1, QK_ROPE), v_s.reshape(nb, 1, GM_WIDTH))
```

```python
import functools

import jax
import jax.numpy as jnp
import numpy as np
from jax import lax
from jax.experimental import pallas as pl
from jax.experimental.pallas import tpu as pltpu

F32 = jnp.float32
BF16 = jnp.bfloat16

D_MODEL = 1024
MLA_HEADS = 8
QK_NOPE = 64
QK_ROPE = 32
V_HEAD = 64
Q_RANK = 256
KV_RANK = 128
MLA_WIDTH = MLA_HEADS * V_HEAD
MLA_SCALE = (QK_NOPE + QK_ROPE) ** -0.5
Q_SCALE = float(MLA_SCALE * np.log2(np.e))
ROPE_THETA = 10000.0
GM_WIDTH = D_MODEL // 2
GM_GROUPS = 4
GM_CHUNK = 128
MEM_TOKENS = 256
X_HEADS = 4
X_HEAD_DIM = 128
X_WIDTH = X_HEADS * X_HEAD_DIM
X_SCALE = X_HEAD_DIM ** -0.5
PEER_HEADS = 8
N_KEYS = 128
N_EXPERTS = N_KEYS * N_KEYS
PEER_TOPK = 16
PEER_DK = 256
PEER_HALF = PEER_DK // 2
PAGE_SIZE = 128
DEPTH = 1
ALPHA = (2.0 * DEPTH) ** 0.25
EPS = 1e-5

LANES = 128
VMEM_LIMIT = 56 * 1024 * 1024

CAND_NB = tuple(PEER_TOPK // (ka + 1) for ka in range(PEER_TOPK))
CAND_OFF = tuple(int(sum(CAND_NB[:ka])) for ka in range(PEER_TOPK))
CAND_N = int(sum(CAND_NB))
CAND_ROWS = 56
LEVEL2_CHAINS = 4
NEG_INF = float("-inf")


def _dot(a, b):
    return jnp.dot(a, b, preferred_element_type=F32)


def _dot_nt(a, b):
    return lax.dot_general(a, b, (((1,), (1,)), ((), ())), preferred_element_type=F32)


def _dot_tn(a, b):
    return lax.dot_general(a, b, (((0,), (0,)), ((), ())), preferred_element_type=F32)


def _layer_norm(x, g, b):
    mu = jnp.mean(x, -1, keepdims=True)
    var = jnp.mean(jnp.square(x - mu), -1, keepdims=True)
    return (x - mu) * lax.rsqrt(var + EPS) * g + b


def _rms_norm(x, g):
    return x * lax.rsqrt(jnp.mean(jnp.square(x), -1, keepdims=True) + EPS) * g


_GELU_K1 = float(-2.0 * np.sqrt(2.0 / np.pi) * np.log2(np.e))
_GELU_K2 = float(0.044715 * _GELU_K1)


def _gelu_tanh(x):
    return x / (1.0 + jnp.exp2(x * (_GELU_K1 + _GELU_K2 * (x * x))))


def _params(*sem):
    return pltpu.CompilerParams(dimension_semantics=sem, vmem_limit_bytes=VMEM_LIMIT)


def _full(shape):
    n = len(shape)
    return pl.BlockSpec(shape, lambda *_: (0,) * n)


def _rope(x, c, s_lo, s_hi):
    width = x.shape[-1]
    return x * c + pltpu.roll(x, width - 16, 1) * s_lo + pltpu.roll(x, 16, 1) * s_hi


def _proj_kernel(x_ref, c_ref, slo_ref, shi_ref, w_in_ref, qg_ref, kvg_ref, w_uq_ref, w_uk_ref,
                 gmg_ref, gmb_ref, qlat_ref, qrope_ref, kcat_ref, ckv_ref, krope_ref, u_ref, v_ref,
                 *maybe_ckvt_ref):
    h = _dot(x_ref[...].astype(BF16), w_in_ref[...])
    c, s_lo, s_hi = c_ref[...], slo_ref[...], shi_ref[...]

    cq = _rms_norm(h[:, :Q_RANK], qg_ref[...])
    q_all = _dot(cq.astype(BF16), w_uq_ref[...])
    q_nope = q_all[:, :MLA_HEADS * QK_NOPE]
    q_lat = _dot(q_nope.astype(BF16), w_uk_ref[...])
    qlat_ref[...] = (q_lat * Q_SCALE).astype(BF16)
    q_rope = _rope(q_all[:, MLA_HEADS * QK_NOPE:], jnp.tile(c, (1, MLA_HEADS)),
                   jnp.tile(s_lo, (1, MLA_HEADS)), jnp.tile(s_hi, (1, MLA_HEADS)))
    qrope_ref[...] = (q_rope * Q_SCALE).astype(BF16)

    ckv = _rms_norm(h[:, Q_RANK:Q_RANK + KV_RANK], kvg_ref[...])
    ckv_ref[...] = ckv
    k_rot = _rope(h[:, Q_RANK + KV_RANK:Q_RANK + KV_RANK + LANES], c, s_lo, s_hi)
    krope_ref[...] = k_rot[:, :QK_ROPE]
    kcat_ref[...] = jnp.concatenate([ckv, k_rot], axis=1).astype(BF16)
    for ckvt_ref in maybe_ckvt_ref:
        for j in range(ckvt_ref.shape[0]):
            ckvt_ref[j] = ckv[j * 256:(j + 1) * 256, :].T.astype(BF16)

    uv = jax.nn.gelu(h[:, Q_RANK + KV_RANK + LANES:])
    u_ref[...] = uv[:, :GM_WIDTH]
    v_ref[...] = _layer_norm(uv[:, GM_WIDTH:], gmg_ref[...], gmb_ref[...])


def _proj(x, tables, wts, *, tm, seq_blocks, emit_kt):
    n = x.shape[0]
    row = lambda w: pl.BlockSpec((tm, w), lambda i: (i, 0))
    tab = pl.BlockSpec((tm, LANES), lambda i: (i % seq_blocks, 0))
    w_in, qg, kvg, w_uq, w_uk, gmg, gmb = wts
    out_shape = [
        jax.ShapeDtypeStruct((n, MLA_HEADS * KV_RANK), BF16),
        jax.ShapeDtypeStruct((n, MLA_HEADS * LANES), BF16),
        jax.ShapeDtypeStruct((n, 2 * LANES), BF16),
        jax.ShapeDtypeStruct((n, KV_RANK), F32),
        jax.ShapeDtypeStruct((n, QK_ROPE), F32),
        jax.ShapeDtypeStruct((n, GM_WIDTH), F32),
        jax.ShapeDtypeStruct((n, GM_WIDTH), F32),
    ]
    out_specs = [row(MLA_HEADS * KV_RANK), row(MLA_HEADS * LANES), row(2 * LANES), row(KV_RANK),
                 row(QK_ROPE), row(GM_WIDTH), row(GM_WIDTH)]
    if emit_kt:
        out_shape.append(jax.ShapeDtypeStruct((n // 256, KV_RANK, 256), BF16))
        out_specs.append(pl.BlockSpec((tm // 256, KV_RANK, 256), lambda i: (i, 0, 0)))
    return pl.pallas_call(
        _proj_kernel,
        grid=(n // tm,),
        in_specs=[row(D_MODEL), tab, tab, tab, _full(w_in.shape), _full(qg.shape), _full(kvg.shape),
                  _full(w_uq.shape), _full(w_uk.shape), _full(gmg.shape), _full(gmb.shape)],
        out_specs=out_specs,
        out_shape=out_shape,
        compiler_params=_params("parallel"),
        name="proj",
    )(x, *tables, w_in, qg, kvg, w_uq, w_uk, gmg, gmb)


Q_BLK = 256
KV_BLK = 256


ATTN_COLS = MLA_HEADS * Q_BLK


def _attn_kernel(qlat_ref, qrope_ref, kcat_ref, ckvt_ref, wuvt_ref, g_ref, a_ref, q_s, m_s, l_s, acc_s):
    qi = pl.program_id(1)
    ql, qr = qlat_ref[...], qrope_ref[...]
    hpc = ATTN_COLS // Q_BLK
    for h in range(MLA_HEADS):
        q_s[h // hpc, (h % hpc) * Q_BLK:(h % hpc + 1) * Q_BLK, :] = jnp.concatenate(
            [ql[:, h * LANES:(h + 1) * LANES], qr[:, h * LANES:(h + 1) * LANES]], axis=1)
    m_s[...] = jnp.full(m_s.shape, NEG_INF, F32)
    l_s[...] = jnp.zeros(l_s.shape, F32)
    acc_s[...] = jnp.zeros(acc_s.shape, F32)
    q_pos = qi * Q_BLK + (lax.broadcasted_iota(jnp.int32, (KV_BLK, ATTN_COLS), 1) & (Q_BLK - 1))
    k_off = lax.broadcasted_iota(jnp.int32, (KV_BLK, ATTN_COLS), 0)

    def step(j, diagonal):
        k = kcat_ref[pl.ds(pl.multiple_of(j * KV_BLK, KV_BLK), KV_BLK), :]
        v_t = ckvt_ref[j]
        for c in range(MLA_HEADS // hpc):
            st = _dot_nt(k, q_s[c])
            if diagonal:
                st = jnp.where(q_pos >= k_off + j * KV_BLK, st, NEG_INF)
            m_old = m_s[c]
            m_new = jnp.maximum(m_old, jnp.max(st, axis=0, keepdims=True))
            alpha = jnp.exp2(m_old - m_new)
            p = jnp.exp2(st - m_new)
            l_s[c] = alpha * l_s[c] + jnp.sum(p, axis=0, keepdims=True)
            acc_s[c] = alpha * acc_s[c] + _dot(v_t, p.astype(BF16))
            m_s[c] = m_new

    last = (qi * Q_BLK) // KV_BLK

    def visible(j, carry):
        step(j, False)
        return carry

    lax.fori_loop(0, last, visible, 0)
    step(last, True)

    o_t = (acc_s[...] / l_s[...]).astype(BF16)
    om_t = jnp.concatenate(
        [_dot(wuvt_ref[h], o_t[h // hpc, :, (h % hpc) * Q_BLK:(h % hpc + 1) * Q_BLK])
         for h in range(MLA_HEADS)], axis=0)
    ms = jnp.mean(jnp.square(om_t), axis=0, keepdims=True)
    a_t = om_t * lax.rsqrt(ms + EPS) * g_ref[...]
    a_ref[...] = a_t.T.astype(BF16)


def _attn_prompt(qlat, qrope, kcat, ckvt, wuvt, g_attn, *, batch, seq):
    n = batch * seq
    nq = seq // Q_BLK
    chains = MLA_HEADS * Q_BLK // ATTN_COLS
    return pl.pallas_call(
        _attn_kernel,
        grid=(batch, nq),
        in_specs=[
            pl.BlockSpec((Q_BLK, MLA_HEADS * KV_RANK), lambda b, i: (b * nq + i, 0)),
            pl.BlockSpec((Q_BLK, MLA_HEADS * LANES), lambda b, i: (b * nq + i, 0)),
            pl.BlockSpec((None, seq, 2 * LANES), lambda b, i: (b, 0, 0)),
            pl.BlockSpec((None, seq // KV_BLK, KV_RANK, KV_BLK), lambda b, i: (b, 0, 0, 0)),
            _full(wuvt.shape), _full(g_attn.shape),
        ],
        out_specs=pl.BlockSpec((Q_BLK, MLA_WIDTH), lambda b, i: (b * nq + i, 0)),
        out_shape=jax.ShapeDtypeStruct((n, MLA_WIDTH), BF16),
        scratch_shapes=[pltpu.VMEM((chains, ATTN_COLS, 2 * LANES), BF16), pltpu.VMEM((chains, 1, ATTN_COLS), F32),
                        pltpu.VMEM((chains, 1, ATTN_COLS), F32), pltpu.VMEM((chains, KV_RANK, ATTN_COLS), F32)],
        compiler_params=_params("parallel", "arbitrary"),
        name="attn_prompt",
    )(qlat, qrope, kcat.reshape(batch, seq, 2 * LANES),
      ckvt.reshape(batch, seq // KV_BLK, KV_RANK, KV_BLK), wuvt, g_attn)


PAGES_PER_STEP = 8


SEQS_PER_STEP = 4


def _decode_kernel(pt_ref, q_ref, *refs):
    del pt_ref
    n_pg = SEQS_PER_STEP * PAGES_PER_STEP
    ckv_refs, kr_refs = refs[:n_pg], refs[n_pg:2 * n_pg]
    ckvn_ref, krn_ref, o_ref, m_s, l_s, acc_s = refs[2 * n_pg:]
    c = pl.program_id(1)

    @pl.when(c == 0)
    def _():
        m_s[...] = jnp.full(m_s.shape, NEG_INF, F32)
        l_s[...] = jnp.zeros(l_s.shape, F32)
        acc_s[...] = jnp.zeros(acc_s.shape, F32)

    def keys(ckv, kr_wide):
        return jnp.concatenate([ckv, kr_wide], axis=1).astype(BF16)

    def rotary_page(kr_t):
        return jnp.concatenate([kr_t, jnp.zeros((LANES - QK_ROPE, PAGE_SIZE), F32)], axis=0).T

    def update(b, kcat, valid_rows):
        st = _dot_nt(kcat, q_ref[b])
        if valid_rows is not None:
            st = jnp.where(lax.broadcasted_iota(jnp.int32, st.shape, 0) < valid_rows, st, NEG_INF)
        m_old = m_s[b]
        m_new = jnp.maximum(m_old, jnp.max(st, axis=0, keepdims=True))
        alpha = jnp.exp2(m_old - m_new)
        p = jnp.exp2(st - m_new)
        l_s[b] = alpha * l_s[b] + jnp.sum(p, axis=0, keepdims=True)
        acc_s[b] = alpha * acc_s[b] + _dot_tn(kcat[:, :KV_RANK], p.astype(BF16))
        m_s[b] = m_new

    for b in range(SEQS_PER_STEP):
        pages = slice(b * PAGES_PER_STEP, (b + 1) * PAGES_PER_STEP)
        update(b, keys(jnp.concatenate([r[...] for r in ckv_refs[pages]], axis=0),
                       jnp.concatenate([rotary_page(r[...]) for r in kr_refs[pages]], axis=0)), None)

    @pl.when(c == pl.num_programs(1) - 1)
    def _():
        for b in range(SEQS_PER_STEP):
            krn = jnp.concatenate([krn_ref[b], jnp.zeros((1, LANES - QK_ROPE), F32)], axis=1)
            update(b, keys(jnp.broadcast_to(ckvn_ref[b], (16, KV_RANK)), jnp.broadcast_to(krn, (16, LANES))), 1)
            o_t = acc_s[b] / l_s[b]
            o_ref[b] = o_t.T[:MLA_HEADS, :]


def _attn_decode(page_table, q_pad, cache_ckv, cache_krope, ckv_new, krope_new):
    nb, n_pages = page_table.shape
    steps = n_pages // PAGES_PER_STEP
    sq = SEQS_PER_STEP

    def page_spec(rows, width, b, i):
        return pl.BlockSpec((None, rows, width),
                            lambda g, c, pt: (pt[g * sq + b, c * PAGES_PER_STEP + i], 0, 0))

    def seq_spec(*dims):
        return pl.BlockSpec((sq,) + dims, lambda g, c, pt: (g,) + (0,) * len(dims))

    in_specs = [seq_spec(LANES, 2 * LANES)]
    in_specs += [page_spec(PAGE_SIZE, KV_RANK, b, i) for b in range(sq) for i in range(PAGES_PER_STEP)]
    in_specs += [page_spec(QK_ROPE, PAGE_SIZE, b, i) for b in range(sq) for i in range(PAGES_PER_STEP)]
    in_specs += [seq_spec(1, KV_RANK), seq_spec(1, QK_ROPE)]
    n_pg = sq * PAGES_PER_STEP
    return pl.pallas_call(
        _decode_kernel,
        grid_spec=pltpu.PrefetchScalarGridSpec(
            num_scalar_prefetch=1,
            grid=(nb // sq, steps),
            in_specs=in_specs,
            out_specs=seq_spec(MLA_HEADS, KV_RANK),
            scratch_shapes=[pltpu.VMEM((sq, 1, LANES), F32), pltpu.VMEM((sq, 1, LANES), F32),
                            pltpu.VMEM((sq, KV_RANK, LANES), F32)],
        ),
        out_shape=jax.ShapeDtypeStruct((nb, MLA_HEADS, KV_RANK), F32),
        compiler_params=_params("parallel", "arbitrary"),
        name="attn_decode",
    )(page_table, q_pad, *([cache_ckv] * n_pg), *([jnp.swapaxes(cache_krope, 1, 2)] * n_pg),
      ckv_new.reshape(nb, 1, KV_RANK), krope_new.reshape(nb, 1, QK_ROPE))


def _memkv_kernel(mem_ref, wk_ref, wv_ref, mk_ref, mv_ref):
    m = mem_ref[...].astype(BF16)
    mk_ref[...] = _dot(m, wk_ref[...])
    mv_ref[...] = _dot(m, wv_ref[...])


def _memkv(mem, w_mk, w_mv):
    n = mem.shape[0]
    tm = 512
    return pl.pallas_call(
        _memkv_kernel,
        grid=(n // tm,),
        in_specs=[pl.BlockSpec((tm, D_MODEL), lambda i: (i, 0)), _full(w_mk.shape), _full(w_mv.shape)],
        out_specs=[pl.BlockSpec((tm, X_WIDTH), lambda i: (i, 0))] * 2,
        out_shape=[jax.ShapeDtypeStruct((n, X_WIDTH), F32)] * 2,
        compiler_params=_params("parallel"),
        name="memkv",
    )(mem, w_mk, w_mv)


POST_ROWS = 256


def _softmax_rows(s):
    e = jnp.exp(s - jnp.max(s, axis=-1, keepdims=True))
    return e / jnp.sum(e, axis=-1, keepdims=True)


def _mix_and_ln1(x, a_bf, o_gm, gmog, w_out, ln1g, ln1b):
    gm_n = _rms_norm(o_gm, gmog)
    y = jnp.concatenate([a_bf, gm_n.astype(BF16)], axis=1)
    return _layer_norm(ALPHA * x + _dot(y, w_out), ln1g, ln1b)


def _post_prompt_kernel(x_ref, a_ref, u_ref, v_ref, ws_ref, bias_ref, gmog_ref, wout_ref, ln1g_ref, ln1b_ref,
                        wxq_ref, mk_ref, mv_ref, wxo_ref, ln2g_ref, ln2b_ref, x2_ref, x2t_ref):
    tril = (lax.broadcasted_iota(jnp.int32, (GM_CHUNK, GM_CHUNK), 0)
            >= lax.broadcasted_iota(jnp.int32, (GM_CHUNK, GM_CHUNK), 1))
    w_s = [jnp.where(tril, ws_ref[g], 0.0).astype(BF16) for g in range(GM_GROUPS)]
    chunks = []
    for c in range(POST_ROWS // GM_CHUNK):
        rows = slice(c * GM_CHUNK, (c + 1) * GM_CHUNK)
        v_c = v_ref[rows, :].astype(BF16)
        s = jnp.concatenate([_dot(w_s[g], v_c[:, g * LANES:(g + 1) * LANES]) for g in range(GM_GROUPS)],
                            axis=1) + bias_ref[...]
        chunks.append(u_ref[rows, :] * s)
    o_gm = jnp.concatenate(chunks, axis=0)
    x1 = _mix_and_ln1(x_ref[...], a_ref[...], o_gm, gmog_ref[...], wout_ref[...], ln1g_ref[...], ln1b_ref[...])

    q = _dot(x1.astype(BF16), wxq_ref[...]).astype(BF16)
    mk, mv = mk_ref[...].astype(BF16), mv_ref[...].astype(BF16)
    heads = []
    for h in range(X_HEADS):
        cs = slice(h * X_HEAD_DIM, (h + 1) * X_HEAD_DIM)
        p = _softmax_rows(_dot_nt(q[:, cs], mk[:, cs]) * X_SCALE)
        heads.append(_dot(p.astype(BF16), mv[:, cs]))
    o = jnp.concatenate(heads, axis=1).astype(BF16)
    x2 = _layer_norm(ALPHA * x1 + _dot(o, wxo_ref[...]), ln2g_ref[...], ln2b_ref[...])
    x2_ref[...] = x2
    x2t_ref[...] = x2.T.astype(BF16)


def _post_prompt(x, a, u, v, mk, mv, wts, *, batch, seq):
    n = batch * seq
    nb = seq // POST_ROWS
    row = lambda w: pl.BlockSpec((POST_ROWS, w), lambda b, i: (b * nb + i, 0))
    mem = pl.BlockSpec((MEM_TOKENS, X_WIDTH), lambda b, i: (b, 0))
    return pl.pallas_call(
        _post_prompt_kernel,
        grid=(batch, nb),
        in_specs=[row(D_MODEL), row(MLA_WIDTH), row(GM_WIDTH), row(GM_WIDTH)]
                 + [_full(w.shape) for w in wts[:6]] + [_full(wts[6].shape), mem, mem]
                 + [_full(w.shape) for w in wts[7:]],
        out_specs=[row(D_MODEL), pl.BlockSpec((None, D_MODEL, POST_ROWS), lambda b, i: (b * nb + i, 0, 0))],
        out_shape=[jax.ShapeDtypeStruct((n, D_MODEL), F32),
                   jax.ShapeDtypeStruct((n // POST_ROWS, D_MODEL, POST_ROWS), BF16)],
        compiler_params=_params("parallel", "parallel"),
        name="post_prompt",
    )(x, a, u, v, *wts[:7], mk, mv, *wts[7:])


SAMPLE_ROWS = 8


def _post_sample_kernel(x_ref, o_ref, u_ref, v_ref, wuv_ref, ag_ref, ws0_ref, bs0_ref, gmog_ref, wout_ref,
                        ln1g_ref, ln1b_ref, wxq_ref, mk_ref, mv_ref, wxo_ref, ln2g_ref, ln2b_ref, x2_ref):
    o_mla = _dot(o_ref[...].astype(BF16), wuv_ref[...])
    a = _rms_norm(o_mla, ag_ref[...]).astype(BF16)
    o_gm = u_ref[...] * (ws0_ref[...] * v_ref[...] + bs0_ref[...])
    x1 = _mix_and_ln1(x_ref[...], a, o_gm, gmog_ref[...], wout_ref[...], ln1g_ref[...], ln1b_ref[...])

    q = _dot(x1.astype(BF16), wxq_ref[...])
    lane_head = lax.broadcasted_iota(jnp.int32, (LANES, X_WIDTH), 1) // X_HEAD_DIM
    on_head = lane_head == lax.broadcasted_iota(jnp.int32, (LANES, X_WIDTH), 0)
    rows = []
    for j in range(SAMPLE_ROWS):
        q_bd = jnp.where(on_head, q[j:j + 1, :], 0.0).astype(BF16)
        s = _dot_nt(mk_ref[j].astype(BF16), q_bd) * X_SCALE
        e = jnp.exp(s - jnp.max(s, axis=0, keepdims=True))
        p = e / jnp.sum(e, axis=0, keepdims=True)
        o_all = _dot_tn(p.astype(BF16), mv_ref[j].astype(BF16))
        rows.append(jnp.sum(jnp.where(on_head, o_all, 0.0), axis=0, keepdims=True))
    o = jnp.concatenate(rows, axis=0).astype(BF16)
    x2_ref[...] = _layer_norm(ALPHA * x1 + _dot(o, wxo_ref[...]), ln2g_ref[...], ln2b_ref[...])


def _post_sample(x, o_lat, u, v, mk, mv, wts):
    n = x.shape[0]
    row = lambda w: pl.BlockSpec((SAMPLE_ROWS, w), lambda i: (i, 0))
    mem = pl.BlockSpec((SAMPLE_ROWS, MEM_TOKENS, X_WIDTH), lambda i: (i, 0, 0))
    return pl.pallas_call(
        _post_sample_kernel,
        grid=(n // SAMPLE_ROWS,),
        in_specs=[row(D_MODEL), row(MLA_HEADS * KV_RANK), row(GM_WIDTH), row(GM_WIDTH)]
                 + [_full(w.shape) for w in wts[:9]] + [mem, mem] + [_full(w.shape) for w in wts[9:]],
        out_specs=row(D_MODEL),
        out_shape=jax.ShapeDtypeStruct((n, D_MODEL), F32),
        compiler_params=_params("parallel"),
        name="post_sample",
    )(x, o_lat, u, v, *wts[:9], mk, mv, *wts[9:])


def _top16(val, row_id, *, break_ties):
    rank = jnp.full(val.shape, 127.0, F32)
    tops = []
    for k in range(PEER_TOPK):
        m = jnp.max(val, axis=0, keepdims=True)
        hit = val == m
        if break_ties:
            hit = row_id == jnp.min(jnp.where(hit, row_id, 1e9), axis=0, keepdims=True)
        val = jnp.where(hit, NEG_INF, val)
        rank = jnp.where(hit, float(k), rank)
        tops.append(m)
    return tops, rank


def _tied(rank):
    marked = jnp.sum(jnp.where(rank < float(PEER_TOPK), 1.0, 0.0), axis=0, keepdims=True)
    return marked - float(PEER_TOPK)


def _packed(x):
    return pltpu.bitcast(x.astype(BF16), jnp.int32)


def _unpacked(w):
    return pltpu.bitcast(w, BF16)


def _bf16_pair(x):
    u = pltpu.bitcast(x, jnp.int32)
    hi = lax.shift_right_logical(u + 0x7FFF + (lax.shift_right_logical(u, 16) & 1), 16)
    return hi | lax.shift_left(hi, 16)


def _peer_topk_kernel(x2t_ref, wpqt_ref, keys_ref, flat_ref, rb_ref, na_ref, ea_ref, eb_ref,
                      qt_s, s_s, rank_s, top_s, cand_s, sel_s):
    nsub = x2t_ref.shape[0] * x2t_ref.shape[2] // LANES
    qt = _dot(wpqt_ref[...], _x2t_block(x2t_ref)).astype(BF16)
    for sub in range(nsub):
        qt_s[sub] = qt[:, sub * LANES:(sub + 1) * LANES]
    key_id = lax.broadcasted_iota(jnp.int32, (N_KEYS, LANES), 0).astype(F32)
    flat = flat_ref[...]

    def keep_level1(hc, sub, tops, rank):
        rank_s[hc, sub] = rank
        for k in range(PEER_TOPK):
            top_s[hc, sub, k:k + 1, :] = tops[k]

    def level2(h, sub, u):
        sa, sb = top_s[2 * h, sub], top_s[2 * h + 1, sub]
        ea_r = jnp.exp(sa - sa[0:1, :])
        eb_r = jnp.exp(sb - sb[0:1, :])
        for ka in range(PEER_TOPK):
            cand_s[u, CAND_OFF[ka]:CAND_OFF[ka] + CAND_NB[ka], :] = sa[ka:ka + 1, :] + sb[0:CAND_NB[ka], :]
        cand_s[u, CAND_N:CAND_ROWS, :] = jnp.full((CAND_ROWS - CAND_N, LANES), NEG_INF, F32)
        _, crank = _top16(cand_s[u], flat, break_ties=True)
        sel_s[u] = jnp.where(crank < float(PEER_TOPK), 1.0, 0.0)
        n_a, z = [], jnp.zeros((1, LANES), F32)
        for ka in range(PEER_TOPK):
            sel_ka = sel_s[u, CAND_OFF[ka]:CAND_OFF[ka] + CAND_NB[ka], :]
            n_a.append(jnp.sum(sel_ka, axis=0, keepdims=True))
            z = z + ea_r[ka:ka + 1, :] * jnp.sum(sel_ka * eb_r[0:CAND_NB[ka], :], axis=0, keepdims=True)
        rank_a = rank_s[2 * h, sub]
        na = jnp.zeros((N_KEYS, LANES), F32)
        for ka in range(PEER_TOPK):
            na = jnp.where(rank_a == float(ka), n_a[ka], na)
        na_ref[h, sub] = _bf16_pair(na)
        rb_ref[h, sub] = _packed(rank_s[2 * h + 1, sub])
        ea_ref[h, sub] = _bf16_pair(jnp.exp(s_s[2 * h, sub] - sa[0:1, :]))
        eb_ref[h, sub] = _packed(jnp.exp(s_s[2 * h + 1, sub] - sb[0:1, :]) / z)

    def per_subtile(sub, carry):
        def quick(h, tied):
            for hc in (2 * h, 2 * h + 1):
                q_blk = qt_s[sub, pl.ds(pl.multiple_of(hc * PEER_HALF, PEER_HALF), PEER_HALF), :]
                s = _dot(keys_ref[hc], q_blk)
                s_s[hc, sub] = s
                tops, rank = _top16(s, key_id, break_ties=False)
                keep_level1(hc, sub, tops, rank)
                tied = jnp.maximum(tied, _tied(rank))
            return tied

        tied = lax.fori_loop(0, PEER_HEADS, quick, jnp.zeros((1, LANES), F32))

        @pl.when(jnp.max(tied) > 0.0)
        def _():
            def careful(hc, c):
                keep_level1(hc, sub, *_top16(s_s[hc, sub], key_id, break_ties=True))
                return c
            lax.fori_loop(0, 2 * PEER_HEADS, careful, 0)

        def heads(q, c):
            for u in range(LEVEL2_CHAINS):
                level2(q * LEVEL2_CHAINS + u, sub, u)
            return c

        lax.fori_loop(0, PEER_HEADS // LEVEL2_CHAINS, heads, 0)
        return carry

    lax.fori_loop(0, nsub, per_subtile, 0)


KEY_TABLE_ROWS = (N_KEYS // 2, N_KEYS, N_KEYS, N_KEYS // 2)


def _key_spec(nsub, rows, index_map):
    return pl.BlockSpec((PEER_HEADS, nsub, rows, LANES), index_map)


def _x2t_spec(x2t, tt, index_map):
    return pl.BlockSpec((tt // x2t.shape[2], D_MODEL, x2t.shape[2]), index_map)


def _x2t_block(x2t_ref):
    return jnp.concatenate([x2t_ref[t] for t in range(x2t_ref.shape[0])], axis=1)


def _peer_topk(x2t, wpqt, keys, flat, *, tt):
    n = x2t.shape[0] * x2t.shape[2]
    nsub = tt // LANES
    hc = 2 * PEER_HEADS
    return pl.pallas_call(
        _peer_topk_kernel,
        grid=(n // tt,),
        in_specs=[_x2t_spec(x2t, tt, lambda i: (i, 0, 0)), _full(wpqt.shape), _full(keys.shape),
                  _full(flat.shape)],
        out_specs=[_key_spec(nsub, rows, lambda i: (0, i, 0, 0)) for rows in KEY_TABLE_ROWS],
        out_shape=[jax.ShapeDtypeStruct((PEER_HEADS, n // LANES, rows, LANES), jnp.int32)
                   for rows in KEY_TABLE_ROWS],
        scratch_shapes=[pltpu.VMEM((nsub, PEER_HEADS * PEER_DK, LANES), BF16),
                        pltpu.VMEM((hc, nsub, N_KEYS, LANES), F32),
                        pltpu.VMEM((hc, nsub, N_KEYS, LANES), F32),
                        pltpu.VMEM((hc, nsub, PEER_TOPK, LANES), F32),
                        pltpu.VMEM((LEVEL2_CHAINS, CAND_ROWS, LANES), F32),
                        pltpu.VMEM((LEVEL2_CHAINS, CAND_ROWS, LANES), F32)],
        compiler_params=_params("parallel"),
        name="peer_topk",
    )(x2t, wpqt, keys, flat)


EXP_BLK = 1024
EXP_GROUPS = EXP_BLK // N_KEYS
SUBS_PER_PIECE = 2
EXP_STEPS = N_EXPERTS // (2 * EXP_BLK)


def _peer_experts_kernel(x2t_ref, x2_ref, rb_ref, na_ref, ea_ref, eb_ref, u_ref, vt_ref, ln3g_ref, ln3b_ref,
                         y_ref, acc_s, h0_s, h1_s, w0_s, w1_s):
    j = pl.program_id(1)
    nsub = x2t_ref.shape[0] * x2t_ref.shape[2] // LANES
    pieces = max(nsub // SUBS_PER_PIECE, 1)
    piece_subs = nsub // pieces
    d_rows = D_MODEL // pieces
    e_rows = EXP_BLK // pieces

    def key_row(ref, h, sub, ia):
        word = jnp.broadcast_to(ref[h, sub, pl.ds(ia, 1), :], (8, LANES))
        return jnp.tile(pltpu.bitcast(word, BF16), (N_KEYS // 16, 1))

    def gate_piece(h_s, w_s, blk, sub):
        for g in range(EXP_GROUPS):
            ia = blk * EXP_GROUPS + g
            gate = jnp.zeros((N_KEYS, LANES), BF16)
            for h in range(PEER_HEADS):
                eb = _unpacked(eb_ref[h, sub])
                keep = _unpacked(rb_ref[h, sub]) < key_row(na_ref, h, sub, ia)
                gate = gate + jnp.where(keep, eb, jnp.zeros_like(eb)) * key_row(ea_ref, h, sub, ia)
            act = _gelu_tanh(h_s[sub, g * N_KEYS:(g + 1) * N_KEYS, :]).astype(BF16) * gate
            w_s[sub, g * (N_KEYS // 2):(g + 1) * (N_KEYS // 2), :] = pltpu.bitcast(act, jnp.int32)

    def half_step(w_done, h_done, w_next, h_next, half, blk, *, down=True, gate=True, up=True):
        experts = slice(half * EXP_BLK, (half + 1) * EXP_BLK)

        def piece(i, carry):
            if up:
                u_rows = pl.ds(pl.multiple_of((half * EXP_BLK + i * e_rows) // 2, e_rows // 2), e_rows // 2)
                h_new = _dot(_unpacked(u_ref[u_rows, :]), _x2t_block(x2t_ref))
                for s in range(nsub):
                    h_next[s, pl.ds(pl.multiple_of(i * e_rows, e_rows), e_rows), :] = (
                        h_new[:, s * LANES:(s + 1) * LANES])
            if down:
                rows = pl.ds(pl.multiple_of(i * d_rows, d_rows), d_rows)
                w = jnp.concatenate([_unpacked(w_done[s]) for s in range(nsub)], axis=1)
                v_rows = pl.ds(pl.multiple_of(i * (d_rows // 2), d_rows // 2), d_rows // 2)
                acc_s[rows, :] += _dot(_unpacked(vt_ref[v_rows, experts]), w)
            if gate:
                for k in range(piece_subs):
                    gate_piece(h_done, w_next, blk, i * piece_subs + k)
            return carry

        lax.fori_loop(0, pieces, piece, 0)

    last = pl.num_programs(1) - 1

    @pl.when(j == 0)
    def _():
        acc_s[...] = jnp.zeros(acc_s.shape, F32)
        half_step(w0_s, h1_s, w1_s, h0_s, 0, 2 * j - 1, down=False, gate=False)
        half_step(w1_s, h0_s, w0_s, h1_s, 1, 2 * j, down=False)

    @pl.when((j > 0) & (j < last))
    def _():
        half_step(w0_s, h1_s, w1_s, h0_s, 0, 2 * j - 1)
        half_step(w1_s, h0_s, w0_s, h1_s, 1, 2 * j)

    @pl.when(j == last)
    def _():
        half_step(w0_s, h1_s, w1_s, h0_s, 0, 2 * j - 1, up=False)
        half_step(w1_s, h0_s, w0_s, h1_s, 1, 2 * j, gate=False, up=False)
        y_ref[...] = _layer_norm(ALPHA * x2_ref[...] + acc_s[...].T, ln3g_ref[...], ln3b_ref[...])


def _peer_experts(x2t, x2, key_arrs, u_bf, vt_bf, ln3g, ln3b, *, tt):
    n = x2.shape[0]
    nsub = tt // LANES
    return pl.pallas_call(
        _peer_experts_kernel,
        grid=(n // tt, EXP_STEPS + 1),
        in_specs=[_x2t_spec(x2t, tt, lambda i, j: (i, 0, 0)),
                  pl.BlockSpec((tt, D_MODEL), lambda i, j: (i, 0)),
                  *[_key_spec(nsub, rows, lambda i, j: (0, i, 0, 0)) for rows in KEY_TABLE_ROWS],
                  pl.BlockSpec((EXP_BLK, D_MODEL), lambda i, j: (jnp.minimum(j, EXP_STEPS - 1), 0)),
                  pl.BlockSpec((None, D_MODEL // 2, 2 * EXP_BLK), lambda i, j: (jnp.maximum(j - 1, 0), 0, 0)),
                  _full(ln3g.shape), _full(ln3b.shape)],
        out_specs=pl.BlockSpec((tt, D_MODEL), lambda i, j: (i, 0)),
        out_shape=jax.ShapeDtypeStruct((n, D_MODEL), F32),
        scratch_shapes=[pltpu.VMEM((D_MODEL, tt), F32),
                        pltpu.VMEM((nsub, EXP_BLK, LANES), F32), pltpu.VMEM((nsub, EXP_BLK, LANES), F32),
                        pltpu.VMEM((nsub, EXP_BLK // 2, LANES), jnp.int32),
                        pltpu.VMEM((nsub, EXP_BLK // 2, LANES), jnp.int32)],
        compiler_params=_params("parallel", "arbitrary"),
        name="peer_experts",
    )(x2t, x2, *key_arrs, u_bf, vt_bf, ln3g, ln3b)


def _peer(x2t, x2, peer_wts, *, tt):
    wpqt, keys, flat, u_bf, vt_bf, ln3g, ln3b = peer_wts
    key_arrs = _peer_topk(x2t, wpqt, keys, flat, tt=tt)
    return _peer_experts(x2t, x2, key_arrs, u_bf, vt_bf, ln3g, ln3b, tt=tt)


def _rope_tables(pos):
    inv = ROPE_THETA ** (-jnp.arange(0, QK_ROPE, 2, dtype=F32) / QK_ROPE)
    ang = pos.astype(F32)[:, None] * inv[None, :]
    cos, sin, zero = jnp.cos(ang), jnp.sin(ang), jnp.zeros_like(ang)
    pad = jnp.zeros((pos.shape[0], LANES - QK_ROPE), F32)
    return (jnp.concatenate([cos, cos, pad], axis=1),
            jnp.concatenate([-sin, zero, pad], axis=1),
            jnp.concatenate([zero, sin, pad], axis=1))


def _row(v):
    return v.reshape(1, -1).astype(F32)


def _pack_experts_kernel(u_ref, v_ref, up_ref, vtp_ref):
    up_ref[...] = _packed(u_ref[...])
    vtp_ref[...] = _packed(v_ref[...].T)


def _pack_experts(peer_u, peer_v):
    blk = 2 * EXP_BLK
    return pl.pallas_call(
        _pack_experts_kernel,
        grid=(EXP_STEPS,),
        in_specs=[pl.BlockSpec((blk, D_MODEL), lambda i: (i, 0))] * 2,
        out_specs=[pl.BlockSpec((blk // 2, D_MODEL), lambda i: (i, 0)),
                   pl.BlockSpec((None, D_MODEL // 2, blk), lambda i: (i, 0, 0))],
        out_shape=[jax.ShapeDtypeStruct((N_EXPERTS // 2, D_MODEL), jnp.int32),
                   jax.ShapeDtypeStruct((EXP_STEPS, D_MODEL // 2, blk), jnp.int32)],
        compiler_params=_params("parallel"),
        name="pack_experts",
    )(peer_u, peer_v)


def kernel(x_prompt, x_sample, mem_prompt, cache_ckv, cache_krope, cache_mem_k, cache_mem_v, page_table,
           w_in, q_norm_g, kv_norm_g, w_uq, w_uk, w_uv, gm_norm_g, gm_norm_b, gm_ws, gm_bs, attn_out_g,
           gm_out_g, w_out, ln1_g, ln1_b, w_xq, w_mk, w_mv, w_xo, ln2_g, ln2_b, w_pq, peer_keys, peer_u,
           peer_v, ln3_g, ln3_b):
    batch, seq = x_prompt.shape[:2]
    nb = x_sample.shape[0]
    past_len = page_table.shape[1] * PAGE_SIZE

    kr_pad = jnp.zeros((D_MODEL, LANES - QK_ROPE), F32)
    w_in_x = jnp.concatenate([w_in[:, :Q_RANK + KV_RANK + QK_ROPE], kr_pad,
                              w_in[:, Q_RANK + KV_RANK + QK_ROPE:]], axis=1).astype(BF16)
    uq_nope = w_uq[:, :, :QK_NOPE].reshape(Q_RANK, MLA_HEADS * QK_NOPE)
    uq_rope = jnp.pad(w_uq[:, :, QK_NOPE:], ((0, 0), (0, 0), (0, LANES - QK_ROPE)))
    w_uq_x = jnp.concatenate([uq_nope, uq_rope.reshape(Q_RANK, MLA_HEADS * LANES)], axis=1).astype(BF16)
    eye = jnp.eye(MLA_HEADS, dtype=F32)
    w_uk_bd = jnp.einsum('rhd,hg->hdgr', w_uk, eye).reshape(MLA_HEADS * QK_NOPE, MLA_HEADS * KV_RANK).astype(BF16)
    w_uv_bd = jnp.einsum('rhd,hg->hrgd', w_uv, eye).reshape(MLA_HEADS * KV_RANK, MLA_WIDTH).astype(BF16)
    w_uv_t = jnp.transpose(w_uv, (1, 2, 0)).astype(BF16)
    proj_wts = (w_in_x, _row(q_norm_g), _row(kv_norm_g), w_uq_x, w_uk_bd, _row(gm_norm_g), _row(gm_norm_b))
    g_attn_col = jnp.broadcast_to(attn_out_g.astype(F32)[:, None], (MLA_WIDTH, Q_BLK))
    bias_tile = jnp.repeat(gm_bs.T, GM_WIDTH // GM_GROUPS, axis=1).astype(F32)
    w_out_bf, w_xq_bf = w_out.astype(BF16), w_xq.reshape(D_MODEL, X_WIDTH).astype(BF16)
    w_xo_bf = w_xo.reshape(X_WIDTH, D_MODEL).astype(BF16)
    post_tail = (w_xo_bf, _row(ln2_g), _row(ln2_b))
    post_wts = (gm_ws.astype(F32), bias_tile, _row(gm_out_g), w_out_bf, _row(ln1_g), _row(ln1_b), w_xq_bf) + post_tail
    ws0 = jnp.repeat(gm_ws[:, 0, 0], GM_WIDTH // GM_GROUPS)
    bs0 = jnp.repeat(gm_bs[:, 0], GM_WIDTH // GM_GROUPS)
    sample_wts = (w_uv_bd, _row(attn_out_g), _row(ws0), _row(bs0), _row(gm_out_g), w_out_bf, _row(ln1_g),
                  _row(ln1_b), w_xq_bf) + post_tail
    flat = np.full((CAND_ROWS,), 1e8, np.float32)
    for ka in range(PEER_TOPK):
        flat[CAND_OFF[ka]:CAND_OFF[ka] + CAND_NB[ka]] = ka * PEER_TOPK + np.arange(CAND_NB[ka])
    flat = jnp.asarray(np.broadcast_to(flat[:, None], (CAND_ROWS, LANES)))
    peer_wts = (w_pq.reshape(D_MODEL, PEER_HEADS * PEER_DK).T.astype(BF16),
                peer_keys.reshape(2 * PEER_HEADS, N_KEYS, PEER_HALF).astype(BF16), flat,
                *_pack_experts(peer_u, peer_v), _row(ln3_g), _row(ln3_b))

    n_p = batch * seq
    xp = x_prompt.reshape(n_p, D_MODEL)
    tm = 512
    qlat, qrope, kcat, ckv_p, krope_p, u_p, v_p, ckvt = _proj(
        xp, _rope_tables(jnp.arange(seq)), proj_wts, tm=tm, seq_blocks=seq // tm, emit_kt=True)
    a_p = _attn_prompt(qlat, qrope, kcat, ckvt, w_uv_t, g_attn_col, batch=batch, seq=seq)
    mk_p, mv_p = _memkv(mem_prompt.reshape(batch * MEM_TOKENS, D_MODEL),
                        w_mk.reshape(D_MODEL, X_WIDTH).astype(BF16), w_mv.reshape(D_MODEL, X_WIDTH).astype(BF16))
    x2_p, x2t_p = _post_prompt(xp, a_p, u_p, v_p, mk_p, mv_p, post_wts, batch=batch, seq=seq)
    y_p = _peer(x2t_p, x2_p, peer_wts, tt=512)

    xs = x_sample.reshape(nb, D_MODEL)
    pos_s = jnp.full((nb,), past_len, jnp.int32)
    qlat_s, qrope_s, _, ckv_s, krope_s, u_s, v_s = _proj(xs, _rope_tables(pos_s), proj_wts, tm=nb, seq_blocks=1,
                                                          emit_kt=False)
    q_cat = jnp.concatenate([qlat_s.reshape(nb, MLA_HEADS, KV_RANK), qrope_s.reshape(nb, MLA_HEADS, LANES)], axis=2)
    q_pad = jnp.pad(q_cat, ((0, 0), (0, LANES - MLA_HEADS), (0, 0)))
    o_lat_s = _attn_decode(page_table, q_pad, cache_ckv, cache_krope, ckv_s, krope_s)
    x2_s = _post_sample(xs, o_lat_s.reshape(nb, MLA_HEADS * KV_RANK), u_s, v_s,
                        cache_mem_k.reshape(nb, MEM_TOKENS, X_WIDTH), cache_mem_v.reshape(nb, MEM_TOKENS, X_WIDTH),
                        sample_wts)
    y_s = _peer(x2_s.T.astype(BF16).reshape(1, D_MODEL, nb), x2_s, peer_wts, tt=nb)

    return (y_p.reshape(batch, seq, D_MODEL), y_s.reshape(nb, 1, D_MODEL),
            ckv_p.reshape(batch, seq, KV_RANK), krope_p.reshape(batch, seq, QK_ROPE),
            mk_p.reshape(batch, MEM_TOKENS, X_HEADS, X_HEAD_DIM), mv_p.reshape(batch, MEM_TOKENS, X_HEADS, X_HEAD_DIM),
            ckv_s.reshape(nb, 1, KV_RANK), krope_s.reshape(nb, 1, QK_ROPE), v_s.reshape(nb, 1, GM_WIDTH))
```

```python
import functools

import jax
import jax.numpy as jnp
import numpy as np
from jax import lax
from jax.experimental import pallas as pl
from jax.experimental.pallas import tpu as pltpu

F32 = jnp.float32
BF16 = jnp.bfloat16

D_MODEL = 1024
MLA_HEADS = 8
QK_NOPE = 64
QK_ROPE = 32
V_HEAD = 64
Q_RANK = 256
KV_RANK = 128
MLA_WIDTH = MLA_HEADS * V_HEAD
MLA_SCALE = (QK_NOPE + QK_ROPE) ** -0.5
Q_SCALE = float(MLA_SCALE * np.log2(np.e))
ROPE_THETA = 10000.0
GM_WIDTH = D_MODEL // 2
GM_GROUPS = 4
GM_CHUNK = 128
MEM_TOKENS = 256
X_HEADS = 4
X_HEAD_DIM = 128
X_WIDTH = X_HEADS * X_HEAD_DIM
X_SCALE = X_HEAD_DIM ** -0.5
PEER_HEADS = 8
N_KEYS = 128
N_EXPERTS = N_KEYS * N_KEYS
PEER_TOPK = 16
PEER_DK = 256
PEER_HALF = PEER_DK // 2
PAGE_SIZE = 128
DEPTH = 1
ALPHA = (2.0 * DEPTH) ** 0.25
EPS = 1e-5

LANES = 128
VMEM_LIMIT = 56 * 1024 * 1024

CAND_NB = tuple(PEER_TOPK // (ka + 1) for ka in range(PEER_TOPK))
CAND_OFF = tuple(int(sum(CAND_NB[:ka])) for ka in range(PEER_TOPK))
CAND_N = int(sum(CAND_NB))
CAND_ROWS = 56
LEVEL1_CHAINS = 16
LEVEL2_CHAINS = 8
NEG_INF = float("-inf")


def _dot(a, b):
    return jnp.dot(a, b, preferred_element_type=F32)


def _dot_nt(a, b):
    return lax.dot_general(a, b, (((1,), (1,)), ((), ())), preferred_element_type=F32)


def _dot_tn(a, b):
    return lax.dot_general(a, b, (((0,), (0,)), ((), ())), preferred_element_type=F32)


def _layer_norm(x, g, b):
    mu = jnp.mean(x, -1, keepdims=True)
    var = jnp.mean(jnp.square(x - mu), -1, keepdims=True)
    return (x - mu) * lax.rsqrt(var + EPS) * g + b


def _rms_norm(x, g):
    return x * lax.rsqrt(jnp.mean(jnp.square(x), -1, keepdims=True) + EPS) * g


_GELU_K1 = float(-2.0 * np.sqrt(2.0 / np.pi) * np.log2(np.e))
_GELU_K2 = float(0.044715 * _GELU_K1)


def _gelu_tanh(x):
    return x / (1.0 + jnp.exp2(x * (_GELU_K1 + _GELU_K2 * (x * x))))


def _params(*sem):
    return pltpu.CompilerParams(dimension_semantics=sem, vmem_limit_bytes=VMEM_LIMIT)


def _full(shape):
    n = len(shape)
    return pl.BlockSpec(shape, lambda *_: (0,) * n)


def _rope(x, c, s_lo, s_hi):
    width = x.shape[-1]
    return x * c + pltpu.roll(x, width - 16, 1) * s_lo + pltpu.roll(x, 16, 1) * s_hi


def _proj_kernel(x_ref, c_ref, slo_ref, shi_ref, w_in_ref, qg_ref, kvg_ref, w_uq_ref, w_uk_ref,
                 gmg_ref, gmb_ref, qlat_ref, qrope_ref, kcat_ref, ckv_ref, krope_ref, u_ref, v_ref,
                 *maybe_ckvt_ref):
    h = _dot(x_ref[...].astype(BF16), w_in_ref[...])
    c, s_lo, s_hi = c_ref[...], slo_ref[...], shi_ref[...]

    cq = _rms_norm(h[:, :Q_RANK], qg_ref[...])
    q_all = _dot(cq.astype(BF16), w_uq_ref[...])
    q_nope = q_all[:, :MLA_HEADS * QK_NOPE]
    q_lat = _dot(q_nope.astype(BF16), w_uk_ref[...])
    qlat_ref[...] = (q_lat * Q_SCALE).astype(BF16)
    q_rope = _rope(q_all[:, MLA_HEADS * QK_NOPE:], jnp.tile(c, (1, MLA_HEADS)),
                   jnp.tile(s_lo, (1, MLA_HEADS)), jnp.tile(s_hi, (1, MLA_HEADS)))
    qrope_ref[...] = (q_rope * Q_SCALE).astype(BF16)

    ckv = _rms_norm(h[:, Q_RANK:Q_RANK + KV_RANK], kvg_ref[...])
    ckv_ref[...] = ckv
    k_rot = _rope(h[:, Q_RANK + KV_RANK:Q_RANK + KV_RANK + LANES], c, s_lo, s_hi)
    krope_ref[...] = k_rot[:, :QK_ROPE]
    kcat_ref[...] = jnp.concatenate([ckv, k_rot], axis=1).astype(BF16)
    for ckvt_ref in maybe_ckvt_ref:
        for j in range(ckvt_ref.shape[0]):
            ckvt_ref[j] = ckv[j * 256:(j + 1) * 256, :].T.astype(BF16)

    uv = jax.nn.gelu(h[:, Q_RANK + KV_RANK + LANES:])
    u_ref[...] = uv[:, :GM_WIDTH]
    v_ref[...] = _layer_norm(uv[:, GM_WIDTH:], gmg_ref[...], gmb_ref[...])


def _proj(x, tables, wts, *, tm, seq_blocks, emit_kt):
    n = x.shape[0]
    row = lambda w: pl.BlockSpec((tm, w), lambda i: (i, 0))
    tab = pl.BlockSpec((tm, LANES), lambda i: (i % seq_blocks, 0))
    w_in, qg, kvg, w_uq, w_uk, gmg, gmb = wts
    out_shape = [
        jax.ShapeDtypeStruct((n, MLA_HEADS * KV_RANK), BF16),
        jax.ShapeDtypeStruct((n, MLA_HEADS * LANES), BF16),
        jax.ShapeDtypeStruct((n, 2 * LANES), BF16),
        jax.ShapeDtypeStruct((n, KV_RANK), F32),
        jax.ShapeDtypeStruct((n, QK_ROPE), F32),
        jax.ShapeDtypeStruct((n, GM_WIDTH), F32),
        jax.ShapeDtypeStruct((n, GM_WIDTH), F32),
    ]
    out_specs = [row(MLA_HEADS * KV_RANK), row(MLA_HEADS * LANES), row(2 * LANES), row(KV_RANK),
                 row(QK_ROPE), row(GM_WIDTH), row(GM_WIDTH)]
    if emit_kt:
        out_shape.append(jax.ShapeDtypeStruct((n // 256, KV_RANK, 256), BF16))
        out_specs.append(pl.BlockSpec((tm // 256, KV_RANK, 256), lambda i: (i, 0, 0)))
    return pl.pallas_call(
        _proj_kernel,
        grid=(n // tm,),
        in_specs=[row(D_MODEL), tab, tab, tab, _full(w_in.shape), _full(qg.shape), _full(kvg.shape),
                  _full(w_uq.shape), _full(w_uk.shape), _full(gmg.shape), _full(gmb.shape)],
        out_specs=out_specs,
        out_shape=out_shape,
        compiler_params=_params("parallel"),
        name="proj",
    )(x, *tables, w_in, qg, kvg, w_uq, w_uk, gmg, gmb)


Q_BLK = 256
KV_BLK = 256


ATTN_COLS = MLA_HEADS * Q_BLK


def _attn_kernel(qlat_ref, qrope_ref, kcat_ref, ckvt_ref, wuvt_ref, g_ref, a_ref, q_s, m_s, l_s, acc_s):
    qi = pl.program_id(1)
    ql, qr = qlat_ref[...], qrope_ref[...]
    hpc = ATTN_COLS // Q_BLK
    for h in range(MLA_HEADS):
        q_s[h // hpc, (h % hpc) * Q_BLK:(h % hpc + 1) * Q_BLK, :] = jnp.concatenate(
            [ql[:, h * LANES:(h + 1) * LANES], qr[:, h * LANES:(h + 1) * LANES]], axis=1)
    m_s[...] = jnp.full(m_s.shape, NEG_INF, F32)
    l_s[...] = jnp.zeros(l_s.shape, F32)
    acc_s[...] = jnp.zeros(acc_s.shape, F32)
    q_pos = qi * Q_BLK + (lax.broadcasted_iota(jnp.int32, (KV_BLK, ATTN_COLS), 1) & (Q_BLK - 1))
    k_off = lax.broadcasted_iota(jnp.int32, (KV_BLK, ATTN_COLS), 0)

    def step(j, diagonal):
        k = kcat_ref[pl.ds(pl.multiple_of(j * KV_BLK, KV_BLK), KV_BLK), :]
        v_t = ckvt_ref[j]
        for c in range(MLA_HEADS // hpc):
            st = _dot_nt(k, q_s[c])
            if diagonal:
                st = jnp.where(q_pos >= k_off + j * KV_BLK, st, NEG_INF)
            m_old = m_s[c]
            m_new = jnp.maximum(m_old, jnp.max(st, axis=0, keepdims=True))
            alpha = jnp.exp2(m_old - m_new)
            p = jnp.exp2(st - m_new)
            l_s[c] = alpha * l_s[c] + jnp.sum(p, axis=0, keepdims=True)
            acc_s[c] = alpha * acc_s[c] + _dot(v_t, p.astype(BF16))
            m_s[c] = m_new

    last = (qi * Q_BLK) // KV_BLK

    def visible(j, carry):
        step(j, False)
        return carry

    lax.fori_loop(0, last, visible, 0)
    step(last, True)

    o_t = (acc_s[...] / l_s[...]).astype(BF16)
    om_t = jnp.concatenate(
        [_dot(wuvt_ref[h], o_t[h // hpc, :, (h % hpc) * Q_BLK:(h % hpc + 1) * Q_BLK])
         for h in range(MLA_HEADS)], axis=0)
    ms = jnp.mean(jnp.square(om_t), axis=0, keepdims=True)
    a_t = om_t * lax.rsqrt(ms + EPS) * g_ref[...]
    a_ref[...] = a_t.T.astype(BF16)


def _attn_prompt(qlat, qrope, kcat, ckvt, wuvt, g_attn, *, batch, seq):
    n = batch * seq
    nq = seq // Q_BLK
    chains = MLA_HEADS * Q_BLK // ATTN_COLS
    return pl.pallas_call(
        _attn_kernel,
        grid=(batch, nq),
        in_specs=[
            pl.BlockSpec((Q_BLK, MLA_HEADS * KV_RANK), lambda b, i: (b * nq + i, 0)),
            pl.BlockSpec((Q_BLK, MLA_HEADS * LANES), lambda b, i: (b * nq + i, 0)),
            pl.BlockSpec((None, seq, 2 * LANES), lambda b, i: (b, 0, 0)),
            pl.BlockSpec((None, seq // KV_BLK, KV_RANK, KV_BLK), lambda b, i: (b, 0, 0, 0)),
            _full(wuvt.shape), _full(g_attn.shape),
        ],
        out_specs=pl.BlockSpec((Q_BLK, MLA_WIDTH), lambda b, i: (b * nq + i, 0)),
        out_shape=jax.ShapeDtypeStruct((n, MLA_WIDTH), BF16),
        scratch_shapes=[pltpu.VMEM((chains, ATTN_COLS, 2 * LANES), BF16), pltpu.VMEM((chains, 1, ATTN_COLS), F32),
                        pltpu.VMEM((chains, 1, ATTN_COLS), F32), pltpu.VMEM((chains, KV_RANK, ATTN_COLS), F32)],
        compiler_params=_params("parallel", "arbitrary"),
        name="attn_prompt",
    )(qlat, qrope, kcat.reshape(batch, seq, 2 * LANES),
      ckvt.reshape(batch, seq // KV_BLK, KV_RANK, KV_BLK), wuvt, g_attn)


PAGES_PER_STEP = 8


SEQS_PER_STEP = 4


def _decode_kernel(pt_ref, q_ref, *refs):
    del pt_ref
    n_pg = SEQS_PER_STEP * PAGES_PER_STEP
    ckv_refs, kr_refs = refs[:n_pg], refs[n_pg:2 * n_pg]
    ckvn_ref, krn_ref, o_ref, m_s, l_s, acc_s = refs[2 * n_pg:]
    c = pl.program_id(1)

    @pl.when(c == 0)
    def _():
        m_s[...] = jnp.full(m_s.shape, NEG_INF, F32)
        l_s[...] = jnp.zeros(l_s.shape, F32)
        acc_s[...] = jnp.zeros(acc_s.shape, F32)

    def keys(ckv, kr_wide):
        return jnp.concatenate([ckv, kr_wide], axis=1).astype(BF16)

    def rotary_page(kr_t):
        return jnp.concatenate([kr_t, jnp.zeros((LANES - QK_ROPE, PAGE_SIZE), F32)], axis=0).T

    def update(b, kcat, valid_rows):
        st = _dot_nt(kcat, q_ref[b])
        if valid_rows is not None:
            st = jnp.where(lax.broadcasted_iota(jnp.int32, st.shape, 0) < valid_rows, st, NEG_INF)
        m_old = m_s[b]
        m_new = jnp.maximum(m_old, jnp.max(st, axis=0, keepdims=True))
        alpha = jnp.exp2(m_old - m_new)
        p = jnp.exp2(st - m_new)
        l_s[b] = alpha * l_s[b] + jnp.sum(p, axis=0, keepdims=True)
        acc_s[b] = alpha * acc_s[b] + _dot_tn(kcat[:, :KV_RANK], p.astype(BF16))
        m_s[b] = m_new

    for b in range(SEQS_PER_STEP):
        pages = slice(b * PAGES_PER_STEP, (b + 1) * PAGES_PER_STEP)
        update(b, keys(jnp.concatenate([r[...] for r in ckv_refs[pages]], axis=0),
                       jnp.concatenate([rotary_page(r[...]) for r in kr_refs[pages]], axis=0)), None)

    @pl.when(c == pl.num_programs(1) - 1)
    def _():
        for b in range(SEQS_PER_STEP):
            krn = jnp.concatenate([krn_ref[b], jnp.zeros((1, LANES - QK_ROPE), F32)], axis=1)
            update(b, keys(jnp.broadcast_to(ckvn_ref[b], (16, KV_RANK)), jnp.broadcast_to(krn, (16, LANES))), 1)
            o_t = acc_s[b] / l_s[b]
            o_ref[b] = o_t.T[:MLA_HEADS, :]


def _attn_decode(page_table, q_pad, cache_ckv, cache_krope, ckv_new, krope_new):
    nb, n_pages = page_table.shape
    steps = n_pages // PAGES_PER_STEP
    sq = SEQS_PER_STEP

    def page_spec(rows, width, b, i):
        return pl.BlockSpec((None, rows, width),
                            lambda g, c, pt: (pt[g * sq + b, c * PAGES_PER_STEP + i], 0, 0))

    def seq_spec(*dims):
        return pl.BlockSpec((sq,) + dims, lambda g, c, pt: (g,) + (0,) * len(dims))

    in_specs = [seq_spec(LANES, 2 * LANES)]
    in_specs += [page_spec(PAGE_SIZE, KV_RANK, b, i) for b in range(sq) for i in range(PAGES_PER_STEP)]
    in_specs += [page_spec(QK_ROPE, PAGE_SIZE, b, i) for b in range(sq) for i in range(PAGES_PER_STEP)]
    in_specs += [seq_spec(1, KV_RANK), seq_spec(1, QK_ROPE)]
    n_pg = sq * PAGES_PER_STEP
    return pl.pallas_call(
        _decode_kernel,
        grid_spec=pltpu.PrefetchScalarGridSpec(
            num_scalar_prefetch=1,
            grid=(nb // sq, steps),
            in_specs=in_specs,
            out_specs=seq_spec(MLA_HEADS, KV_RANK),
            scratch_shapes=[pltpu.VMEM((sq, 1, LANES), F32), pltpu.VMEM((sq, 1, LANES), F32),
                            pltpu.VMEM((sq, KV_RANK, LANES), F32)],
        ),
        out_shape=jax.ShapeDtypeStruct((nb, MLA_HEADS, KV_RANK), F32),
        compiler_params=_params("parallel", "arbitrary"),
        name="attn_decode",
    )(page_table, q_pad, *([cache_ckv] * n_pg), *([jnp.swapaxes(cache_krope, 1, 2)] * n_pg),
      ckv_new.reshape(nb, 1, KV_RANK), krope_new.reshape(nb, 1, QK_ROPE))


def _memkv_kernel(mem_ref, wk_ref, wv_ref, mk_ref, mv_ref):
    m = mem_ref[...].astype(BF16)
    mk_ref[...] = _dot(m, wk_ref[...])
    mv_ref[...] = _dot(m, wv_ref[...])


def _memkv(mem, w_mk, w_mv):
    n = mem.shape[0]
    tm = 512
    return pl.pallas_call(
        _memkv_kernel,
        grid=(n // tm,),
        in_specs=[pl.BlockSpec((tm, D_MODEL), lambda i: (i, 0)), _full(w_mk.shape), _full(w_mv.shape)],
        out_specs=[pl.BlockSpec((tm, X_WIDTH), lambda i: (i, 0))] * 2,
        out_shape=[jax.ShapeDtypeStruct((n, X_WIDTH), F32)] * 2,
        compiler_params=_params("parallel"),
        name="memkv",
    )(mem, w_mk, w_mv)


POST_ROWS = 256


def _softmax_rows(s):
    e = jnp.exp(s - jnp.max(s, axis=-1, keepdims=True))
    return e / jnp.sum(e, axis=-1, keepdims=True)


def _mix_and_ln1(x, a_bf, o_gm, gmog, w_out, ln1g, ln1b):
    gm_n = _rms_norm(o_gm, gmog)
    y = jnp.concatenate([a_bf, gm_n.astype(BF16)], axis=1)
    return _layer_norm(ALPHA * x + _dot(y, w_out), ln1g, ln1b)


def _post_prompt_kernel(x_ref, a_ref, u_ref, v_ref, ws_ref, bias_ref, gmog_ref, wout_ref, ln1g_ref, ln1b_ref,
                        wxq_ref, mk_ref, mv_ref, wxo_ref, ln2g_ref, ln2b_ref, x2_ref, x2t_ref):
    tril = (lax.broadcasted_iota(jnp.int32, (GM_CHUNK, GM_CHUNK), 0)
            >= lax.broadcasted_iota(jnp.int32, (GM_CHUNK, GM_CHUNK), 1))
    w_s = [jnp.where(tril, ws_ref[g], 0.0).astype(BF16) for g in range(GM_GROUPS)]
    chunks = []
    for c in range(POST_ROWS // GM_CHUNK):
        rows = slice(c * GM_CHUNK, (c + 1) * GM_CHUNK)
        v_c = v_ref[rows, :].astype(BF16)
        s = jnp.concatenate([_dot(w_s[g], v_c[:, g * LANES:(g + 1) * LANES]) for g in range(GM_GROUPS)],
                            axis=1) + bias_ref[...]
        chunks.append(u_ref[rows, :] * s)
    o_gm = jnp.concatenate(chunks, axis=0)
    x1 = _mix_and_ln1(x_ref[...], a_ref[...], o_gm, gmog_ref[...], wout_ref[...], ln1g_ref[...], ln1b_ref[...])

    q = _dot(x1.astype(BF16), wxq_ref[...]).astype(BF16)
    mk, mv = mk_ref[...].astype(BF16), mv_ref[...].astype(BF16)
    heads = []
    for h in range(X_HEADS):
        cs = slice(h * X_HEAD_DIM, (h + 1) * X_HEAD_DIM)
        p = _softmax_rows(_dot_nt(q[:, cs], mk[:, cs]) * X_SCALE)
        heads.append(_dot(p.astype(BF16), mv[:, cs]))
    o = jnp.concatenate(heads, axis=1).astype(BF16)
    x2 = _layer_norm(ALPHA * x1 + _dot(o, wxo_ref[...]), ln2g_ref[...], ln2b_ref[...])
    x2_ref[...] = x2
    x2t_ref[...] = x2.T.astype(BF16)


def _post_prompt(x, a, u, v, mk, mv, wts, *, batch, seq):
    n = batch * seq
    nb = seq // POST_ROWS
    row = lambda w: pl.BlockSpec((POST_ROWS, w), lambda b, i: (b * nb + i, 0))
    mem = pl.BlockSpec((MEM_TOKENS, X_WIDTH), lambda b, i: (b, 0))
    return pl.pallas_call(
        _post_prompt_kernel,
        grid=(batch, nb),
        in_specs=[row(D_MODEL), row(MLA_WIDTH), row(GM_WIDTH), row(GM_WIDTH)]
                 + [_full(w.shape) for w in wts[:6]] + [_full(wts[6].shape), mem, mem]
                 + [_full(w.shape) for w in wts[7:]],
        out_specs=[row(D_MODEL), pl.BlockSpec((None, D_MODEL, POST_ROWS), lambda b, i: (b * nb + i, 0, 0))],
        out_shape=[jax.ShapeDtypeStruct((n, D_MODEL), F32),
                   jax.ShapeDtypeStruct((n // POST_ROWS, D_MODEL, POST_ROWS), BF16)],
        compiler_params=_params("parallel", "parallel"),
        name="post_prompt",
    )(x, a, u, v, *wts[:7], mk, mv, *wts[7:])


SAMPLE_ROWS = 8


def _post_sample_kernel(x_ref, o_ref, u_ref, v_ref, wuv_ref, ag_ref, ws0_ref, bs0_ref, gmog_ref, wout_ref,
                        ln1g_ref, ln1b_ref, wxq_ref, mk_ref, mv_ref, wxo_ref, ln2g_ref, ln2b_ref, x2_ref):
    o_mla = _dot(o_ref[...].astype(BF16), wuv_ref[...])
    a = _rms_norm(o_mla, ag_ref[...]).astype(BF16)
    o_gm = u_ref[...] * (ws0_ref[...] * v_ref[...] + bs0_ref[...])
    x1 = _mix_and_ln1(x_ref[...], a, o_gm, gmog_ref[...], wout_ref[...], ln1g_ref[...], ln1b_ref[...])

    q = _dot(x1.astype(BF16), wxq_ref[...])
    lane_head = lax.broadcasted_iota(jnp.int32, (LANES, X_WIDTH), 1) // X_HEAD_DIM
    on_head = lane_head == lax.broadcasted_iota(jnp.int32, (LANES, X_WIDTH), 0)
    rows = []
    for j in range(SAMPLE_ROWS):
        q_bd = jnp.where(on_head, q[j:j + 1, :], 0.0).astype(BF16)
        s = _dot_nt(mk_ref[j].astype(BF16), q_bd) * X_SCALE
        e = jnp.exp(s - jnp.max(s, axis=0, keepdims=True))
        p = e / jnp.sum(e, axis=0, keepdims=True)
        o_all = _dot_tn(p.astype(BF16), mv_ref[j].astype(BF16))
        rows.append(jnp.sum(jnp.where(on_head, o_all, 0.0), axis=0, keepdims=True))
    o = jnp.concatenate(rows, axis=0).astype(BF16)
    x2_ref[...] = _layer_norm(ALPHA * x1 + _dot(o, wxo_ref[...]), ln2g_ref[...], ln2b_ref[...])


def _post_sample(x, o_lat, u, v, mk, mv, wts):
    n = x.shape[0]
    row = lambda w: pl.BlockSpec((SAMPLE_ROWS, w), lambda i: (i, 0))
    mem = pl.BlockSpec((SAMPLE_ROWS, MEM_TOKENS, X_WIDTH), lambda i: (i, 0, 0))
    return pl.pallas_call(
        _post_sample_kernel,
        grid=(n // SAMPLE_ROWS,),
        in_specs=[row(D_MODEL), row(MLA_HEADS * KV_RANK), row(GM_WIDTH), row(GM_WIDTH)]
                 + [_full(w.shape) for w in wts[:9]] + [mem, mem] + [_full(w.shape) for w in wts[9:]],
        out_specs=row(D_MODEL),
        out_shape=jax.ShapeDtypeStruct((n, D_MODEL), F32),
        compiler_params=_params("parallel"),
        name="post_sample",
    )(x, o_lat, u, v, *wts[:9], mk, mv, *wts[9:])


def _top16(val, row_id, *, break_ties):
    rank = jnp.full(val.shape, 127.0, F32)
    tops = []
    for k in range(PEER_TOPK):
        m = jnp.max(val, axis=0, keepdims=True)
        hit = val == m
        if break_ties:
            hit = row_id == jnp.min(jnp.where(hit, row_id, 1e9), axis=0, keepdims=True)
        val = jnp.where(hit, NEG_INF, val)
        rank = jnp.where(hit, float(k), rank)
        tops.append(m)
    return tops, rank


def _tied(rank):
    marked = jnp.sum(jnp.where(rank < float(PEER_TOPK), 1.0, 0.0), axis=0, keepdims=True)
    return marked - float(PEER_TOPK)


def _packed(x):
    return pltpu.bitcast(x.astype(BF16), jnp.int32)


def _unpacked(w):
    return pltpu.bitcast(w, BF16)


def _bf16_pair(x):
    u = pltpu.bitcast(x, jnp.int32)
    hi = lax.shift_right_logical(u + 0x7FFF + (lax.shift_right_logical(u, 16) & 1), 16)
    return hi | lax.shift_left(hi, 16)


def _peer_topk_kernel(x2t_ref, wpqt_ref, keys_ref, flat_ref, rb_ref, na_ref, ea_ref, eb_ref,
                      qt_s, s_s, rank_s, top_s, cand_s, sel_s):
    nsub = x2t_ref.shape[0] * x2t_ref.shape[2] // LANES
    qt = _dot(wpqt_ref[...], _x2t_block(x2t_ref)).astype(BF16)
    for sub in range(nsub):
        qt_s[sub] = qt[:, sub * LANES:(sub + 1) * LANES]
    key_id = lax.broadcasted_iota(jnp.int32, (N_KEYS, LANES), 0).astype(F32)
    flat = flat_ref[...]

    def keep_level1(hc, sub, tops, rank):
        rank_s[hc, sub] = rank
        for k in range(PEER_TOPK):
            top_s[hc, sub, k:k + 1, :] = tops[k]

    def level2(h, sub, u):
        sa, sb = top_s[2 * h, sub], top_s[2 * h + 1, sub]
        ea_r = jnp.exp(sa - sa[0:1, :])
        eb_r = jnp.exp(sb - sb[0:1, :])
        for ka in range(PEER_TOPK):
            cand_s[u, CAND_OFF[ka]:CAND_OFF[ka] + CAND_NB[ka], :] = sa[ka:ka + 1, :] + sb[0:CAND_NB[ka], :]
        cand_s[u, CAND_N:CAND_ROWS, :] = jnp.full((CAND_ROWS - CAND_N, LANES), NEG_INF, F32)
        _, crank = _top16(cand_s[u], flat, break_ties=True)
        sel_s[u] = jnp.where(crank < float(PEER_TOPK), 1.0, 0.0)
        n_a, z = [], jnp.zeros((1, LANES), F32)
        for ka in range(PEER_TOPK):
            sel_ka = sel_s[u, CAND_OFF[ka]:CAND_OFF[ka] + CAND_NB[ka], :]
            n_a.append(jnp.sum(sel_ka, axis=0, keepdims=True))
            z = z + ea_r[ka:ka + 1, :] * jnp.sum(sel_ka * eb_r[0:CAND_NB[ka], :], axis=0, keepdims=True)
        rank_a = rank_s[2 * h, sub]
        na = jnp.zeros((N_KEYS, LANES), F32)
        for ka in range(PEER_TOPK):
            na = jnp.where(rank_a == float(ka), n_a[ka], na)
        na_ref[h, sub] = _bf16_pair(na)
        rb_ref[h, sub] = _packed(rank_s[2 * h + 1, sub])
        ea_ref[h, sub] = _bf16_pair(jnp.exp(s_s[2 * h, sub] - sa[0:1, :]))
        eb_ref[h, sub] = _packed(jnp.exp(s_s[2 * h + 1, sub] - sb[0:1, :]) / z)

    def per_subtile(sub, carry):
        def quick(h, tied):
            for hc in [LEVEL1_CHAINS * h + u for u in range(LEVEL1_CHAINS)]:
                q_blk = qt_s[sub, pl.ds(pl.multiple_of(hc * PEER_HALF, PEER_HALF), PEER_HALF), :]
                s = _dot(keys_ref[hc], q_blk)
                s_s[hc, sub] = s
                tops, rank = _top16(s, key_id, break_ties=False)
                keep_level1(hc, sub, tops, rank)
                tied = jnp.maximum(tied, _tied(rank))
            return tied

        tied = lax.fori_loop(0, 2 * PEER_HEADS // LEVEL1_CHAINS, quick, jnp.zeros((1, LANES), F32))

        @pl.when(jnp.max(tied) > 0.0)
        def _():
            def careful(hc, c):
                keep_level1(hc, sub, *_top16(s_s[hc, sub], key_id, break_ties=True))
                return c
            lax.fori_loop(0, 2 * PEER_HEADS, careful, 0)

        def heads(q, c):
            for u in range(LEVEL2_CHAINS):
                level2(q * LEVEL2_CHAINS + u, sub, u)
            return c

        lax.fori_loop(0, PEER_HEADS // LEVEL2_CHAINS, heads, 0)
        return carry

    lax.fori_loop(0, nsub, per_subtile, 0)


KEY_TABLE_ROWS = (N_KEYS // 2, N_KEYS, N_KEYS, N_KEYS // 2)


def _key_spec(nsub, rows, index_map):
    return pl.BlockSpec((PEER_HEADS, nsub, rows, LANES), index_map)


def _x2t_spec(x2t, tt, index_map):
    return pl.BlockSpec((tt // x2t.shape[2], D_MODEL, x2t.shape[2]), index_map)


def _x2t_block(x2t_ref):
    return jnp.concatenate([x2t_ref[t] for t in range(x2t_ref.shape[0])], axis=1)


def _peer_topk(x2t, wpqt, keys, flat, *, tt):
    n = x2t.shape[0] * x2t.shape[2]
    nsub = tt // LANES
    hc = 2 * PEER_HEADS
    return pl.pallas_call(
        _peer_topk_kernel,
        grid=(n // tt,),
        in_specs=[_x2t_spec(x2t, tt, lambda i: (i, 0, 0)), _full(wpqt.shape), _full(keys.shape),
                  _full(flat.shape)],
        out_specs=[_key_spec(nsub, rows, lambda i: (0, i, 0, 0)) for rows in KEY_TABLE_ROWS],
        out_shape=[jax.ShapeDtypeStruct((PEER_HEADS, n // LANES, rows, LANES), jnp.int32)
                   for rows in KEY_TABLE_ROWS],
        scratch_shapes=[pltpu.VMEM((nsub, PEER_HEADS * PEER_DK, LANES), BF16),
                        pltpu.VMEM((hc, nsub, N_KEYS, LANES), F32),
                        pltpu.VMEM((hc, nsub, N_KEYS, LANES), F32),
                        pltpu.VMEM((hc, nsub, PEER_TOPK, LANES), F32),
                        pltpu.VMEM((LEVEL2_CHAINS, CAND_ROWS, LANES), F32),
                        pltpu.VMEM((LEVEL2_CHAINS, CAND_ROWS, LANES), F32)],
        compiler_params=_params("parallel"),
        name="peer_topk",
    )(x2t, wpqt, keys, flat)


EXP_BLK = 1024
EXP_GROUPS = EXP_BLK // N_KEYS
SUBS_PER_PIECE = 2
EXP_STEPS = N_EXPERTS // (2 * EXP_BLK)


def _peer_experts_kernel(x2t_ref, x2_ref, rb_ref, na_ref, ea_ref, eb_ref, u_ref, vt_ref, ln3g_ref, ln3b_ref,
                         y_ref, acc_s, h0_s, h1_s, w0_s, w1_s):
    j = pl.program_id(1)
    nsub = x2t_ref.shape[0] * x2t_ref.shape[2] // LANES
    pieces = max(nsub // SUBS_PER_PIECE, 1)
    piece_subs = nsub // pieces
    d_rows = D_MODEL // pieces
    e_rows = EXP_BLK // pieces

    def key_row(ref, h, sub, ia):
        word = jnp.broadcast_to(ref[h, sub, pl.ds(ia, 1), :], (8, LANES))
        return jnp.tile(pltpu.bitcast(word, BF16), (N_KEYS // 16, 1))

    def gate_piece(h_s, w_s, blk, sub):
        for g in range(EXP_GROUPS):
            ia = blk * EXP_GROUPS + g
            gate = jnp.zeros((N_KEYS, LANES), BF16)
            for h in range(PEER_HEADS):
                eb = _unpacked(eb_ref[h, sub])
                keep = _unpacked(rb_ref[h, sub]) < key_row(na_ref, h, sub, ia)
                gate = gate + jnp.where(keep, eb, jnp.zeros_like(eb)) * key_row(ea_ref, h, sub, ia)
            act = _gelu_tanh(h_s[sub, g * N_KEYS:(g + 1) * N_KEYS, :]).astype(BF16) * gate
            w_s[sub, g * (N_KEYS // 2):(g + 1) * (N_KEYS // 2), :] = pltpu.bitcast(act, jnp.int32)

    def half_step(w_done, h_done, w_next, h_next, half, blk, *, down=True, gate=True, up=True):
        experts = slice(half * EXP_BLK, (half + 1) * EXP_BLK)

        def piece(i, carry):
            if up:
                u_rows = pl.ds(pl.multiple_of((half * EXP_BLK + i * e_rows) // 2, e_rows // 2), e_rows // 2)
                h_new = _dot(_unpacked(u_ref[u_rows, :]), _x2t_block(x2t_ref))
                for s in range(nsub):
                    h_next[s, pl.ds(pl.multiple_of(i * e_rows, e_rows), e_rows), :] = (
                        h_new[:, s * LANES:(s + 1) * LANES])
            if down:
                rows = pl.ds(pl.multiple_of(i * d_rows, d_rows), d_rows)
                w = jnp.concatenate([_unpacked(w_done[s]) for s in range(nsub)], axis=1)
                v_rows = pl.ds(pl.multiple_of(i * (d_rows // 2), d_rows // 2), d_rows // 2)
                acc_s[rows, :] += _dot(_unpacked(vt_ref[v_rows, experts]), w)
            if gate:
                for k in range(piece_subs):
                    gate_piece(h_done, w_next, blk, i * piece_subs + k)
            return carry

        lax.fori_loop(0, pieces, piece, 0)

    last = pl.num_programs(1) - 1

    @pl.when(j == 0)
    def _():
        acc_s[...] = jnp.zeros(acc_s.shape, F32)
        half_step(w0_s, h1_s, w1_s, h0_s, 0, 2 * j - 1, down=False, gate=False)
        half_step(w1_s, h0_s, w0_s, h1_s, 1, 2 * j, down=False)

    @pl.when((j > 0) & (j < last))
    def _():
        half_step(w0_s, h1_s, w1_s, h0_s, 0, 2 * j - 1)
        half_step(w1_s, h0_s, w0_s, h1_s, 1, 2 * j)

    @pl.when(j == last)
    def _():
        half_step(w0_s, h1_s, w1_s, h0_s, 0, 2 * j - 1, up=False)
        half_step(w1_s, h0_s, w0_s, h1_s, 1, 2 * j, gate=False, up=False)
        y_ref[...] = _layer_norm(ALPHA * x2_ref[...] + acc_s[...].T, ln3g_ref[...], ln3b_ref[...])


def _peer_experts(x2t, x2, key_arrs, u_bf, vt_bf, ln3g, ln3b, *, tt):
    n = x2.shape[0]
    nsub = tt // LANES
    return pl.pallas_call(
        _peer_experts_kernel,
        grid=(n // tt, EXP_STEPS + 1),
        in_specs=[_x2t_spec(x2t, tt, lambda i, j: (i, 0, 0)),
                  pl.BlockSpec((tt, D_MODEL), lambda i, j: (i, 0)),
                  *[_key_spec(nsub, rows, lambda i, j: (0, i, 0, 0)) for rows in KEY_TABLE_ROWS],
                  pl.BlockSpec((EXP_BLK, D_MODEL), lambda i, j: (jnp.minimum(j, EXP_STEPS - 1), 0)),
                  pl.BlockSpec((None, D_MODEL // 2, 2 * EXP_BLK), lambda i, j: (jnp.maximum(j - 1, 0), 0, 0)),
                  _full(ln3g.shape), _full(ln3b.shape)],
        out_specs=pl.BlockSpec((tt, D_MODEL), lambda i, j: (i, 0)),
        out_shape=jax.ShapeDtypeStruct((n, D_MODEL), F32),
        scratch_shapes=[pltpu.VMEM((D_MODEL, tt), F32),
                        pltpu.VMEM((nsub, EXP_BLK, LANES), F32), pltpu.VMEM((nsub, EXP_BLK, LANES), F32),
                        pltpu.VMEM((nsub, EXP_BLK // 2, LANES), jnp.int32),
                        pltpu.VMEM((nsub, EXP_BLK // 2, LANES), jnp.int32)],
        compiler_params=_params("parallel", "arbitrary"),
        name="peer_experts",
    )(x2t, x2, *key_arrs, u_bf, vt_bf, ln3g, ln3b)


def _peer(x2t, x2, peer_wts, *, tt):
    wpqt, keys, flat, u_bf, vt_bf, ln3g, ln3b = peer_wts
    key_arrs = _peer_topk(x2t, wpqt, keys, flat, tt=tt)
    return _peer_experts(x2t, x2, key_arrs, u_bf, vt_bf, ln3g, ln3b, tt=tt)


def _rope_tables(pos):
    inv = ROPE_THETA ** (-jnp.arange(0, QK_ROPE, 2, dtype=F32) / QK_ROPE)
    ang = pos.astype(F32)[:, None] * inv[None, :]
    cos, sin, zero = jnp.cos(ang), jnp.sin(ang), jnp.zeros_like(ang)
    pad = jnp.zeros((pos.shape[0], LANES - QK_ROPE), F32)
    return (jnp.concatenate([cos, cos, pad], axis=1),
            jnp.concatenate([-sin, zero, pad], axis=1),
            jnp.concatenate([zero, sin, pad], axis=1))


def _row(v):
    return v.reshape(1, -1).astype(F32)


def _pack_experts_kernel(u_ref, v_ref, up_ref, vtp_ref):
    up_ref[...] = _packed(u_ref[...])
    vtp_ref[...] = _packed(v_ref[...].T)


def _pack_experts(peer_u, peer_v):
    blk = 2 * EXP_BLK
    return pl.pallas_call(
        _pack_experts_kernel,
        grid=(EXP_STEPS,),
        in_specs=[pl.BlockSpec((blk, D_MODEL), lambda i: (i, 0))] * 2,
        out_specs=[pl.BlockSpec((blk // 2, D_MODEL), lambda i: (i, 0)),
                   pl.BlockSpec((None, D_MODEL // 2, blk), lambda i: (i, 0, 0))],
        out_shape=[jax.ShapeDtypeStruct((N_EXPERTS // 2, D_MODEL), jnp.int32),
                   jax.ShapeDtypeStruct((EXP_STEPS, D_MODEL // 2, blk), jnp.int32)],
        compiler_params=_params("parallel"),
        name="pack_experts",
    )(peer_u, peer_v)


def kernel(x_prompt, x_sample, mem_prompt, cache_ckv, cache_krope, cache_mem_k, cache_mem_v, page_table,
           w_in, q_norm_g, kv_norm_g, w_uq, w_uk, w_uv, gm_norm_g, gm_norm_b, gm_ws, gm_bs, attn_out_g,
           gm_out_g, w_out, ln1_g, ln1_b, w_xq, w_mk, w_mv, w_xo, ln2_g, ln2_b, w_pq, peer_keys, peer_u,
           peer_v, ln3_g, ln3_b):
    batch, seq = x_prompt.shape[:2]
    nb = x_sample.shape[0]
    past_len = page_table.shape[1] * PAGE_SIZE

    kr_pad = jnp.zeros((D_MODEL, LANES - QK_ROPE), F32)
    w_in_x = jnp.concatenate([w_in[:, :Q_RANK + KV_RANK + QK_ROPE], kr_pad,
                              w_in[:, Q_RANK + KV_RANK + QK_ROPE:]], axis=1).astype(BF16)
    uq_nope = w_uq[:, :, :QK_NOPE].reshape(Q_RANK, MLA_HEADS * QK_NOPE)
    uq_rope = jnp.pad(w_uq[:, :, QK_NOPE:], ((0, 0), (0, 0), (0, LANES - QK_ROPE)))
    w_uq_x = jnp.concatenate([uq_nope, uq_rope.reshape(Q_RANK, MLA_HEADS * LANES)], axis=1).astype(BF16)
    eye = jnp.eye(MLA_HEADS, dtype=F32)
    w_uk_bd = jnp.einsum('rhd,hg->hdgr', w_uk, eye).reshape(MLA_HEADS * QK_NOPE, MLA_HEADS * KV_RANK).astype(BF16)
    w_uv_bd = jnp.einsum('rhd,hg->hrgd', w_uv, eye).reshape(MLA_HEADS * KV_RANK, MLA_WIDTH).astype(BF16)
    w_uv_t = jnp.transpose(w_uv, (1, 2, 0)).astype(BF16)
    proj_wts = (w_in_x, _row(q_norm_g), _row(kv_norm_g), w_uq_x, w_uk_bd, _row(gm_norm_g), _row(gm_norm_b))
    g_attn_col = jnp.broadcast_to(attn_out_g.astype(F32)[:, None], (MLA_WIDTH, Q_BLK))
    bias_tile = jnp.repeat(gm_bs.T, GM_WIDTH // GM_GROUPS, axis=1).astype(F32)
    w_out_bf, w_xq_bf = w_out.astype(BF16), w_xq.reshape(D_MODEL, X_WIDTH).astype(BF16)
    w_xo_bf = w_xo.reshape(X_WIDTH, D_MODEL).astype(BF16)
    post_tail = (w_xo_bf, _row(ln2_g), _row(ln2_b))
    post_wts = (gm_ws.astype(F32), bias_tile, _row(gm_out_g), w_out_bf, _row(ln1_g), _row(ln1_b), w_xq_bf) + post_tail
    ws0 = jnp.repeat(gm_ws[:, 0, 0], GM_WIDTH // GM_GROUPS)
    bs0 = jnp.repeat(gm_bs[:, 0], GM_WIDTH // GM_GROUPS)
    sample_wts = (w_uv_bd, _row(attn_out_g), _row(ws0), _row(bs0), _row(gm_out_g), w_out_bf, _row(ln1_g),
                  _row(ln1_b), w_xq_bf) + post_tail
    flat = np.full((CAND_ROWS,), 1e8, np.float32)
    for ka in range(PEER_TOPK):
        flat[CAND_OFF[ka]:CAND_OFF[ka] + CAND_NB[ka]] = ka * PEER_TOPK + np.arange(CAND_NB[ka])
    flat = jnp.asarray(np.broadcast_to(flat[:, None], (CAND_ROWS, LANES)))
    peer_wts = (w_pq.reshape(D_MODEL, PEER_HEADS * PEER_DK).T.astype(BF16),
                peer_keys.reshape(2 * PEER_HEADS, N_KEYS, PEER_HALF).astype(BF16), flat,
                *_pack_experts(peer_u, peer_v), _row(ln3_g), _row(ln3_b))

    n_p = batch * seq
    xp = x_prompt.reshape(n_p, D_MODEL)
    tm = 512
    qlat, qrope, kcat, ckv_p, krope_p, u_p, v_p, ckvt = _proj(
        xp, _rope_tables(jnp.arange(seq)), proj_wts, tm=tm, seq_blocks=seq // tm, emit_kt=True)
    a_p = _attn_prompt(qlat, qrope, kcat, ckvt, w_uv_t, g_attn_col, batch=batch, seq=seq)
    mk_p, mv_p = _memkv(mem_prompt.reshape(batch * MEM_TOKENS, D_MODEL),
                        w_mk.reshape(D_MODEL, X_WIDTH).astype(BF16), w_mv.reshape(D_MODEL, X_WIDTH).astype(BF16))
    x2_p, x2t_p = _post_prompt(xp, a_p, u_p, v_p, mk_p, mv_p, post_wts, batch=batch, seq=seq)
    y_p = _peer(x2t_p, x2_p, peer_wts, tt=512)

    xs = x_sample.reshape(nb, D_MODEL)
    pos_s = jnp.full((nb,), past_len, jnp.int32)
    qlat_s, qrope_s, _, ckv_s, krope_s, u_s, v_s = _proj(xs, _rope_tables(pos_s), proj_wts, tm=nb, seq_blocks=1,
                                                          emit_kt=False)
    q_cat = jnp.concatenate([qlat_s.reshape(nb, MLA_HEADS, KV_RANK), qrope_s.reshape(nb, MLA_HEADS, LANES)], axis=2)
    q_pad = jnp.pad(q_cat, ((0, 0), (0, LANES - MLA_HEADS), (0, 0)))
    o_lat_s = _attn_decode(page_table, q_pad, cache_ckv, cache_krope, ckv_s, krope_s)
    x2_s = _post_sample(xs, o_lat_s.reshape(nb, MLA_HEADS * KV_RANK), u_s, v_s,
                        cache_mem_k.reshape(nb, MEM_TOKENS, X_WIDTH), cache_mem_v.reshape(nb, MEM_TOKENS, X_WIDTH),
                        sample_wts)
    y_s = _peer(x2_s.T.astype(BF16).reshape(1, D_MODEL, nb), x2_s, peer_wts, tt=nb)

    return (y_p.reshape(batch, seq, D_MODEL), y_s.reshape(nb, 1, D_MODEL),
            ckv_p.reshape(batch, seq, KV_RANK), krope_p.reshape(batch, seq, QK_ROPE),
            mk_p.reshape(batch, MEM_TOKENS, X_HEADS, X_HEAD_DIM), mv_p.reshape(batch, MEM_TOKENS, X_HEADS, X_HEAD_DIM),
            ckv_s.reshape(nb, 1, KV_RANK), krope_s.reshape(nb, 1, QK_ROPE), v_s.reshape(nb, 1, GM_WIDTH))
```

```python
import functools

import jax
import jax.numpy as jnp
import numpy as np
from jax import lax
from jax.experimental import pallas as pl
from jax.experimental.pallas import tpu as pltpu

F32 = jnp.float32
BF16 = jnp.bfloat16

D_MODEL = 1024
MLA_HEADS = 8
QK_NOPE = 64
QK_ROPE = 32
V_HEAD = 64
Q_RANK = 256
KV_RANK = 128
MLA_WIDTH = MLA_HEADS * V_HEAD
MLA_SCALE = (QK_NOPE + QK_ROPE) ** -0.5
Q_SCALE = float(MLA_SCALE * np.log2(np.e))
ROPE_THETA = 10000.0
GM_WIDTH = D_MODEL // 2
GM_GROUPS = 4
GM_CHUNK = 128
MEM_TOKENS = 256
X_HEADS = 4
X_HEAD_DIM = 128
X_WIDTH = X_HEADS * X_HEAD_DIM
X_SCALE = X_HEAD_DIM ** -0.5
PEER_HEADS = 8
N_KEYS = 128
N_EXPERTS = N_KEYS * N_KEYS
PEER_TOPK = 16
PEER_DK = 256
PEER_HALF = PEER_DK // 2
PAGE_SIZE = 128
DEPTH = 1
ALPHA = (2.0 * DEPTH) ** 0.25
EPS = 1e-5

LANES = 128
VMEM_LIMIT = 56 * 1024 * 1024

CAND_NB = tuple(PEER_TOPK // (ka + 1) for ka in range(PEER_TOPK))
CAND_OFF = tuple(int(sum(CAND_NB[:ka])) for ka in range(PEER_TOPK))
CAND_N = int(sum(CAND_NB))
CAND_ROWS = 56
LEVEL1_CHAINS = 16
LEVEL2_CHAINS = 8
NEG_INF = float("-inf")


def _dot(a, b):
    return jnp.dot(a, b, preferred_element_type=F32)


def _dot_nt(a, b):
    return lax.dot_general(a, b, (((1,), (1,)), ((), ())), preferred_element_type=F32)


def _dot_tn(a, b):
    return lax.dot_general(a, b, (((0,), (0,)), ((), ())), preferred_element_type=F32)


def _layer_norm(x, g, b):
    mu = jnp.mean(x, -1, keepdims=True)
    var = jnp.mean(jnp.square(x - mu), -1, keepdims=True)
    return (x - mu) * lax.rsqrt(var + EPS) * g + b


def _rms_norm(x, g):
    return x * lax.rsqrt(jnp.mean(jnp.square(x), -1, keepdims=True) + EPS) * g


_GELU_K1 = float(-2.0 * np.sqrt(2.0 / np.pi) * np.log2(np.e))
_GELU_K2 = float(0.044715 * _GELU_K1)


def _gelu_tanh(x):
    return x / (1.0 + jnp.exp2(x * (_GELU_K1 + _GELU_K2 * (x * x))))


def _params(*sem):
    return pltpu.CompilerParams(dimension_semantics=sem, vmem_limit_bytes=VMEM_LIMIT)


def _full(shape):
    n = len(shape)
    return pl.BlockSpec(shape, lambda *_: (0,) * n)


def _rope(x, c, s_lo, s_hi):
    width = x.shape[-1]
    return x * c + pltpu.roll(x, width - 16, 1) * s_lo + pltpu.roll(x, 16, 1) * s_hi


def _proj_kernel(x_ref, c_ref, slo_ref, shi_ref, w_in_ref, qg_ref, kvg_ref, w_uq_ref, w_uk_ref,
                 gmg_ref, gmb_ref, qlat_ref, qrope_ref, kcat_ref, ckv_ref, krope_ref, u_ref, v_ref,
                 *maybe_ckvt_ref):
    h = _dot(x_ref[...].astype(BF16), w_in_ref[...])
    c, s_lo, s_hi = c_ref[...], slo_ref[...], shi_ref[...]

    cq = _rms_norm(h[:, :Q_RANK], qg_ref[...])
    q_all = _dot(cq.astype(BF16), w_uq_ref[...])
    q_nope = q_all[:, :MLA_HEADS * QK_NOPE]
    q_lat = _dot(q_nope.astype(BF16), w_uk_ref[...])
    qlat_ref[...] = (q_lat * Q_SCALE).astype(BF16)
    q_rope = _rope(q_all[:, MLA_HEADS * QK_NOPE:], jnp.tile(c, (1, MLA_HEADS)),
                   jnp.tile(s_lo, (1, MLA_HEADS)), jnp.tile(s_hi, (1, MLA_HEADS)))
    qrope_ref[...] = (q_rope * Q_SCALE).astype(BF16)

    ckv = _rms_norm(h[:, Q_RANK:Q_RANK + KV_RANK], kvg_ref[...])
    ckv_ref[...] = ckv
    k_rot = _rope(h[:, Q_RANK + KV_RANK:Q_RANK + KV_RANK + LANES], c, s_lo, s_hi)
    krope_ref[...] = k_rot[:, :QK_ROPE]
    kcat_ref[...] = jnp.concatenate([ckv, k_rot], axis=1).astype(BF16)
    for ckvt_ref in maybe_ckvt_ref:
        for j in range(ckvt_ref.shape[0]):
            ckvt_ref[j] = ckv[j * 256:(j + 1) * 256, :].T.astype(BF16)

    uv = jax.nn.gelu(h[:, Q_RANK + KV_RANK + LANES:])
    u_ref[...] = uv[:, :GM_WIDTH]
    v_ref[...] = _layer_norm(uv[:, GM_WIDTH:], gmg_ref[...], gmb_ref[...])


def _proj(x, tables, wts, *, tm, seq_blocks, emit_kt):
    n = x.shape[0]
    row = lambda w: pl.BlockSpec((tm, w), lambda i: (i, 0))
    tab = pl.BlockSpec((tm, LANES), lambda i: (i % seq_blocks, 0))
    w_in, qg, kvg, w_uq, w_uk, gmg, gmb = wts
    out_shape = [
        jax.ShapeDtypeStruct((n, MLA_HEADS * KV_RANK), BF16),
        jax.ShapeDtypeStruct((n, MLA_HEADS * LANES), BF16),
        jax.ShapeDtypeStruct((n, 2 * LANES), BF16),
        jax.ShapeDtypeStruct((n, KV_RANK), F32),
        jax.ShapeDtypeStruct((n, QK_ROPE), F32),
        jax.ShapeDtypeStruct((n, GM_WIDTH), F32),
        jax.ShapeDtypeStruct((n, GM_WIDTH), F32),
    ]
    out_specs = [row(MLA_HEADS * KV_RANK), row(MLA_HEADS * LANES), row(2 * LANES), row(KV_RANK),
                 row(QK_ROPE), row(GM_WIDTH), row(GM_WIDTH)]
    if emit_kt:
        out_shape.append(jax.ShapeDtypeStruct((n // 256, KV_RANK, 256), BF16))
        out_specs.append(pl.BlockSpec((tm // 256, KV_RANK, 256), lambda i: (i, 0, 0)))
    return pl.pallas_call(
        _proj_kernel,
        grid=(n // tm,),
        in_specs=[row(D_MODEL), tab, tab, tab, _full(w_in.shape), _full(qg.shape), _full(kvg.shape),
                  _full(w_uq.shape), _full(w_uk.shape), _full(gmg.shape), _full(gmb.shape)],
        out_specs=out_specs,
        out_shape=out_shape,
        compiler_params=_params("parallel"),
        name="proj",
    )(x, *tables, w_in, qg, kvg, w_uq, w_uk, gmg, gmb)


Q_BLK = 256
KV_BLK = 256


ATTN_COLS = MLA_HEADS * Q_BLK


def _attn_kernel(qlat_ref, qrope_ref, kcat_ref, ckvt_ref, wuvt_ref, g_ref, a_ref, q_s, m_s, l_s, acc_s):
    qi = pl.program_id(1)
    ql, qr = qlat_ref[...], qrope_ref[...]
    hpc = ATTN_COLS // Q_BLK
    for h in range(MLA_HEADS):
        q_s[h // hpc, (h % hpc) * Q_BLK:(h % hpc + 1) * Q_BLK, :] = jnp.concatenate(
            [ql[:, h * LANES:(h + 1) * LANES], qr[:, h * LANES:(h + 1) * LANES]], axis=1)
    m_s[...] = jnp.full(m_s.shape, NEG_INF, F32)
    l_s[...] = jnp.zeros(l_s.shape, F32)
    acc_s[...] = jnp.zeros(acc_s.shape, F32)
    q_pos = qi * Q_BLK + (lax.broadcasted_iota(jnp.int32, (KV_BLK, ATTN_COLS), 1) & (Q_BLK - 1))
    k_off = lax.broadcasted_iota(jnp.int32, (KV_BLK, ATTN_COLS), 0)

    def scores(j, c):
        k = kcat_ref[pl.ds(pl.multiple_of(j * KV_BLK, KV_BLK), KV_BLK), :]
        return _dot_nt(k, q_s[c])

    def update(j, c, st, diagonal):
        if diagonal:
            st = jnp.where(q_pos >= k_off + j * KV_BLK, st, NEG_INF)
        m_old = m_s[c]
        m_new = jnp.maximum(m_old, jnp.max(st, axis=0, keepdims=True))
        alpha = jnp.exp2(m_old - m_new)
        p = jnp.exp2(st - m_new)
        l_s[c] = alpha * l_s[c] + jnp.sum(p, axis=0, keepdims=True)
        acc_s[c] = alpha * acc_s[c] + _dot(ckvt_ref[j], p.astype(BF16))
        m_s[c] = m_new

    def step(j, diagonal):
        for c in range(MLA_HEADS // hpc):
            update(j, c, scores(j, c), diagonal)

    last = (qi * Q_BLK) // KV_BLK

    def visible_pair(t, carry):
        for c in range(MLA_HEADS // hpc):
            st_a, st_b = scores(2 * t, c), scores(2 * t + 1, c)
            update(2 * t, c, st_a, False)
            update(2 * t + 1, c, st_b, False)
        return carry

    lax.fori_loop(0, last // 2, visible_pair, 0)

    @pl.when(last % 2 == 1)
    def _():
        step(last - 1, False)

    step(last, True)

    o_t = (acc_s[...] / l_s[...]).astype(BF16)
    om_t = jnp.concatenate(
        [_dot(wuvt_ref[h], o_t[h // hpc, :, (h % hpc) * Q_BLK:(h % hpc + 1) * Q_BLK])
         for h in range(MLA_HEADS)], axis=0)
    ms = jnp.mean(jnp.square(om_t), axis=0, keepdims=True)
    a_t = om_t * lax.rsqrt(ms + EPS) * g_ref[...]
    a_ref[...] = a_t.T.astype(BF16)


def _attn_prompt(qlat, qrope, kcat, ckvt, wuvt, g_attn, *, batch, seq):
    n = batch * seq
    nq = seq // Q_BLK
    chains = MLA_HEADS * Q_BLK // ATTN_COLS
    return pl.pallas_call(
        _attn_kernel,
        grid=(batch, nq),
        in_specs=[
            pl.BlockSpec((Q_BLK, MLA_HEADS * KV_RANK), lambda b, i: (b * nq + i, 0)),
            pl.BlockSpec((Q_BLK, MLA_HEADS * LANES), lambda b, i: (b * nq + i, 0)),
            pl.BlockSpec((None, seq, 2 * LANES), lambda b, i: (b, 0, 0)),
            pl.BlockSpec((None, seq // KV_BLK, KV_RANK, KV_BLK), lambda b, i: (b, 0, 0, 0)),
            _full(wuvt.shape), _full(g_attn.shape),
        ],
        out_specs=pl.BlockSpec((Q_BLK, MLA_WIDTH), lambda b, i: (b * nq + i, 0)),
        out_shape=jax.ShapeDtypeStruct((n, MLA_WIDTH), BF16),
        scratch_shapes=[pltpu.VMEM((chains, ATTN_COLS, 2 * LANES), BF16), pltpu.VMEM((chains, 1, ATTN_COLS), F32),
                        pltpu.VMEM((chains, 1, ATTN_COLS), F32), pltpu.VMEM((chains, KV_RANK, ATTN_COLS), F32)],
        compiler_params=_params("parallel", "arbitrary"),
        name="attn_prompt",
    )(qlat, qrope, kcat.reshape(batch, seq, 2 * LANES),
      ckvt.reshape(batch, seq // KV_BLK, KV_RANK, KV_BLK), wuvt, g_attn)


PAGES_PER_STEP = 8


SEQS_PER_STEP = 4


def _decode_kernel(pt_ref, q_ref, *refs):
    del pt_ref
    n_pg = SEQS_PER_STEP * PAGES_PER_STEP
    ckv_refs, kr_refs = refs[:n_pg], refs[n_pg:2 * n_pg]
    ckvn_ref, krn_ref, o_ref, m_s, l_s, acc_s = refs[2 * n_pg:]
    c = pl.program_id(1)

    @pl.when(c == 0)
    def _():
        m_s[...] = jnp.full(m_s.shape, NEG_INF, F32)
        l_s[...] = jnp.zeros(l_s.shape, F32)
        acc_s[...] = jnp.zeros(acc_s.shape, F32)

    def keys(ckv, kr_wide):
        return jnp.concatenate([ckv, kr_wide], axis=1).astype(BF16)

    def rotary_page(kr_t):
        return jnp.concatenate([kr_t, jnp.zeros((LANES - QK_ROPE, PAGE_SIZE), F32)], axis=0).T

    def update(b, kcat, valid_rows):
        half = kcat.shape[0] // 2
        if valid_rows is None:
            st = jnp.concatenate([_dot_nt(kcat[:half], q_ref[b]), _dot_nt(kcat[half:], q_ref[b])], axis=0)
        else:
            st = _dot_nt(kcat, q_ref[b])
        if valid_rows is not None:
            st = jnp.where(lax.broadcasted_iota(jnp.int32, st.shape, 0) < valid_rows, st, NEG_INF)
        m_old = m_s[b]
        m_new = jnp.maximum(m_old, jnp.max(st, axis=0, keepdims=True))
        alpha = jnp.exp2(m_old - m_new)
        p = jnp.exp2(st - m_new)
        l_s[b] = alpha * l_s[b] + jnp.sum(p, axis=0, keepdims=True)
        acc_s[b] = alpha * acc_s[b] + _dot_tn(kcat[:, :KV_RANK], p.astype(BF16))
        m_s[b] = m_new

    for b in range(SEQS_PER_STEP):
        pages = slice(b * PAGES_PER_STEP, (b + 1) * PAGES_PER_STEP)
        update(b, keys(jnp.concatenate([r[...] for r in ckv_refs[pages]], axis=0),
                       jnp.concatenate([rotary_page(r[...]) for r in kr_refs[pages]], axis=0)), None)

    @pl.when(c == pl.num_programs(1) - 1)
    def _():
        for b in range(SEQS_PER_STEP):
            krn = jnp.concatenate([krn_ref[b], jnp.zeros((1, LANES - QK_ROPE), F32)], axis=1)
            update(b, keys(jnp.broadcast_to(ckvn_ref[b], (16, KV_RANK)), jnp.broadcast_to(krn, (16, LANES))), 1)
            o_t = acc_s[b] / l_s[b]
            o_ref[b] = o_t.T[:MLA_HEADS, :]


def _attn_decode(page_table, q_pad, cache_ckv, cache_krope, ckv_new, krope_new):
    nb, n_pages = page_table.shape
    steps = n_pages // PAGES_PER_STEP
    sq = SEQS_PER_STEP

    def page_spec(rows, width, b, i):
        return pl.BlockSpec((None, rows, width),
                            lambda g, c, pt: (pt[g * sq + b, c * PAGES_PER_STEP + i], 0, 0))

    def seq_spec(*dims):
        return pl.BlockSpec((sq,) + dims, lambda g, c, pt: (g,) + (0,) * len(dims))

    in_specs = [seq_spec(LANES, 2 * LANES)]
    in_specs += [page_spec(PAGE_SIZE, KV_RANK, b, i) for b in range(sq) for i in range(PAGES_PER_STEP)]
    in_specs += [page_spec(QK_ROPE, PAGE_SIZE, b, i) for b in range(sq) for i in range(PAGES_PER_STEP)]
    in_specs += [seq_spec(1, KV_RANK), seq_spec(1, QK_ROPE)]
    n_pg = sq * PAGES_PER_STEP
    return pl.pallas_call(
        _decode_kernel,
        grid_spec=pltpu.PrefetchScalarGridSpec(
            num_scalar_prefetch=1,
            grid=(nb // sq, steps),
            in_specs=in_specs,
            out_specs=seq_spec(MLA_HEADS, KV_RANK),
            scratch_shapes=[pltpu.VMEM((sq, 1, LANES), F32), pltpu.VMEM((sq, 1, LANES), F32),
                            pltpu.VMEM((sq, KV_RANK, LANES), F32)],
        ),
        out_shape=jax.ShapeDtypeStruct((nb, MLA_HEADS, KV_RANK), F32),
        compiler_params=_params("parallel", "arbitrary"),
        name="attn_decode",
    )(page_table, q_pad, *([cache_ckv] * n_pg), *([jnp.swapaxes(cache_krope, 1, 2)] * n_pg),
      ckv_new.reshape(nb, 1, KV_RANK), krope_new.reshape(nb, 1, QK_ROPE))


def _memkv_kernel(mem_ref, wk_ref, wv_ref, mk_ref, mv_ref):
    m = mem_ref[...].astype(BF16)
    mk_ref[...] = _dot(m, wk_ref[...])
    mv_ref[...] = _dot(m, wv_ref[...])


def _memkv(mem, w_mk, w_mv):
    n = mem.shape[0]
    tm = 512
    return pl.pallas_call(
        _memkv_kernel,
        grid=(n // tm,),
        in_specs=[pl.BlockSpec((tm, D_MODEL), lambda i: (i, 0)), _full(w_mk.shape), _full(w_mv.shape)],
        out_specs=[pl.BlockSpec((tm, X_WIDTH), lambda i: (i, 0))] * 2,
        out_shape=[jax.ShapeDtypeStruct((n, X_WIDTH), F32)] * 2,
        compiler_params=_params("parallel"),
        name="memkv",
    )(mem, w_mk, w_mv)


POST_ROWS = 512


def _softmax_rows(s):
    e = jnp.exp(s - jnp.max(s, axis=-1, keepdims=True))
    return e / jnp.sum(e, axis=-1, keepdims=True)


def _mix_and_ln1(x, a_bf, o_gm, gmog, w_out, ln1g, ln1b):
    gm_n = _rms_norm(o_gm, gmog)
    y = jnp.concatenate([a_bf, gm_n.astype(BF16)], axis=1)
    return _layer_norm(ALPHA * x + _dot(y, w_out), ln1g, ln1b)


def _post_prompt_kernel(x_ref, a_ref, u_ref, v_ref, ws_ref, bias_ref, gmog_ref, wout_ref, ln1g_ref, ln1b_ref,
                        wxq_ref, mk_ref, mv_ref, wxo_ref, ln2g_ref, ln2b_ref, x2_ref, x2t_ref):
    tril = (lax.broadcasted_iota(jnp.int32, (GM_CHUNK, GM_CHUNK), 0)
            >= lax.broadcasted_iota(jnp.int32, (GM_CHUNK, GM_CHUNK), 1))
    w_s = [jnp.where(tril, ws_ref[g], 0.0).astype(BF16) for g in range(GM_GROUPS)]
    chunks = []
    for c in range(POST_ROWS // GM_CHUNK):
        rows = slice(c * GM_CHUNK, (c + 1) * GM_CHUNK)
        v_c = v_ref[rows, :].astype(BF16)
        s = jnp.concatenate([_dot(w_s[g], v_c[:, g * LANES:(g + 1) * LANES]) for g in range(GM_GROUPS)],
                            axis=1) + bias_ref[...]
        chunks.append(u_ref[rows, :] * s)
    o_gm = jnp.concatenate(chunks, axis=0)
    x1 = _mix_and_ln1(x_ref[...], a_ref[...], o_gm, gmog_ref[...], wout_ref[...], ln1g_ref[...], ln1b_ref[...])

    q = _dot(x1.astype(BF16), wxq_ref[...]).astype(BF16)
    mk, mv = mk_ref[...].astype(BF16), mv_ref[...].astype(BF16)
    heads = []
    for h in range(X_HEADS):
        cs = slice(h * X_HEAD_DIM, (h + 1) * X_HEAD_DIM)
        p = _softmax_rows(_dot_nt(q[:, cs], mk[:, cs]) * X_SCALE)
        heads.append(_dot(p.astype(BF16), mv[:, cs]))
    o = jnp.concatenate(heads, axis=1).astype(BF16)
    x2 = _layer_norm(ALPHA * x1 + _dot(o, wxo_ref[...]), ln2g_ref[...], ln2b_ref[...])
    x2_ref[...] = x2
    x2t_ref[...] = x2.T.astype(BF16)


def _post_prompt(x, a, u, v, mk, mv, wts, *, batch, seq):
    n = batch * seq
    nb = seq // POST_ROWS
    row = lambda w: pl.BlockSpec((POST_ROWS, w), lambda b, i: (b * nb + i, 0))
    mem = pl.BlockSpec((MEM_TOKENS, X_WIDTH), lambda b, i: (b, 0))
    return pl.pallas_call(
        _post_prompt_kernel,
        grid=(batch, nb),
        in_specs=[row(D_MODEL), row(MLA_WIDTH), row(GM_WIDTH), row(GM_WIDTH)]
                 + [_full(w.shape) for w in wts[:6]] + [_full(wts[6].shape), mem, mem]
                 + [_full(w.shape) for w in wts[7:]],
        out_specs=[row(D_MODEL), pl.BlockSpec((None, D_MODEL, POST_ROWS), lambda b, i: (b * nb + i, 0, 0))],
        out_shape=[jax.ShapeDtypeStruct((n, D_MODEL), F32),
                   jax.ShapeDtypeStruct((n // POST_ROWS, D_MODEL, POST_ROWS), BF16)],
        compiler_params=_params("parallel", "parallel"),
        name="post_prompt",
    )(x, a, u, v, *wts[:7], mk, mv, *wts[7:])


SAMPLE_ROWS = 8


def _post_sample_kernel(x_ref, o_ref, u_ref, v_ref, wuv_ref, ag_ref, ws0_ref, bs0_ref, gmog_ref, wout_ref,
                        ln1g_ref, ln1b_ref, wxq_ref, mk_ref, mv_ref, wxo_ref, ln2g_ref, ln2b_ref, x2_ref):
    o_mla = _dot(o_ref[...].astype(BF16), wuv_ref[...])
    a = _rms_norm(o_mla, ag_ref[...]).astype(BF16)
    o_gm = u_ref[...] * (ws0_ref[...] * v_ref[...] + bs0_ref[...])
    x1 = _mix_and_ln1(x_ref[...], a, o_gm, gmog_ref[...], wout_ref[...], ln1g_ref[...], ln1b_ref[...])

    q = _dot(x1.astype(BF16), wxq_ref[...])
    lane_head = lax.broadcasted_iota(jnp.int32, (LANES, X_WIDTH), 1) // X_HEAD_DIM
    on_head = lane_head == lax.broadcasted_iota(jnp.int32, (LANES, X_WIDTH), 0)
    rows = []
    for j in range(SAMPLE_ROWS):
        q_bd = jnp.where(on_head, q[j:j + 1, :], 0.0).astype(BF16)
        s = _dot_nt(mk_ref[j].astype(BF16), q_bd) * X_SCALE
        e = jnp.exp(s - jnp.max(s, axis=0, keepdims=True))
        p = e / jnp.sum(e, axis=0, keepdims=True)
        o_all = _dot_tn(p.astype(BF16), mv_ref[j].astype(BF16))
        rows.append(jnp.sum(jnp.where(on_head, o_all, 0.0), axis=0, keepdims=True))
    o = jnp.concatenate(rows, axis=0).astype(BF16)
    x2_ref[...] = _layer_norm(ALPHA * x1 + _dot(o, wxo_ref[...]), ln2g_ref[...], ln2b_ref[...])


def _post_sample(x, o_lat, u, v, mk, mv, wts):
    n = x.shape[0]
    row = lambda w: pl.BlockSpec((SAMPLE_ROWS, w), lambda i: (i, 0))
    mem = pl.BlockSpec((SAMPLE_ROWS, MEM_TOKENS, X_WIDTH), lambda i: (i, 0, 0))
    return pl.pallas_call(
        _post_sample_kernel,
        grid=(n // SAMPLE_ROWS,),
        in_specs=[row(D_MODEL), row(MLA_HEADS * KV_RANK), row(GM_WIDTH), row(GM_WIDTH)]
                 + [_full(w.shape) for w in wts[:9]] + [mem, mem] + [_full(w.shape) for w in wts[9:]],
        out_specs=row(D_MODEL),
        out_shape=jax.ShapeDtypeStruct((n, D_MODEL), F32),
        compiler_params=_params("parallel"),
        name="post_sample",
    )(x, o_lat, u, v, *wts[:9], mk, mv, *wts[9:])


def _top16(val, row_id, *, break_ties):
    rank = jnp.full(val.shape, 127.0, F32)
    tops = []
    for k in range(PEER_TOPK):
        m = jnp.max(val, axis=0, keepdims=True)
        hit = val == m
        if break_ties:
            hit = row_id == jnp.min(jnp.where(hit, row_id, 1e9), axis=0, keepdims=True)
        val = jnp.where(hit, NEG_INF, val)
        rank = jnp.where(hit, float(k), rank)
        tops.append(m)
    return tops, rank


def _tied(rank):
    marked = jnp.sum(jnp.where(rank < float(PEER_TOPK), 1.0, 0.0), axis=0, keepdims=True)
    return marked - float(PEER_TOPK)


def _packed(x):
    return pltpu.bitcast(x.astype(BF16), jnp.int32)


def _unpacked(w):
    return pltpu.bitcast(w, BF16)


def _bf16_pair(x):
    u = pltpu.bitcast(x, jnp.int32)
    hi = lax.shift_right_logical(u + 0x7FFF + (lax.shift_right_logical(u, 16) & 1), 16)
    return hi | lax.shift_left(hi, 16)


def _peer_topk_kernel(x2t_ref, wpqt_ref, keys_ref, flat_ref, rb_ref, na_ref, ea_ref, eb_ref,
                      qt_s, s_s, rank_s, top_s, cand_s, sel_s):
    nsub = x2t_ref.shape[0] * x2t_ref.shape[2] // LANES
    qt = _dot(wpqt_ref[...], _x2t_block(x2t_ref)).astype(BF16)
    for sub in range(nsub):
        qt_s[sub] = qt[:, sub * LANES:(sub + 1) * LANES]
    key_id = lax.broadcasted_iota(jnp.int32, (N_KEYS, LANES), 0).astype(F32)
    flat = flat_ref[...]

    def keep_level1(hc, sub, tops, rank):
        rank_s[hc, sub] = rank
        for k in range(PEER_TOPK):
            top_s[hc, sub, k:k + 1, :] = tops[k]

    def level2(h, sub, u):
        sa, sb = top_s[2 * h, sub], top_s[2 * h + 1, sub]
        ea_r = jnp.exp(sa - sa[0:1, :])
        eb_r = jnp.exp(sb - sb[0:1, :])
        for ka in range(PEER_TOPK):
            cand_s[u, CAND_OFF[ka]:CAND_OFF[ka] + CAND_NB[ka], :] = sa[ka:ka + 1, :] + sb[0:CAND_NB[ka], :]
        cand_s[u, CAND_N:CAND_ROWS, :] = jnp.full((CAND_ROWS - CAND_N, LANES), NEG_INF, F32)
        _, crank = _top16(cand_s[u], flat, break_ties=True)
        sel_s[u] = jnp.where(crank < float(PEER_TOPK), 1.0, 0.0)
        n_a, z = [], jnp.zeros((1, LANES), F32)
        for ka in range(PEER_TOPK):
            sel_ka = sel_s[u, CAND_OFF[ka]:CAND_OFF[ka] + CAND_NB[ka], :]
            n_a.append(jnp.sum(sel_ka, axis=0, keepdims=True))
            z = z + ea_r[ka:ka + 1, :] * jnp.sum(sel_ka * eb_r[0:CAND_NB[ka], :], axis=0, keepdims=True)
        rank_a = rank_s[2 * h, sub]
        na = jnp.zeros((N_KEYS, LANES), F32)
        for ka in range(PEER_TOPK):
            na = jnp.where(rank_a == float(ka), n_a[ka], na)
        na_ref[h, sub] = _bf16_pair(na)
        rb_ref[h, sub] = _packed(rank_s[2 * h + 1, sub])
        ea_ref[h, sub] = _bf16_pair(jnp.exp(s_s[2 * h, sub] - sa[0:1, :]))
        eb_ref[h, sub] = _packed(jnp.exp(s_s[2 * h + 1, sub] - sb[0:1, :]) / z)

    def per_subtile(sub, carry):
        def quick(h, tied):
            for hc in [LEVEL1_CHAINS * h + u for u in range(LEVEL1_CHAINS)]:
                q_blk = qt_s[sub, pl.ds(pl.multiple_of(hc * PEER_HALF, PEER_HALF), PEER_HALF), :]
                s = _dot(keys_ref[hc], q_blk)
                s_s[hc, sub] = s
                tops, rank = _top16(s, key_id, break_ties=False)
                keep_level1(hc, sub, tops, rank)
                tied = jnp.maximum(tied, _tied(rank))
            return tied

        tied = lax.fori_loop(0, 2 * PEER_HEADS // LEVEL1_CHAINS, quick, jnp.zeros((1, LANES), F32))

        @pl.when(jnp.max(tied) > 0.0)
        def _():
            def careful(hc, c):
                keep_level1(hc, sub, *_top16(s_s[hc, sub], key_id, break_ties=True))
                return c
            lax.fori_loop(0, 2 * PEER_HEADS, careful, 0)

        def heads(q, c):
            for u in range(LEVEL2_CHAINS):
                level2(q * LEVEL2_CHAINS + u, sub, u)
            return c

        lax.fori_loop(0, PEER_HEADS // LEVEL2_CHAINS, heads, 0)
        return carry

    lax.fori_loop(0, nsub, per_subtile, 0)


KEY_TABLE_ROWS = (N_KEYS // 2, N_KEYS, N_KEYS, N_KEYS // 2)


def _key_spec(nsub, rows, index_map):
    return pl.BlockSpec((PEER_HEADS, nsub, rows, LANES), index_map)


def _x2t_spec(x2t, tt, index_map):
    return pl.BlockSpec((tt // x2t.shape[2], D_MODEL, x2t.shape[2]), index_map)


def _x2t_block(x2t_ref):
    return jnp.concatenate([x2t_ref[t] for t in range(x2t_ref.shape[0])], axis=1)


def _peer_topk(x2t, wpqt, keys, flat, *, tt):
    n = x2t.shape[0] * x2t.shape[2]
    nsub = tt // LANES
    hc = 2 * PEER_HEADS
    return pl.pallas_call(
        _peer_topk_kernel,
        grid=(n // tt,),
        in_specs=[_x2t_spec(x2t, tt, lambda i: (i, 0, 0)), _full(wpqt.shape), _full(keys.shape),
                  _full(flat.shape)],
        out_specs=[_key_spec(nsub, rows, lambda i: (0, i, 0, 0)) for rows in KEY_TABLE_ROWS],
        out_shape=[jax.ShapeDtypeStruct((PEER_HEADS, n // LANES, rows, LANES), jnp.int32)
                   for rows in KEY_TABLE_ROWS],
        scratch_shapes=[pltpu.VMEM((nsub, PEER_HEADS * PEER_DK, LANES), BF16),
                        pltpu.VMEM((hc, nsub, N_KEYS, LANES), F32),
                        pltpu.VMEM((hc, nsub, N_KEYS, LANES), F32),
                        pltpu.VMEM((hc, nsub, PEER_TOPK, LANES), F32),
                        pltpu.VMEM((LEVEL2_CHAINS, CAND_ROWS, LANES), F32),
                        pltpu.VMEM((LEVEL2_CHAINS, CAND_ROWS, LANES), F32)],
        compiler_params=_params("parallel"),
        name="peer_topk",
    )(x2t, wpqt, keys, flat)


EXP_BLK = 1024
EXP_GROUPS = EXP_BLK // N_KEYS
SUBS_PER_PIECE = 2
EXP_STEPS = N_EXPERTS // (2 * EXP_BLK)


def _peer_experts_kernel(x2t_ref, x2_ref, rb_ref, na_ref, ea_ref, eb_ref, u_ref, vt_ref, ln3g_ref, ln3b_ref,
                         y_ref, acc_s, h0_s, h1_s, w0_s, w1_s):
    j = pl.program_id(1)
    nsub = x2t_ref.shape[0] * x2t_ref.shape[2] // LANES
    pieces = max(nsub // SUBS_PER_PIECE, 1)
    piece_subs = nsub // pieces
    d_rows = D_MODEL // pieces
    e_rows = EXP_BLK // pieces

    def key_row(ref, h, sub, ia):
        word = jnp.broadcast_to(ref[h, sub, pl.ds(ia, 1), :], (8, LANES))
        return jnp.tile(pltpu.bitcast(word, BF16), (N_KEYS // 16, 1))

    def gate_piece(h_s, w_s, blk, sub):
        for g in range(EXP_GROUPS):
            ia = blk * EXP_GROUPS + g
            gate = jnp.zeros((N_KEYS, LANES), BF16)
            for h in range(PEER_HEADS):
                eb = _unpacked(eb_ref[h, sub])
                keep = _unpacked(rb_ref[h, sub]) < key_row(na_ref, h, sub, ia)
                gate = gate + jnp.where(keep, eb, jnp.zeros_like(eb)) * key_row(ea_ref, h, sub, ia)
            act = _gelu_tanh(h_s[sub, g * N_KEYS:(g + 1) * N_KEYS, :]).astype(BF16) * gate
            w_s[sub, g * (N_KEYS // 2):(g + 1) * (N_KEYS // 2), :] = pltpu.bitcast(act, jnp.int32)

    def half_step(w_done, h_done, w_next, h_next, half, blk, *, down=True, gate=True, up=True):
        experts = slice(half * EXP_BLK, (half + 1) * EXP_BLK)

        def piece(i, carry):
            if up:
                u_rows = pl.ds(pl.multiple_of((half * EXP_BLK + i * e_rows) // 2, e_rows // 2), e_rows // 2)
                h_new = _dot(_unpacked(u_ref[u_rows, :]), _x2t_block(x2t_ref))
                for s in range(nsub):
                    h_next[s, pl.ds(pl.multiple_of(i * e_rows, e_rows), e_rows), :] = (
                        h_new[:, s * LANES:(s + 1) * LANES])
            if down:
                rows = pl.ds(pl.multiple_of(i * d_rows, d_rows), d_rows)
                w = jnp.concatenate([_unpacked(w_done[s]) for s in range(nsub)], axis=1)
                v_rows = pl.ds(pl.multiple_of(i * (d_rows // 2), d_rows // 2), d_rows // 2)
                acc_s[rows, :] += _dot(_unpacked(vt_ref[v_rows, experts]), w)
            if gate:
                for k in range(piece_subs):
                    gate_piece(h_done, w_next, blk, i * piece_subs + k)
            return carry

        lax.fori_loop(0, pieces, piece, 0)

    last = pl.num_programs(1) - 1

    @pl.when(j == 0)
    def _():
        acc_s[...] = jnp.zeros(acc_s.shape, F32)
        half_step(w0_s, h1_s, w1_s, h0_s, 0, 2 * j - 1, down=False, gate=False)
        half_step(w1_s, h0_s, w0_s, h1_s, 1, 2 * j, down=False)

    @pl.when((j > 0) & (j < last))
    def _():
        half_step(w0_s, h1_s, w1_s, h0_s, 0, 2 * j - 1)
        half_step(w1_s, h0_s, w0_s, h1_s, 1, 2 * j)

    @pl.when(j == last)
    def _():
        half_step(w0_s, h1_s, w1_s, h0_s, 0, 2 * j - 1, up=False)
        half_step(w1_s, h0_s, w0_s, h1_s, 1, 2 * j, gate=False, up=False)
        y_ref[...] = _layer_norm(ALPHA * x2_ref[...] + acc_s[...].T, ln3g_ref[...], ln3b_ref[...])


def _peer_experts(x2t, x2, key_arrs, u_bf, vt_bf, ln3g, ln3b, *, tt):
    n = x2.shape[0]
    nsub = tt // LANES
    return pl.pallas_call(
        _peer_experts_kernel,
        grid=(n // tt, EXP_STEPS + 1),
        in_specs=[_x2t_spec(x2t, tt, lambda i, j: (i, 0, 0)),
                  pl.BlockSpec((tt, D_MODEL), lambda i, j: (i, 0)),
                  *[_key_spec(nsub, rows, lambda i, j: (0, i, 0, 0)) for rows in KEY_TABLE_ROWS],
                  pl.BlockSpec((EXP_BLK, D_MODEL), lambda i, j: (jnp.minimum(j, EXP_STEPS - 1), 0)),
                  pl.BlockSpec((None, D_MODEL // 2, 2 * EXP_BLK), lambda i, j: (jnp.maximum(j - 1, 0), 0, 0)),
                  _full(ln3g.shape), _full(ln3b.shape)],
        out_specs=pl.BlockSpec((tt, D_MODEL), lambda i, j: (i, 0)),
        out_shape=jax.ShapeDtypeStruct((n, D_MODEL), F32),
        scratch_shapes=[pltpu.VMEM((D_MODEL, tt), F32),
                        pltpu.VMEM((nsub, EXP_BLK, LANES), F32), pltpu.VMEM((nsub, EXP_BLK, LANES), F32),
                        pltpu.VMEM((nsub, EXP_BLK // 2, LANES), jnp.int32),
                        pltpu.VMEM((nsub, EXP_BLK // 2, LANES), jnp.int32)],
        compiler_params=_params("parallel", "arbitrary"),
        name="peer_experts",
    )(x2t, x2, *key_arrs, u_bf, vt_bf, ln3g, ln3b)


def _peer(x2t, x2, peer_wts, *, tt):
    wpqt, keys, flat, u_bf, vt_bf, ln3g, ln3b = peer_wts
    key_arrs = _peer_topk(x2t, wpqt, keys, flat, tt=tt)
    return _peer_experts(x2t, x2, key_arrs, u_bf, vt_bf, ln3g, ln3b, tt=tt)


def _rope_tables(pos):
    inv = ROPE_THETA ** (-jnp.arange(0, QK_ROPE, 2, dtype=F32) / QK_ROPE)
    ang = pos.astype(F32)[:, None] * inv[None, :]
    cos, sin, zero = jnp.cos(ang), jnp.sin(ang), jnp.zeros_like(ang)
    pad = jnp.zeros((pos.shape[0], LANES - QK_ROPE), F32)
    return (jnp.concatenate([cos, cos, pad], axis=1),
            jnp.concatenate([-sin, zero, pad], axis=1),
            jnp.concatenate([zero, sin, pad], axis=1))


def _row(v):
    return v.reshape(1, -1).astype(F32)


def _pack_experts_kernel(u_ref, v_ref, up_ref, vtp_ref):
    up_ref[...] = _packed(u_ref[...])
    vtp_ref[...] = _packed(v_ref[...].T)


def _pack_experts(peer_u, peer_v):
    blk = 2 * EXP_BLK
    return pl.pallas_call(
        _pack_experts_kernel,
        grid=(EXP_STEPS,),
        in_specs=[pl.BlockSpec((blk, D_MODEL), lambda i: (i, 0))] * 2,
        out_specs=[pl.BlockSpec((blk // 2, D_MODEL), lambda i: (i, 0)),
                   pl.BlockSpec((None, D_MODEL // 2, blk), lambda i: (i, 0, 0))],
        out_shape=[jax.ShapeDtypeStruct((N_EXPERTS // 2, D_MODEL), jnp.int32),
                   jax.ShapeDtypeStruct((EXP_STEPS, D_MODEL // 2, blk), jnp.int32)],
        compiler_params=_params("parallel"),
        name="pack_experts",
    )(peer_u, peer_v)


def kernel(x_prompt, x_sample, mem_prompt, cache_ckv, cache_krope, cache_mem_k, cache_mem_v, page_table,
           w_in, q_norm_g, kv_norm_g, w_uq, w_uk, w_uv, gm_norm_g, gm_norm_b, gm_ws, gm_bs, attn_out_g,
           gm_out_g, w_out, ln1_g, ln1_b, w_xq, w_mk, w_mv, w_xo, ln2_g, ln2_b, w_pq, peer_keys, peer_u,
           peer_v, ln3_g, ln3_b):
    batch, seq = x_prompt.shape[:2]
    nb = x_sample.shape[0]
    past_len = page_table.shape[1] * PAGE_SIZE

    kr_pad = jnp.zeros((D_MODEL, LANES - QK_ROPE), F32)
    w_in_x = jnp.concatenate([w_in[:, :Q_RANK + KV_RANK + QK_ROPE], kr_pad,
                              w_in[:, Q_RANK + KV_RANK + QK_ROPE:]], axis=1).astype(BF16)
    uq_nope = w_uq[:, :, :QK_NOPE].reshape(Q_RANK, MLA_HEADS * QK_NOPE)
    uq_rope = jnp.pad(w_uq[:, :, QK_NOPE:], ((0, 0), (0, 0), (0, LANES - QK_ROPE)))
    w_uq_x = jnp.concatenate([uq_nope, uq_rope.reshape(Q_RANK, MLA_HEADS * LANES)], axis=1).astype(BF16)
    eye = jnp.eye(MLA_HEADS, dtype=F32)
    w_uk_bd = jnp.einsum('rhd,hg->hdgr', w_uk, eye).reshape(MLA_HEADS * QK_NOPE, MLA_HEADS * KV_RANK).astype(BF16)
    w_uv_bd = jnp.einsum('rhd,hg->hrgd', w_uv, eye).reshape(MLA_HEADS * KV_RANK, MLA_WIDTH).astype(BF16)
    w_uv_t = jnp.transpose(w_uv, (1, 2, 0)).astype(BF16)
    proj_wts = (w_in_x, _row(q_norm_g), _row(kv_norm_g), w_uq_x, w_uk_bd, _row(gm_norm_g), _row(gm_norm_b))
    g_attn_col = jnp.broadcast_to(attn_out_g.astype(F32)[:, None], (MLA_WIDTH, Q_BLK))
    bias_tile = jnp.repeat(gm_bs.T, GM_WIDTH // GM_GROUPS, axis=1).astype(F32)
    w_out_bf, w_xq_bf = w_out.astype(BF16), w_xq.reshape(D_MODEL, X_WIDTH).astype(BF16)
    w_xo_bf = w_xo.reshape(X_WIDTH, D_MODEL).astype(BF16)
    post_tail = (w_xo_bf, _row(ln2_g), _row(ln2_b))
    post_wts = (gm_ws.astype(F32), bias_tile, _row(gm_out_g), w_out_bf, _row(ln1_g), _row(ln1_b), w_xq_bf) + post_tail
    ws0 = jnp.repeat(gm_ws[:, 0, 0], GM_WIDTH // GM_GROUPS)
    bs0 = jnp.repeat(gm_bs[:, 0], GM_WIDTH // GM_GROUPS)
    sample_wts = (w_uv_bd, _row(attn_out_g), _row(ws0), _row(bs0), _row(gm_out_g), w_out_bf, _row(ln1_g),
                  _row(ln1_b), w_xq_bf) + post_tail
    flat = np.full((CAND_ROWS,), 1e8, np.float32)
    for ka in range(PEER_TOPK):
        flat[CAND_OFF[ka]:CAND_OFF[ka] + CAND_NB[ka]] = ka * PEER_TOPK + np.arange(CAND_NB[ka])
    flat = jnp.asarray(np.broadcast_to(flat[:, None], (CAND_ROWS, LANES)))
    peer_wts = (w_pq.reshape(D_MODEL, PEER_HEADS * PEER_DK).T.astype(BF16),
                peer_keys.reshape(2 * PEER_HEADS, N_KEYS, PEER_HALF).astype(BF16), flat,
                *_pack_experts(peer_u, peer_v), _row(ln3_g), _row(ln3_b))

    n_p = batch * seq
    xp = x_prompt.reshape(n_p, D_MODEL)
    tm = 512
    qlat, qrope, kcat, ckv_p, krope_p, u_p, v_p, ckvt = _proj(
        xp, _rope_tables(jnp.arange(seq)), proj_wts, tm=tm, seq_blocks=seq // tm, emit_kt=True)
    a_p = _attn_prompt(qlat, qrope, kcat, ckvt, w_uv_t, g_attn_col, batch=batch, seq=seq)
    mk_p, mv_p = _memkv(mem_prompt.reshape(batch * MEM_TOKENS, D_MODEL),
                        w_mk.reshape(D_MODEL, X_WIDTH).astype(BF16), w_mv.reshape(D_MODEL, X_WIDTH).astype(BF16))
    x2_p, x2t_p = _post_prompt(xp, a_p, u_p, v_p, mk_p, mv_p, post_wts, batch=batch, seq=seq)
    y_p = _peer(x2t_p, x2_p, peer_wts, tt=512)

    xs = x_sample.reshape(nb, D_MODEL)
    pos_s = jnp.full((nb,), past_len, jnp.int32)
    qlat_s, qrope_s, _, ckv_s, krope_s, u_s, v_s = _proj(xs, _rope_tables(pos_s), proj_wts, tm=nb, seq_blocks=1,
                                                          emit_kt=False)
    q_cat = jnp.concatenate([qlat_s.reshape(nb, MLA_HEADS, KV_RANK), qrope_s.reshape(nb, MLA_HEADS, LANES)], axis=2)
    q_pad = jnp.pad(q_cat, ((0, 0), (0, LANES - MLA_HEADS), (0, 0)))
    o_lat_s = _attn_decode(page_table, q_pad, cache_ckv, cache_krope, ckv_s, krope_s)
    x2_s = _post_sample(xs, o_lat_s.reshape(nb, MLA_HEADS * KV_RANK), u_s, v_s,
                        cache_mem_k.reshape(nb, MEM_TOKENS, X_WIDTH), cache_mem_v.reshape(nb, MEM_TOKENS, X_WIDTH),
                        sample_wts)
    y_s = _peer(x2_s.T.astype(BF16).reshape(1, D_MODEL, nb), x2_s, peer_wts, tt=nb)

    return (y_p.reshape(batch, seq, D_MODEL), y_s.reshape(nb, 1, D_MODEL),
            ckv_p.reshape(batch, seq, KV_RANK), krope_p.reshape(batch, seq, QK_ROPE),
            mk_p.reshape(batch, MEM_TOKENS, X_HEADS, X_HEAD_DIM), mv_p.reshape(batch, MEM_TOKENS, X_HEADS, X_HEAD_DIM),
            ckv_s.reshape(nb, 1, KV_RANK), krope_s.reshape(nb, 1, QK_ROPE), v_s.reshape(nb, 1, GM_WIDTH))
```

```python
import functools

import jax
import jax.numpy as jnp
import numpy as np
from jax import lax
from jax.experimental import pallas as pl
from jax.experimental.pallas import tpu as pltpu

F32 = jnp.float32
BF16 = jnp.bfloat16

D_MODEL = 1024
MLA_HEADS = 8
QK_NOPE = 64
QK_ROPE = 32
V_HEAD = 64
Q_RANK = 256
KV_RANK = 128
MLA_WIDTH = MLA_HEADS * V_HEAD
MLA_SCALE = (QK_NOPE + QK_ROPE) ** -0.5
Q_SCALE = float(MLA_SCALE * np.log2(np.e))
ROPE_THETA = 10000.0
GM_WIDTH = D_MODEL // 2
GM_GROUPS = 4
GM_CHUNK = 128
MEM_TOKENS = 256
X_HEADS = 4
X_HEAD_DIM = 128
X_WIDTH = X_HEADS * X_HEAD_DIM
X_SCALE = X_HEAD_DIM ** -0.5
PEER_HEADS = 8
N_KEYS = 128
N_EXPERTS = N_KEYS * N_KEYS
PEER_TOPK = 16
PEER_DK = 256
PEER_HALF = PEER_DK // 2
PAGE_SIZE = 128
DEPTH = 1
ALPHA = (2.0 * DEPTH) ** 0.25
EPS = 1e-5

LANES = 128
VMEM_LIMIT = 56 * 1024 * 1024

CAND_NB = tuple(PEER_TOPK // (ka + 1) for ka in range(PEER_TOPK))
CAND_OFF = tuple(int(sum(CAND_NB[:ka])) for ka in range(PEER_TOPK))
CAND_N = int(sum(CAND_NB))
CAND_ROWS = 56
LEVEL1_CHAINS = 16
LEVEL2_CHAINS = 8
NEG_INF = float("-inf")


def _dot(a, b):
    return jnp.dot(a, b, preferred_element_type=F32)


def _dot_nt(a, b):
    return lax.dot_general(a, b, (((1,), (1,)), ((), ())), preferred_element_type=F32)


def _dot_tn(a, b):
    return lax.dot_general(a, b, (((0,), (0,)), ((), ())), preferred_element_type=F32)


def _layer_norm(x, g, b):
    mu = jnp.mean(x, -1, keepdims=True)
    var = jnp.mean(jnp.square(x - mu), -1, keepdims=True)
    return (x - mu) * lax.rsqrt(var + EPS) * g + b


def _rms_norm(x, g):
    return x * lax.rsqrt(jnp.mean(jnp.square(x), -1, keepdims=True) + EPS) * g


_GELU_K1 = float(-2.0 * np.sqrt(2.0 / np.pi) * np.log2(np.e))
_GELU_K2 = float(0.044715 * _GELU_K1)


def _gelu_tanh(x):
    return x / (1.0 + jnp.exp2(x * (_GELU_K1 + _GELU_K2 * (x * x))))


def _params(*sem):
    return pltpu.CompilerParams(dimension_semantics=sem, vmem_limit_bytes=VMEM_LIMIT)


def _full(shape):
    n = len(shape)
    return pl.BlockSpec(shape, lambda *_: (0,) * n)


def _rope(x, c, s_lo, s_hi):
    width = x.shape[-1]
    return x * c + pltpu.roll(x, width - 16, 1) * s_lo + pltpu.roll(x, 16, 1) * s_hi


def _proj_kernel(x_ref, c_ref, slo_ref, shi_ref, w_in_ref, qg_ref, kvg_ref, w_uq_ref, w_uk_ref,
                 gmg_ref, gmb_ref, qlat_ref, qrope_ref, kcat_ref, ckv_ref, krope_ref, u_ref, v_ref,
                 *maybe_ckvt_ref):
    h = _dot(x_ref[...].astype(BF16), w_in_ref[...])
    c, s_lo, s_hi = c_ref[...], slo_ref[...], shi_ref[...]

    cq = _rms_norm(h[:, :Q_RANK], qg_ref[...])
    q_all = _dot(cq.astype(BF16), w_uq_ref[...])
    q_nope = q_all[:, :MLA_HEADS * QK_NOPE]
    q_lat = _dot(q_nope.astype(BF16), w_uk_ref[...])
    qlat_ref[...] = (q_lat * Q_SCALE).astype(BF16)
    q_rope = _rope(q_all[:, MLA_HEADS * QK_NOPE:], jnp.tile(c, (1, MLA_HEADS)),
                   jnp.tile(s_lo, (1, MLA_HEADS)), jnp.tile(s_hi, (1, MLA_HEADS)))
    qrope_ref[...] = (q_rope * Q_SCALE).astype(BF16)

    ckv = _rms_norm(h[:, Q_RANK:Q_RANK + KV_RANK], kvg_ref[...])
    ckv_ref[...] = ckv
    k_rot = _rope(h[:, Q_RANK + KV_RANK:Q_RANK + KV_RANK + LANES], c, s_lo, s_hi)
    krope_ref[...] = k_rot[:, :QK_ROPE]
    kcat_ref[...] = jnp.concatenate([ckv, k_rot], axis=1).astype(BF16)
    for ckvt_ref in maybe_ckvt_ref:
        for j in range(ckvt_ref.shape[0]):
            ckvt_ref[j] = ckv[j * 256:(j + 1) * 256, :].T.astype(BF16)

    uv = jax.nn.gelu(h[:, Q_RANK + KV_RANK + LANES:])
    u_ref[...] = uv[:, :GM_WIDTH]
    v_ref[...] = _layer_norm(uv[:, GM_WIDTH:], gmg_ref[...], gmb_ref[...])


def _proj(x, tables, wts, *, tm, seq_blocks, emit_kt):
    n = x.shape[0]
    row = lambda w: pl.BlockSpec((tm, w), lambda i: (i, 0))
    tab = pl.BlockSpec((tm, LANES), lambda i: (i % seq_blocks, 0))
    w_in, qg, kvg, w_uq, w_uk, gmg, gmb = wts
    out_shape = [
        jax.ShapeDtypeStruct((n, MLA_HEADS * KV_RANK), BF16),
        jax.ShapeDtypeStruct((n, MLA_HEADS * LANES), BF16),
        jax.ShapeDtypeStruct((n, 2 * LANES), BF16),
        jax.ShapeDtypeStruct((n, KV_RANK), F32),
        jax.ShapeDtypeStruct((n, QK_ROPE), F32),
        jax.ShapeDtypeStruct((n, GM_WIDTH), F32),
        jax.ShapeDtypeStruct((n, GM_WIDTH), F32),
    ]
    out_specs = [row(MLA_HEADS * KV_RANK), row(MLA_HEADS * LANES), row(2 * LANES), row(KV_RANK),
                 row(QK_ROPE), row(GM_WIDTH), row(GM_WIDTH)]
    if emit_kt:
        out_shape.append(jax.ShapeDtypeStruct((n // 256, KV_RANK, 256), BF16))
        out_specs.append(pl.BlockSpec((tm // 256, KV_RANK, 256), lambda i: (i, 0, 0)))
    return pl.pallas_call(
        _proj_kernel,
        grid=(n // tm,),
        in_specs=[row(D_MODEL), tab, tab, tab, _full(w_in.shape), _full(qg.shape), _full(kvg.shape),
                  _full(w_uq.shape), _full(w_uk.shape), _full(gmg.shape), _full(gmb.shape)],
        out_specs=out_specs,
        out_shape=out_shape,
        compiler_params=_params("parallel"),
        name="proj",
    )(x, *tables, w_in, qg, kvg, w_uq, w_uk, gmg, gmb)


Q_BLK = 256
KV_BLK = 256


ATTN_COLS = MLA_HEADS * Q_BLK


def _attn_kernel(qlat_ref, qrope_ref, kcat_ref, ckvt_ref, wuvt_ref, g_ref, a_ref, q_s, m_s, l_s, acc_s):
    qi = pl.program_id(1)
    ql, qr = qlat_ref[...], qrope_ref[...]
    hpc = ATTN_COLS // Q_BLK
    for h in range(MLA_HEADS):
        q_s[h // hpc, (h % hpc) * Q_BLK:(h % hpc + 1) * Q_BLK, :] = jnp.concatenate(
            [ql[:, h * LANES:(h + 1) * LANES], qr[:, h * LANES:(h + 1) * LANES]], axis=1)
    m_s[...] = jnp.full(m_s.shape, NEG_INF, F32)
    l_s[...] = jnp.zeros(l_s.shape, F32)
    acc_s[...] = jnp.zeros(acc_s.shape, F32)
    q_pos = qi * Q_BLK + (lax.broadcasted_iota(jnp.int32, (KV_BLK, ATTN_COLS), 1) & (Q_BLK - 1))
    k_off = lax.broadcasted_iota(jnp.int32, (KV_BLK, ATTN_COLS), 0)

    def scores(j, c):
        k = kcat_ref[pl.ds(pl.multiple_of(j * KV_BLK, KV_BLK), KV_BLK), :]
        return _dot_nt(k, q_s[c])

    def update(j, c, st, diagonal):
        if diagonal:
            st = jnp.where(q_pos >= k_off + j * KV_BLK, st, NEG_INF)
        m_old = m_s[c]
        m_new = jnp.maximum(m_old, jnp.max(st, axis=0, keepdims=True))
        alpha = jnp.exp2(m_old - m_new)
        p = jnp.exp2(st - m_new)
        l_s[c] = alpha * l_s[c] + jnp.sum(p, axis=0, keepdims=True)
        acc_s[c] = alpha * acc_s[c] + _dot(ckvt_ref[j], p.astype(BF16))
        m_s[c] = m_new

    def step(j, diagonal):
        for c in range(MLA_HEADS // hpc):
            update(j, c, scores(j, c), diagonal)

    last = (qi * Q_BLK) // KV_BLK

    def visible_pair(t, carry):
        for c in range(MLA_HEADS // hpc):
            st_a, st_b = scores(2 * t, c), scores(2 * t + 1, c)
            update(2 * t, c, st_a, False)
            update(2 * t + 1, c, st_b, False)
        return carry

    lax.fori_loop(0, last // 2, visible_pair, 0)

    @pl.when(last % 2 == 1)
    def _():
        step(last - 1, False)

    for d in range(Q_BLK // KV_BLK):
        step(last + d, True)

    o_t = (acc_s[...] / l_s[...]).astype(BF16)
    om_t = jnp.concatenate(
        [_dot(wuvt_ref[h], o_t[h // hpc, :, (h % hpc) * Q_BLK:(h % hpc + 1) * Q_BLK])
         for h in range(MLA_HEADS)], axis=0)
    ms = jnp.mean(jnp.square(om_t), axis=0, keepdims=True)
    a_t = om_t * lax.rsqrt(ms + EPS) * g_ref[...]
    a_ref[...] = a_t.T.astype(BF16)


def _attn_prompt(qlat, qrope, kcat, ckvt, wuvt, g_attn, *, batch, seq):
    n = batch * seq
    nq = seq // Q_BLK
    chains = MLA_HEADS * Q_BLK // ATTN_COLS
    return pl.pallas_call(
        _attn_kernel,
        grid=(batch, nq),
        in_specs=[
            pl.BlockSpec((Q_BLK, MLA_HEADS * KV_RANK), lambda b, i: (b * nq + i, 0)),
            pl.BlockSpec((Q_BLK, MLA_HEADS * LANES), lambda b, i: (b * nq + i, 0)),
            pl.BlockSpec((None, seq, 2 * LANES), lambda b, i: (b, 0, 0)),
            pl.BlockSpec((None, seq // KV_BLK, KV_RANK, KV_BLK), lambda b, i: (b, 0, 0, 0)),
            _full(wuvt.shape), _full(g_attn.shape),
        ],
        out_specs=pl.BlockSpec((Q_BLK, MLA_WIDTH), lambda b, i: (b * nq + i, 0)),
        out_shape=jax.ShapeDtypeStruct((n, MLA_WIDTH), BF16),
        scratch_shapes=[pltpu.VMEM((chains, ATTN_COLS, 2 * LANES), BF16), pltpu.VMEM((chains, 1, ATTN_COLS), F32),
                        pltpu.VMEM((chains, 1, ATTN_COLS), F32), pltpu.VMEM((chains, KV_RANK, ATTN_COLS), F32)],
        compiler_params=_params("parallel", "arbitrary"),
        name="attn_prompt",
    )(qlat, qrope, kcat.reshape(batch, seq, 2 * LANES),
      ckvt.reshape(batch, seq // KV_BLK, KV_RANK, KV_BLK), wuvt, g_attn)


PAGES_PER_STEP = 16


SEQS_PER_STEP = 4


def _decode_kernel(pt_ref, q_ref, *refs):
    del pt_ref
    n_pg = SEQS_PER_STEP * PAGES_PER_STEP
    ckv_refs, kr_refs = refs[:n_pg], refs[n_pg:2 * n_pg]
    ckvn_ref, krn_ref, o_ref, m_s, l_s, acc_s = refs[2 * n_pg:]
    c = pl.program_id(1)

    @pl.when(c == 0)
    def _():
        m_s[...] = jnp.full(m_s.shape, NEG_INF, F32)
        l_s[...] = jnp.zeros(l_s.shape, F32)
        acc_s[...] = jnp.zeros(acc_s.shape, F32)

    def keys(ckv, kr_wide):
        return jnp.concatenate([ckv, kr_wide], axis=1).astype(BF16)

    def rotary_page(kr_t):
        return jnp.concatenate([kr_t, jnp.zeros((LANES - QK_ROPE, PAGE_SIZE), F32)], axis=0).T

    def update(b, kcat, valid_rows):
        half = kcat.shape[0] // 2
        if valid_rows is None:
            st = jnp.concatenate([_dot_nt(kcat[:half], q_ref[b]), _dot_nt(kcat[half:], q_ref[b])], axis=0)
        else:
            st = _dot_nt(kcat, q_ref[b])
        if valid_rows is not None:
            st = jnp.where(lax.broadcasted_iota(jnp.int32, st.shape, 0) < valid_rows, st, NEG_INF)
        m_old = m_s[b]
        m_new = jnp.maximum(m_old, jnp.max(st, axis=0, keepdims=True))
        alpha = jnp.exp2(m_old - m_new)
        p = jnp.exp2(st - m_new)
        l_s[b] = alpha * l_s[b] + jnp.sum(p, axis=0, keepdims=True)
        acc_s[b] = alpha * acc_s[b] + _dot_tn(kcat[:, :KV_RANK], p.astype(BF16))
        m_s[b] = m_new

    for b in range(SEQS_PER_STEP):
        pages = slice(b * PAGES_PER_STEP, (b + 1) * PAGES_PER_STEP)
        update(b, keys(jnp.concatenate([r[...] for r in ckv_refs[pages]], axis=0),
                       jnp.concatenate([rotary_page(r[...]) for r in kr_refs[pages]], axis=0)), None)

    @pl.when(c == pl.num_programs(1) - 1)
    def _():
        for b in range(SEQS_PER_STEP):
            krn = jnp.concatenate([krn_ref[b], jnp.zeros((1, LANES - QK_ROPE), F32)], axis=1)
            update(b, keys(jnp.broadcast_to(ckvn_ref[b], (16, KV_RANK)), jnp.broadcast_to(krn, (16, LANES))), 1)
            o_t = acc_s[b] / l_s[b]
            o_ref[b] = o_t.T[:MLA_HEADS, :]


def _attn_decode(page_table, q_pad, cache_ckv, cache_krope, ckv_new, krope_new):
    nb, n_pages = page_table.shape
    steps = n_pages // PAGES_PER_STEP
    sq = SEQS_PER_STEP

    def page_spec(rows, width, b, i):
        return pl.BlockSpec((None, rows, width),
                            lambda g, c, pt: (pt[g * sq + b, c * PAGES_PER_STEP + i], 0, 0))

    def seq_spec(*dims):
        return pl.BlockSpec((sq,) + dims, lambda g, c, pt: (g,) + (0,) * len(dims))

    in_specs = [seq_spec(LANES, 2 * LANES)]
    in_specs += [page_spec(PAGE_SIZE, KV_RANK, b, i) for b in range(sq) for i in range(PAGES_PER_STEP)]
    in_specs += [page_spec(QK_ROPE, PAGE_SIZE, b, i) for b in range(sq) for i in range(PAGES_PER_STEP)]
    in_specs += [seq_spec(1, KV_RANK), seq_spec(1, QK_ROPE)]
    n_pg = sq * PAGES_PER_STEP
    return pl.pallas_call(
        _decode_kernel,
        grid_spec=pltpu.PrefetchScalarGridSpec(
            num_scalar_prefetch=1,
            grid=(nb // sq, steps),
            in_specs=in_specs,
            out_specs=seq_spec(MLA_HEADS, KV_RANK),
            scratch_shapes=[pltpu.VMEM((sq, 1, LANES), F32), pltpu.VMEM((sq, 1, LANES), F32),
                            pltpu.VMEM((sq, KV_RANK, LANES), F32)],
        ),
        out_shape=jax.ShapeDtypeStruct((nb, MLA_HEADS, KV_RANK), F32),
        compiler_params=_params("parallel", "arbitrary"),
        name="attn_decode",
    )(page_table, q_pad, *([cache_ckv] * n_pg), *([jnp.swapaxes(cache_krope, 1, 2)] * n_pg),
      ckv_new.reshape(nb, 1, KV_RANK), krope_new.reshape(nb, 1, QK_ROPE))


def _memkv_kernel(mem_ref, wk_ref, wv_ref, mk_ref, mv_ref):
    m = mem_ref[...].astype(BF16)
    mk_ref[...] = _dot(m, wk_ref[...])
    mv_ref[...] = _dot(m, wv_ref[...])


def _memkv(mem, w_mk, w_mv):
    n = mem.shape[0]
    tm = 512
    return pl.pallas_call(
        _memkv_kernel,
        grid=(n // tm,),
        in_specs=[pl.BlockSpec((tm, D_MODEL), lambda i: (i, 0)), _full(w_mk.shape), _full(w_mv.shape)],
        out_specs=[pl.BlockSpec((tm, X_WIDTH), lambda i: (i, 0))] * 2,
        out_shape=[jax.ShapeDtypeStruct((n, X_WIDTH), F32)] * 2,
        compiler_params=_params("parallel"),
        name="memkv",
    )(mem, w_mk, w_mv)


POST_ROWS = 512


def _softmax_rows(s):
    e = jnp.exp(s - jnp.max(s, axis=-1, keepdims=True))
    return e / jnp.sum(e, axis=-1, keepdims=True)


def _mix_and_ln1(x, a_bf, o_gm, gmog, w_out, ln1g, ln1b):
    gm_n = _rms_norm(o_gm, gmog)
    y = jnp.concatenate([a_bf, gm_n.astype(BF16)], axis=1)
    return _layer_norm(ALPHA * x + _dot(y, w_out), ln1g, ln1b)


def _post_prompt_kernel(x_ref, a_ref, u_ref, v_ref, ws_ref, bias_ref, gmog_ref, wout_ref, ln1g_ref, ln1b_ref,
                        wxq_ref, mk_ref, mv_ref, wxo_ref, ln2g_ref, ln2b_ref, x2_ref, x2t_ref):
    tril = (lax.broadcasted_iota(jnp.int32, (GM_CHUNK, GM_CHUNK), 0)
            >= lax.broadcasted_iota(jnp.int32, (GM_CHUNK, GM_CHUNK), 1))
    w_s = [jnp.where(tril, ws_ref[g], 0.0).astype(BF16) for g in range(GM_GROUPS)]
    chunks = []
    for c in range(POST_ROWS // GM_CHUNK):
        rows = slice(c * GM_CHUNK, (c + 1) * GM_CHUNK)
        v_c = v_ref[rows, :].astype(BF16)
        s = jnp.concatenate([_dot(w_s[g], v_c[:, g * LANES:(g + 1) * LANES]) for g in range(GM_GROUPS)],
                            axis=1) + bias_ref[...]
        chunks.append(u_ref[rows, :] * s)
    o_gm = jnp.concatenate(chunks, axis=0)
    x1 = _mix_and_ln1(x_ref[...], a_ref[...], o_gm, gmog_ref[...], wout_ref[...], ln1g_ref[...], ln1b_ref[...])

    q = _dot(x1.astype(BF16), wxq_ref[...]).astype(BF16)
    mk, mv = mk_ref[...].astype(BF16), mv_ref[...].astype(BF16)
    heads = []
    for h in range(X_HEADS):
        cs = slice(h * X_HEAD_DIM, (h + 1) * X_HEAD_DIM)
        p = _softmax_rows(_dot_nt(q[:, cs], mk[:, cs]) * X_SCALE)
        heads.append(_dot(p.astype(BF16), mv[:, cs]))
    o = jnp.concatenate(heads, axis=1).astype(BF16)
    x2 = _layer_norm(ALPHA * x1 + _dot(o, wxo_ref[...]), ln2g_ref[...], ln2b_ref[...])
    x2_ref[...] = x2
    x2t_ref[...] = x2.T.astype(BF16)


def _post_prompt(x, a, u, v, mk, mv, wts, *, batch, seq):
    n = batch * seq
    nb = seq // POST_ROWS
    row = lambda w: pl.BlockSpec((POST_ROWS, w), lambda b, i: (b * nb + i, 0))
    mem = pl.BlockSpec((MEM_TOKENS, X_WIDTH), lambda b, i: (b, 0))
    return pl.pallas_call(
        _post_prompt_kernel,
        grid=(batch, nb),
        in_specs=[row(D_MODEL), row(MLA_WIDTH), row(GM_WIDTH), row(GM_WIDTH)]
                 + [_full(w.shape) for w in wts[:6]] + [_full(wts[6].shape), mem, mem]
                 + [_full(w.shape) for w in wts[7:]],
        out_specs=[row(D_MODEL), pl.BlockSpec((None, D_MODEL, POST_ROWS), lambda b, i: (b * nb + i, 0, 0))],
        out_shape=[jax.ShapeDtypeStruct((n, D_MODEL), F32),
                   jax.ShapeDtypeStruct((n // POST_ROWS, D_MODEL, POST_ROWS), BF16)],
        compiler_params=_params("parallel", "parallel"),
        name="post_prompt",
    )(x, a, u, v, *wts[:7], mk, mv, *wts[7:])


SAMPLE_ROWS = 8


def _post_sample_kernel(x_ref, o_ref, u_ref, v_ref, wuv_ref, ag_ref, ws0_ref, bs0_ref, gmog_ref, wout_ref,
                        ln1g_ref, ln1b_ref, wxq_ref, mk_ref, mv_ref, wxo_ref, ln2g_ref, ln2b_ref, x2_ref):
    o_mla = _dot(o_ref[...].astype(BF16), wuv_ref[...])
    a = _rms_norm(o_mla, ag_ref[...]).astype(BF16)
    o_gm = u_ref[...] * (ws0_ref[...] * v_ref[...] + bs0_ref[...])
    x1 = _mix_and_ln1(x_ref[...], a, o_gm, gmog_ref[...], wout_ref[...], ln1g_ref[...], ln1b_ref[...])

    q = _dot(x1.astype(BF16), wxq_ref[...])
    lane_head = lax.broadcasted_iota(jnp.int32, (LANES, X_WIDTH), 1) // X_HEAD_DIM
    on_head = lane_head == lax.broadcasted_iota(jnp.int32, (LANES, X_WIDTH), 0)
    rows = []
    for j in range(SAMPLE_ROWS):
        q_bd = jnp.where(on_head, q[j:j + 1, :], 0.0).astype(BF16)
        s = _dot_nt(mk_ref[j].astype(BF16), q_bd) * X_SCALE
        e = jnp.exp(s - jnp.max(s, axis=0, keepdims=True))
        p = e / jnp.sum(e, axis=0, keepdims=True)
        o_all = _dot_tn(p.astype(BF16), mv_ref[j].astype(BF16))
        rows.append(jnp.sum(jnp.where(on_head, o_all, 0.0), axis=0, keepdims=True))
    o = jnp.concatenate(rows, axis=0).astype(BF16)
    x2_ref[...] = _layer_norm(ALPHA * x1 + _dot(o, wxo_ref[...]), ln2g_ref[...], ln2b_ref[...])


def _post_sample(x, o_lat, u, v, mk, mv, wts):
    n = x.shape[0]
    row = lambda w: pl.BlockSpec((SAMPLE_ROWS, w), lambda i: (i, 0))
    mem = pl.BlockSpec((SAMPLE_ROWS, MEM_TOKENS, X_WIDTH), lambda i: (i, 0, 0))
    return pl.pallas_call(
        _post_sample_kernel,
        grid=(n // SAMPLE_ROWS,),
        in_specs=[row(D_MODEL), row(MLA_HEADS * KV_RANK), row(GM_WIDTH), row(GM_WIDTH)]
                 + [_full(w.shape) for w in wts[:9]] + [mem, mem] + [_full(w.shape) for w in wts[9:]],
        out_specs=row(D_MODEL),
        out_shape=jax.ShapeDtypeStruct((n, D_MODEL), F32),
        compiler_params=_params("parallel"),
        name="post_sample",
    )(x, o_lat, u, v, *wts[:9], mk, mv, *wts[9:])


def _top16(val, row_id, *, break_ties):
    rank = jnp.full(val.shape, 127.0, F32)
    tops = []
    for k in range(PEER_TOPK):
        m = jnp.max(val, axis=0, keepdims=True)
        hit = val == m
        if break_ties:
            hit = row_id == jnp.min(jnp.where(hit, row_id, 1e9), axis=0, keepdims=True)
        val = jnp.where(hit, NEG_INF, val)
        rank = jnp.where(hit, float(k), rank)
        tops.append(m)
    return tops, rank


def _tied(rank):
    marked = jnp.sum(jnp.where(rank < float(PEER_TOPK), 1.0, 0.0), axis=0, keepdims=True)
    return marked - float(PEER_TOPK)


def _packed(x):
    return pltpu.bitcast(x.astype(BF16), jnp.int32)


def _unpacked(w):
    return pltpu.bitcast(w, BF16)


def _bf16_pair(x):
    u = pltpu.bitcast(x, jnp.int32)
    hi = lax.shift_right_logical(u + 0x7FFF + (lax.shift_right_logical(u, 16) & 1), 16)
    return hi | lax.shift_left(hi, 16)


def _small_int_pair(x):
    u = pltpu.bitcast(x, jnp.int32)
    return u | lax.shift_right_logical(u, 16)


def _peer_topk_kernel(x2t_ref, wpqt_ref, keys_ref, flat_ref, rb_ref, na_ref, ea_ref, eb_ref,
                      qt_s, s_s, rank_s, top_s, cand_s, sel_s):
    nsub = x2t_ref.shape[0] * x2t_ref.shape[2] // LANES
    qt = _dot(wpqt_ref[...], _x2t_block(x2t_ref)).astype(BF16)
    for sub in range(nsub):
        qt_s[sub] = qt[:, sub * LANES:(sub + 1) * LANES]
    key_id = lax.broadcasted_iota(jnp.int32, (N_KEYS, LANES), 0).astype(F32)
    flat = flat_ref[...]

    def keep_level1(hc, sub, tops, rank):
        rank_s[hc, sub] = rank
        for k in range(PEER_TOPK):
            top_s[hc, sub, k:k + 1, :] = tops[k]

    def level2(h, sub, u):
        sa, sb = top_s[2 * h, sub], top_s[2 * h + 1, sub]
        ea_r = jnp.exp(sa - sa[0:1, :])
        eb_r = jnp.exp(sb - sb[0:1, :])
        for ka in range(PEER_TOPK):
            cand_s[u, CAND_OFF[ka]:CAND_OFF[ka] + CAND_NB[ka], :] = sa[ka:ka + 1, :] + sb[0:CAND_NB[ka], :]
        cand_s[u, CAND_N:CAND_ROWS, :] = jnp.full((CAND_ROWS - CAND_N, LANES), NEG_INF, F32)
        _, crank = _top16(cand_s[u], flat, break_ties=True)
        sel_s[u] = jnp.where(crank < float(PEER_TOPK), 1.0, 0.0)
        n_a, z = [], jnp.zeros((1, LANES), F32)
        for ka in range(PEER_TOPK):
            sel_ka = sel_s[u, CAND_OFF[ka]:CAND_OFF[ka] + CAND_NB[ka], :]
            n_a.append(jnp.sum(sel_ka, axis=0, keepdims=True))
            z = z + ea_r[ka:ka + 1, :] * jnp.sum(sel_ka * eb_r[0:CAND_NB[ka], :], axis=0, keepdims=True)
        rank_a = rank_s[2 * h, sub]
        na = jnp.zeros((N_KEYS, LANES), F32)
        for ka in range(PEER_TOPK):
            na = jnp.where(rank_a == float(ka), n_a[ka], na)
        na_ref[h, sub] = _small_int_pair(na)
        rb_ref[h, sub] = _packed(rank_s[2 * h + 1, sub])
        ea_ref[h, sub] = _bf16_pair(jnp.exp(s_s[2 * h, sub] - sa[0:1, :]))
        eb_ref[h, sub] = _packed(jnp.exp(s_s[2 * h + 1, sub] - sb[0:1, :]) / z)

    def per_subtile(sub, carry):
        def quick(h, tied):
            for hc in [LEVEL1_CHAINS * h + u for u in range(LEVEL1_CHAINS)]:
                q_blk = qt_s[sub, pl.ds(pl.multiple_of(hc * PEER_HALF, PEER_HALF), PEER_HALF), :]
                s = _dot(keys_ref[hc], q_blk)
                s_s[hc, sub] = s
                tops, rank = _top16(s, key_id, break_ties=False)
                keep_level1(hc, sub, tops, rank)
                tied = jnp.maximum(tied, _tied(rank))
            return tied

        tied = lax.fori_loop(0, 2 * PEER_HEADS // LEVEL1_CHAINS, quick, jnp.zeros((1, LANES), F32))

        @pl.when(jnp.max(tied) > 0.0)
        def _():
            def careful(hc, c):
                keep_level1(hc, sub, *_top16(s_s[hc, sub], key_id, break_ties=True))
                return c
            lax.fori_loop(0, 2 * PEER_HEADS, careful, 0)

        def heads(q, c):
            for u in range(LEVEL2_CHAINS):
                level2(q * LEVEL2_CHAINS + u, sub, u)
            return c

        lax.fori_loop(0, PEER_HEADS // LEVEL2_CHAINS, heads, 0)
        return carry

    lax.fori_loop(0, nsub, per_subtile, 0)


KEY_TABLE_ROWS = (N_KEYS // 2, N_KEYS, N_KEYS, N_KEYS // 2)


def _key_spec(nsub, rows, index_map):
    return pl.BlockSpec((PEER_HEADS, nsub, rows, LANES), index_map)


def _x2t_spec(x2t, tt, index_map):
    return pl.BlockSpec((tt // x2t.shape[2], D_MODEL, x2t.shape[2]), index_map)


def _x2t_block(x2t_ref):
    return jnp.concatenate([x2t_ref[t] for t in range(x2t_ref.shape[0])], axis=1)


def _peer_topk(x2t, wpqt, keys, flat, *, tt):
    n = x2t.shape[0] * x2t.shape[2]
    nsub = tt // LANES
    hc = 2 * PEER_HEADS
    return pl.pallas_call(
        _peer_topk_kernel,
        grid=(n // tt,),
        in_specs=[_x2t_spec(x2t, tt, lambda i: (i, 0, 0)), _full(wpqt.shape), _full(keys.shape),
                  _full(flat.shape)],
        out_specs=[_key_spec(nsub, rows, lambda i: (0, i, 0, 0)) for rows in KEY_TABLE_ROWS],
        out_shape=[jax.ShapeDtypeStruct((PEER_HEADS, n // LANES, rows, LANES), jnp.int32)
                   for rows in KEY_TABLE_ROWS],
        scratch_shapes=[pltpu.VMEM((nsub, PEER_HEADS * PEER_DK, LANES), BF16),
                        pltpu.VMEM((hc, nsub, N_KEYS, LANES), F32),
                        pltpu.VMEM((hc, nsub, N_KEYS, LANES), F32),
                        pltpu.VMEM((hc, nsub, PEER_TOPK, LANES), F32),
                        pltpu.VMEM((LEVEL2_CHAINS, CAND_ROWS, LANES), F32),
                        pltpu.VMEM((LEVEL2_CHAINS, CAND_ROWS, LANES), F32)],
        compiler_params=_params("parallel"),
        name="peer_topk",
    )(x2t, wpqt, keys, flat)


EXP_BLK = 1024
EXP_GROUPS = EXP_BLK // N_KEYS
SUBS_PER_PIECE = 1
EXP_STEPS = N_EXPERTS // (2 * EXP_BLK)


def _peer_experts_kernel(x2t_ref, x2_ref, rb_ref, na_ref, ea_ref, eb_ref, u_ref, vt_ref, ln3g_ref, ln3b_ref,
                         y_ref, acc_s, h0_s, h1_s, w0_s, w1_s):
    j = pl.program_id(1)
    nsub = x2t_ref.shape[0] * x2t_ref.shape[2] // LANES
    pieces = max(nsub // SUBS_PER_PIECE, 1)
    piece_subs = nsub // pieces
    d_rows = D_MODEL // pieces
    e_rows = EXP_BLK // pieces

    def key_row(ref, h, sub, ia):
        word = jnp.broadcast_to(ref[h, sub, pl.ds(ia, 1), :], (8, LANES))
        return jnp.tile(pltpu.bitcast(word, BF16), (N_KEYS // 16, 1))

    def gate_piece(h_s, w_s, blk, sub):
        for g in range(EXP_GROUPS):
            ia = blk * EXP_GROUPS + g
            gate = jnp.zeros((N_KEYS, LANES), BF16)
            for h in range(PEER_HEADS):
                eb = _unpacked(eb_ref[h, sub])
                keep = _unpacked(rb_ref[h, sub]) < key_row(na_ref, h, sub, ia)
                gate = gate + jnp.where(keep, eb, jnp.zeros_like(eb)) * key_row(ea_ref, h, sub, ia)
            act = _gelu_tanh(h_s[sub, g * N_KEYS:(g + 1) * N_KEYS, :]).astype(BF16) * gate
            w_s[sub, g * (N_KEYS // 2):(g + 1) * (N_KEYS // 2), :] = pltpu.bitcast(act, jnp.int32)

    def half_step(w_done, h_done, w_next, h_next, half, blk, *, down=True, gate=True, up=True):
        experts = slice(half * EXP_BLK, (half + 1) * EXP_BLK)

        def piece(i, carry):
            if up:
                u_rows = pl.ds(pl.multiple_of((half * EXP_BLK + i * e_rows) // 2, e_rows // 2), e_rows // 2)
                h_new = _dot(_unpacked(u_ref[u_rows, :]), _x2t_block(x2t_ref))
                for s in range(nsub):
                    h_next[s, pl.ds(pl.multiple_of(i * e_rows, e_rows), e_rows), :] = (
                        h_new[:, s * LANES:(s + 1) * LANES])
            if down:
                rows = pl.ds(pl.multiple_of(i * d_rows, d_rows), d_rows)
                w = jnp.concatenate([_unpacked(w_done[s]) for s in range(nsub)], axis=1)
                v_rows = pl.ds(pl.multiple_of(i * (d_rows // 2), d_rows // 2), d_rows // 2)
                acc_s[rows, :] += _dot(_unpacked(vt_ref[v_rows, experts]), w)
            if gate:
                for k in range(piece_subs):
                    gate_piece(h_done, w_next, blk, i * piece_subs + k)
            return carry

        lax.fori_loop(0, pieces, piece, 0)

    last = pl.num_programs(1) - 1

    @pl.when(j == 0)
    def _():
        acc_s[...] = jnp.zeros(acc_s.shape, F32)
        half_step(w0_s, h1_s, w1_s, h0_s, 0, 2 * j - 1, down=False, gate=False)
        half_step(w1_s, h0_s, w0_s, h1_s, 1, 2 * j, down=False)

    @pl.when((j > 0) & (j < last))
    def _():
        half_step(w0_s, h1_s, w1_s, h0_s, 0, 2 * j - 1)
        half_step(w1_s, h0_s, w0_s, h1_s, 1, 2 * j)

    @pl.when(j == last)
    def _():
        half_step(w0_s, h1_s, w1_s, h0_s, 0, 2 * j - 1, up=False)
        half_step(w1_s, h0_s, w0_s, h1_s, 1, 2 * j, gate=False, up=False)
        y_ref[...] = _layer_norm(ALPHA * x2_ref[...] + acc_s[...].T, ln3g_ref[...], ln3b_ref[...])


def _peer_experts(x2t, x2, key_arrs, u_bf, vt_bf, ln3g, ln3b, *, tt):
    n = x2.shape[0]
    nsub = tt // LANES
    return pl.pallas_call(
        _peer_experts_kernel,
        grid=(n // tt, EXP_STEPS + 1),
        in_specs=[_x2t_spec(x2t, tt, lambda i, j: (i, 0, 0)),
                  pl.BlockSpec((tt, D_MODEL), lambda i, j: (i, 0)),
                  *[_key_spec(nsub, rows, lambda i, j: (0, i, 0, 0)) for rows in KEY_TABLE_ROWS],
                  pl.BlockSpec((EXP_BLK, D_MODEL), lambda i, j: (jnp.minimum(j, EXP_STEPS - 1), 0)),
                  pl.BlockSpec((None, D_MODEL // 2, 2 * EXP_BLK), lambda i, j: (jnp.maximum(j - 1, 0), 0, 0)),
                  _full(ln3g.shape), _full(ln3b.shape)],
        out_specs=pl.BlockSpec((tt, D_MODEL), lambda i, j: (i, 0)),
        out_shape=jax.ShapeDtypeStruct((n, D_MODEL), F32),
        scratch_shapes=[pltpu.VMEM((D_MODEL, tt), F32),
                        pltpu.VMEM((nsub, EXP_BLK, LANES), F32), pltpu.VMEM((nsub, EXP_BLK, LANES), F32),
                        pltpu.VMEM((nsub, EXP_BLK // 2, LANES), jnp.int32),
                        pltpu.VMEM((nsub, EXP_BLK // 2, LANES), jnp.int32)],
        compiler_params=_params("parallel", "arbitrary"),
        name="peer_experts",
    )(x2t, x2, *key_arrs, u_bf, vt_bf, ln3g, ln3b)


def _peer(x2t, x2, peer_wts, *, tt):
    wpqt, keys, flat, u_bf, vt_bf, ln3g, ln3b = peer_wts
    key_arrs = _peer_topk(x2t, wpqt, keys, flat, tt=tt)
    return _peer_experts(x2t, x2, key_arrs, u_bf, vt_bf, ln3g, ln3b, tt=tt)


def _rope_tables(pos):
    inv = ROPE_THETA ** (-jnp.arange(0, QK_ROPE, 2, dtype=F32) / QK_ROPE)
    ang = pos.astype(F32)[:, None] * inv[None, :]
    cos, sin, zero = jnp.cos(ang), jnp.sin(ang), jnp.zeros_like(ang)
    pad = jnp.zeros((pos.shape[0], LANES - QK_ROPE), F32)
    return (jnp.concatenate([cos, cos, pad], axis=1),
            jnp.concatenate([-sin, zero, pad], axis=1),
            jnp.concatenate([zero, sin, pad], axis=1))


def _row(v):
    return v.reshape(1, -1).astype(F32)


def _pack_experts_kernel(u_ref, v_ref, up_ref, vtp_ref):
    up_ref[...] = _packed(u_ref[...])
    vtp_ref[...] = _packed(v_ref[...].T)


def _pack_experts(peer_u, peer_v):
    blk = 2 * EXP_BLK
    return pl.pallas_call(
        _pack_experts_kernel,
        grid=(EXP_STEPS,),
        in_specs=[pl.BlockSpec((blk, D_MODEL), lambda i: (i, 0))] * 2,
        out_specs=[pl.BlockSpec((blk // 2, D_MODEL), lambda i: (i, 0)),
                   pl.BlockSpec((None, D_MODEL // 2, blk), lambda i: (i, 0, 0))],
        out_shape=[jax.ShapeDtypeStruct((N_EXPERTS // 2, D_MODEL), jnp.int32),
                   jax.ShapeDtypeStruct((EXP_STEPS, D_MODEL // 2, blk), jnp.int32)],
        compiler_params=_params("parallel"),
        name="pack_experts",
    )(peer_u, peer_v)


def kernel(x_prompt, x_sample, mem_prompt, cache_ckv, cache_krope, cache_mem_k, cache_mem_v, page_table,
           w_in, q_norm_g, kv_norm_g, w_uq, w_uk, w_uv, gm_norm_g, gm_norm_b, gm_ws, gm_bs, attn_out_g,
           gm_out_g, w_out, ln1_g, ln1_b, w_xq, w_mk, w_mv, w_xo, ln2_g, ln2_b, w_pq, peer_keys, peer_u,
           peer_v, ln3_g, ln3_b):
    batch, seq = x_prompt.shape[:2]
    nb = x_sample.shape[0]
    past_len = page_table.shape[1] * PAGE_SIZE

    kr_pad = jnp.zeros((D_MODEL, LANES - QK_ROPE), F32)
    w_in_x = jnp.concatenate([w_in[:, :Q_RANK + KV_RANK + QK_ROPE], kr_pad,
                              w_in[:, Q_RANK + KV_RANK + QK_ROPE:]], axis=1).astype(BF16)
    uq_nope = w_uq[:, :, :QK_NOPE].reshape(Q_RANK, MLA_HEADS * QK_NOPE)
    uq_rope = jnp.pad(w_uq[:, :, QK_NOPE:], ((0, 0), (0, 0), (0, LANES - QK_ROPE)))
    w_uq_x = jnp.concatenate([uq_nope, uq_rope.reshape(Q_RANK, MLA_HEADS * LANES)], axis=1).astype(BF16)
    eye = jnp.eye(MLA_HEADS, dtype=F32)
    w_uk_bd = jnp.einsum('rhd,hg->hdgr', w_uk, eye).reshape(MLA_HEADS * QK_NOPE, MLA_HEADS * KV_RANK).astype(BF16)
    w_uv_bd = jnp.einsum('rhd,hg->hrgd', w_uv, eye).reshape(MLA_HEADS * KV_RANK, MLA_WIDTH).astype(BF16)
    w_uv_t = jnp.transpose(w_uv, (1, 2, 0)).astype(BF16)
    proj_wts = (w_in_x, _row(q_norm_g), _row(kv_norm_g), w_uq_x, w_uk_bd, _row(gm_norm_g), _row(gm_norm_b))
    g_attn_col = jnp.broadcast_to(attn_out_g.astype(F32)[:, None], (MLA_WIDTH, Q_BLK))
    bias_tile = jnp.repeat(gm_bs.T, GM_WIDTH // GM_GROUPS, axis=1).astype(F32)
    w_out_bf, w_xq_bf = w_out.astype(BF16), w_xq.reshape(D_MODEL, X_WIDTH).astype(BF16)
    w_xo_bf = w_xo.reshape(X_WIDTH, D_MODEL).astype(BF16)
    post_tail = (w_xo_bf, _row(ln2_g), _row(ln2_b))
    post_wts = (gm_ws.astype(F32), bias_tile, _row(gm_out_g), w_out_bf, _row(ln1_g), _row(ln1_b), w_xq_bf) + post_tail
    ws0 = jnp.repeat(gm_ws[:, 0, 0], GM_WIDTH // GM_GROUPS)
    bs0 = jnp.repeat(gm_bs[:, 0], GM_WIDTH // GM_GROUPS)
    sample_wts = (w_uv_bd, _row(attn_out_g), _row(ws0), _row(bs0), _row(gm_out_g), w_out_bf, _row(ln1_g),
                  _row(ln1_b), w_xq_bf) + post_tail
    flat = np.full((CAND_ROWS,), 1e8, np.float32)
    for ka in range(PEER_TOPK):
        flat[CAND_OFF[ka]:CAND_OFF[ka] + CAND_NB[ka]] = ka * PEER_TOPK + np.arange(CAND_NB[ka])
    flat = jnp.asarray(np.broadcast_to(flat[:, None], (CAND_ROWS, LANES)))
    peer_wts = (w_pq.reshape(D_MODEL, PEER_HEADS * PEER_DK).T.astype(BF16),
                peer_keys.reshape(2 * PEER_HEADS, N_KEYS, PEER_HALF).astype(BF16), flat,
                *_pack_experts(peer_u, peer_v), _row(ln3_g), _row(ln3_b))

    n_p = batch * seq
    xp = x_prompt.reshape(n_p, D_MODEL)
    tm = 512
    qlat, qrope, kcat, ckv_p, krope_p, u_p, v_p, ckvt = _proj(
        xp, _rope_tables(jnp.arange(seq)), proj_wts, tm=tm, seq_blocks=seq // tm, emit_kt=True)
    a_p = _attn_prompt(qlat, qrope, kcat, ckvt, w_uv_t, g_attn_col, batch=batch, seq=seq)
    mk_p, mv_p = _memkv(mem_prompt.reshape(batch * MEM_TOKENS, D_MODEL),
                        w_mk.reshape(D_MODEL, X_WIDTH).astype(BF16), w_mv.reshape(D_MODEL, X_WIDTH).astype(BF16))
    x2_p, x2t_p = _post_prompt(xp, a_p, u_p, v_p, mk_p, mv_p, post_wts, batch=batch, seq=seq)
    y_p = _peer(x2t_p, x2_p, peer_wts, tt=512)

    xs = x_sample.reshape(nb, D_MODEL)
    pos_s = jnp.full((nb,), past_len, jnp.int32)
    qlat_s, qrope_s, _, ckv_s, krope_s, u_s, v_s = _proj(xs, _rope_tables(pos_s), proj_wts, tm=nb, seq_blocks=1,
                                                          emit_kt=False)
    q_cat = jnp.concatenate([qlat_s.reshape(nb, MLA_HEADS, KV_RANK), qrope_s.reshape(nb, MLA_HEADS, LANES)], axis=2)
    q_pad = jnp.pad(q_cat, ((0, 0), (0, LANES - MLA_HEADS), (0, 0)))
    o_lat_s = _attn_decode(page_table, q_pad, cache_ckv, cache_krope, ckv_s, krope_s)
    x2_s = _post_sample(xs, o_lat_s.reshape(nb, MLA_HEADS * KV_RANK), u_s, v_s,
                        cache_mem_k.reshape(nb, MEM_TOKENS, X_WIDTH), cache_mem_v.reshape(nb, MEM_TOKENS, X_WIDTH),
                        sample_wts)
    y_s = _peer(x2_s.T.astype(BF16).reshape(1, D_MODEL, nb), x2_s, peer_wts, tt=nb)

    return (y_p.reshape(batch, seq, D_MODEL), y_s.reshape(nb, 1, D_MODEL),
            ckv_p.reshape(batch, seq, KV_RANK), krope_p.reshape(batch, seq, QK_ROPE),
            mk_p.reshape(batch, MEM_TOKENS, X_HEADS, X_HEAD_DIM), mv_p.reshape(batch, MEM_TOKENS, X_HEADS, X_HEAD_DIM),
            ckv_s.reshape(nb, 1, KV_RANK), krope_s.reshape(nb, 1, QK_ROPE), v_s.reshape(nb, 1, GM_WIDTH))
```

```python
import functools

import jax
import jax.numpy as jnp
import numpy as np
from jax import lax
from jax.experimental import pallas as pl
from jax.experimental.pallas import tpu as pltpu

F32 = jnp.float32
BF16 = jnp.bfloat16

D_MODEL = 1024
MLA_HEADS = 8
QK_NOPE = 64
QK_ROPE = 32
V_HEAD = 64
Q_RANK = 256
KV_RANK = 128
MLA_WIDTH = MLA_HEADS * V_HEAD
MLA_SCALE = (QK_NOPE + QK_ROPE) ** -0.5
Q_SCALE = float(MLA_SCALE * np.log2(np.e))
ROPE_THETA = 10000.0
GM_WIDTH = D_MODEL // 2
GM_GROUPS = 4
GM_CHUNK = 128
MEM_TOKENS = 256
X_HEADS = 4
X_HEAD_DIM = 128
X_WIDTH = X_HEADS * X_HEAD_DIM
X_SCALE = X_HEAD_DIM ** -0.5
PEER_HEADS = 8
N_KEYS = 128
N_EXPERTS = N_KEYS * N_KEYS
PEER_TOPK = 16
PEER_DK = 256
PEER_HALF = PEER_DK // 2
PAGE_SIZE = 128
DEPTH = 1
ALPHA = (2.0 * DEPTH) ** 0.25
EPS = 1e-5

LANES = 128
VMEM_LIMIT = 56 * 1024 * 1024

CAND_NB = tuple(PEER_TOPK // (ka + 1) for ka in range(PEER_TOPK))
CAND_OFF = tuple(int(sum(CAND_NB[:ka])) for ka in range(PEER_TOPK))
CAND_N = int(sum(CAND_NB))
CAND_ROWS = 56
LEVEL1_CHAINS = 16
LEVEL2_CHAINS = 8
NEG_INF = float("-inf")


def _dot(a, b):
    return jnp.dot(a, b, preferred_element_type=F32)


def _dot_nt(a, b):
    return lax.dot_general(a, b, (((1,), (1,)), ((), ())), preferred_element_type=F32)


def _dot_tn(a, b):
    return lax.dot_general(a, b, (((0,), (0,)), ((), ())), preferred_element_type=F32)


def _layer_norm(x, g, b):
    mu = jnp.mean(x, -1, keepdims=True)
    var = jnp.mean(jnp.square(x - mu), -1, keepdims=True)
    return (x - mu) * lax.rsqrt(var + EPS) * g + b


def _rms_norm(x, g):
    return x * lax.rsqrt(jnp.mean(jnp.square(x), -1, keepdims=True) + EPS) * g


_GELU_K1 = float(-2.0 * np.sqrt(2.0 / np.pi) * np.log2(np.e))
_GELU_K2 = float(0.044715 * _GELU_K1)


def _gelu_tanh(x):
    return x / (1.0 + jnp.exp2(x * (_GELU_K1 + _GELU_K2 * (x * x))))


def _params(*sem):
    return pltpu.CompilerParams(dimension_semantics=sem, vmem_limit_bytes=VMEM_LIMIT)


def _full(shape):
    n = len(shape)
    return pl.BlockSpec(shape, lambda *_: (0,) * n)


def _rope(x, c, s_lo, s_hi):
    width = x.shape[-1]
    return x * c + pltpu.roll(x, width - 16, 1) * s_lo + pltpu.roll(x, 16, 1) * s_hi


def _proj_kernel(x_ref, c_ref, slo_ref, shi_ref, w_in_ref, qg_ref, kvg_ref, w_uq_ref, w_uk_ref,
                 gmg_ref, gmb_ref, qlat_ref, qrope_ref, kcat_ref, ckv_ref, krope_ref, u_ref, v_ref,
                 *maybe_ckvt_ref):
    h = _dot(x_ref[...].astype(BF16), w_in_ref[...])
    c, s_lo, s_hi = c_ref[...], slo_ref[...], shi_ref[...]

    cq = _rms_norm(h[:, :Q_RANK], qg_ref[...])
    q_all = _dot(cq.astype(BF16), w_uq_ref[...])
    q_nope = q_all[:, :MLA_HEADS * QK_NOPE]
    q_lat = _dot(q_nope.astype(BF16), w_uk_ref[...])
    qlat_ref[...] = (q_lat * Q_SCALE).astype(BF16)
    q_rope = _rope(q_all[:, MLA_HEADS * QK_NOPE:], jnp.tile(c, (1, MLA_HEADS)),
                   jnp.tile(s_lo, (1, MLA_HEADS)), jnp.tile(s_hi, (1, MLA_HEADS)))
    qrope_ref[...] = (q_rope * Q_SCALE).astype(BF16)

    ckv = _rms_norm(h[:, Q_RANK:Q_RANK + KV_RANK], kvg_ref[...])
    ckv_ref[...] = ckv
    k_rot = _rope(h[:, Q_RANK + KV_RANK:Q_RANK + KV_RANK + LANES], c, s_lo, s_hi)
    krope_ref[...] = k_rot[:, :QK_ROPE]
    kcat_ref[...] = jnp.concatenate([ckv, k_rot], axis=1).astype(BF16)
    for ckvt_ref in maybe_ckvt_ref:
        for j in range(ckvt_ref.shape[0]):
            ckvt_ref[j] = ckv[j * 256:(j + 1) * 256, :].T.astype(BF16)

    uv = jax.nn.gelu(h[:, Q_RANK + KV_RANK + LANES:])
    u_ref[...] = uv[:, :GM_WIDTH]
    v_ref[...] = _layer_norm(uv[:, GM_WIDTH:], gmg_ref[...], gmb_ref[...])


def _proj(x, tables, wts, *, tm, seq_blocks, emit_kt):
    n = x.shape[0]
    row = lambda w: pl.BlockSpec((tm, w), lambda i: (i, 0))
    tab = pl.BlockSpec((tm, LANES), lambda i: (i % seq_blocks, 0))
    w_in, qg, kvg, w_uq, w_uk, gmg, gmb = wts
    out_shape = [
        jax.ShapeDtypeStruct((n, MLA_HEADS * KV_RANK), BF16),
        jax.ShapeDtypeStruct((n, MLA_HEADS * LANES), BF16),
        jax.ShapeDtypeStruct((n, 2 * LANES), BF16),
        jax.ShapeDtypeStruct((n, KV_RANK), F32),
        jax.ShapeDtypeStruct((n, QK_ROPE), F32),
        jax.ShapeDtypeStruct((n, GM_WIDTH), F32),
        jax.ShapeDtypeStruct((n, GM_WIDTH), F32),
    ]
    out_specs = [row(MLA_HEADS * KV_RANK), row(MLA_HEADS * LANES), row(2 * LANES), row(KV_RANK),
                 row(QK_ROPE), row(GM_WIDTH), row(GM_WIDTH)]
    if emit_kt:
        out_shape.append(jax.ShapeDtypeStruct((n // 256, KV_RANK, 256), BF16))
        out_specs.append(pl.BlockSpec((tm // 256, KV_RANK, 256), lambda i: (i, 0, 0)))
    return pl.pallas_call(
        _proj_kernel,
        grid=(n // tm,),
        in_specs=[row(D_MODEL), tab, tab, tab, _full(w_in.shape), _full(qg.shape), _full(kvg.shape),
                  _full(w_uq.shape), _full(w_uk.shape), _full(gmg.shape), _full(gmb.shape)],
        out_specs=out_specs,
        out_shape=out_shape,
        compiler_params=_params("parallel"),
        name="proj",
    )(x, *tables, w_in, qg, kvg, w_uq, w_uk, gmg, gmb)


Q_BLK = 256
KV_BLK = 256


ATTN_COLS = MLA_HEADS * Q_BLK


def _attn_kernel(qlat_ref, qrope_ref, kcat_ref, ckvt_ref, wuvt_ref, g_ref, a_ref, q_s, m_s, l_s, acc_s):
    qi = pl.program_id(1)
    ql, qr = qlat_ref[...], qrope_ref[...]
    hpc = ATTN_COLS // Q_BLK
    for h in range(MLA_HEADS):
        q_s[h // hpc, (h % hpc) * Q_BLK:(h % hpc + 1) * Q_BLK, :] = jnp.concatenate(
            [ql[:, h * LANES:(h + 1) * LANES], qr[:, h * LANES:(h + 1) * LANES]], axis=1)
    m_s[...] = jnp.full(m_s.shape, NEG_INF, F32)
    l_s[...] = jnp.zeros(l_s.shape, F32)
    acc_s[...] = jnp.zeros(acc_s.shape, F32)
    q_pos = qi * Q_BLK + (lax.broadcasted_iota(jnp.int32, (KV_BLK, ATTN_COLS), 1) & (Q_BLK - 1))
    k_off = lax.broadcasted_iota(jnp.int32, (KV_BLK, ATTN_COLS), 0)

    def scores(j, c):
        k = kcat_ref[pl.ds(pl.multiple_of(j * KV_BLK, KV_BLK), KV_BLK), :]
        return _dot_nt(k, q_s[c])

    def update(j, c, st, diagonal):
        if diagonal:
            st = jnp.where(q_pos >= k_off + j * KV_BLK, st, NEG_INF)
        m_old = m_s[c]
        m_new = jnp.maximum(m_old, jnp.max(st, axis=0, keepdims=True))
        alpha = jnp.exp2(m_old - m_new)
        p = jnp.exp2(st - m_new)
        l_s[c] = alpha * l_s[c] + jnp.sum(p, axis=0, keepdims=True)
        acc_s[c] = alpha * acc_s[c] + _dot(ckvt_ref[j], p.astype(BF16))
        m_s[c] = m_new

    def step(j, diagonal):
        for c in range(MLA_HEADS // hpc):
            update(j, c, scores(j, c), diagonal)

    last = (qi * Q_BLK) // KV_BLK

    def visible_pair(t, carry):
        for c in range(MLA_HEADS // hpc):
            st_a, st_b = scores(2 * t, c), scores(2 * t + 1, c)
            update(2 * t, c, st_a, False)
            update(2 * t + 1, c, st_b, False)
        return carry

    lax.fori_loop(0, last // 2, visible_pair, 0)

    @pl.when(last % 2 == 1)
    def _():
        step(last - 1, False)

    step(last, True)

    o_t = (acc_s[...] / l_s[...]).astype(BF16)
    om_t = jnp.concatenate(
        [_dot(wuvt_ref[h], o_t[h // hpc, :, (h % hpc) * Q_BLK:(h % hpc + 1) * Q_BLK])
         for h in range(MLA_HEADS)], axis=0)
    ms = jnp.mean(jnp.square(om_t), axis=0, keepdims=True)
    a_t = om_t * lax.rsqrt(ms + EPS) * g_ref[...]
    a_ref[...] = a_t.T.astype(BF16)


def _attn_prompt(qlat, qrope, kcat, ckvt, wuvt, g_attn, *, batch, seq):
    n = batch * seq
    nq = seq // Q_BLK
    chains = MLA_HEADS * Q_BLK // ATTN_COLS
    return pl.pallas_call(
        _attn_kernel,
        grid=(batch, nq),
        in_specs=[
            pl.BlockSpec((Q_BLK, MLA_HEADS * KV_RANK), lambda b, i: (b * nq + i, 0)),
            pl.BlockSpec((Q_BLK, MLA_HEADS * LANES), lambda b, i: (b * nq + i, 0)),
            pl.BlockSpec((None, seq, 2 * LANES), lambda b, i: (b, 0, 0)),
            pl.BlockSpec((None, seq // KV_BLK, KV_RANK, KV_BLK), lambda b, i: (b, 0, 0, 0)),
            _full(wuvt.shape), _full(g_attn.shape),
        ],
        out_specs=pl.BlockSpec((Q_BLK, MLA_WIDTH), lambda b, i: (b * nq + i, 0)),
        out_shape=jax.ShapeDtypeStruct((n, MLA_WIDTH), BF16),
        scratch_shapes=[pltpu.VMEM((chains, ATTN_COLS, 2 * LANES), BF16), pltpu.VMEM((chains, 1, ATTN_COLS), F32),
                        pltpu.VMEM((chains, 1, ATTN_COLS), F32), pltpu.VMEM((chains, KV_RANK, ATTN_COLS), F32)],
        compiler_params=_params("parallel", "arbitrary"),
        name="attn_prompt",
    )(qlat, qrope, kcat.reshape(batch, seq, 2 * LANES),
      ckvt.reshape(batch, seq // KV_BLK, KV_RANK, KV_BLK), wuvt, g_attn)


PAGES_PER_STEP = 16


SEQS_PER_STEP = 4


def _decode_kernel(pt_ref, q_ref, *refs):
    del pt_ref
    n_pg = SEQS_PER_STEP * PAGES_PER_STEP
    ckv_refs, kr_refs = refs[:n_pg], refs[n_pg:2 * n_pg]
    ckvn_ref, krn_ref, o_ref, m_s, l_s, acc_s = refs[2 * n_pg:]
    c = pl.program_id(1)

    @pl.when(c == 0)
    def _():
        m_s[...] = jnp.full(m_s.shape, NEG_INF, F32)
        l_s[...] = jnp.zeros(l_s.shape, F32)
        acc_s[...] = jnp.zeros(acc_s.shape, F32)

    def keys(ckv, kr_wide):
        return jnp.concatenate([ckv, kr_wide], axis=1).astype(BF16)

    def rotary_page(kr_t):
        return jnp.concatenate([kr_t, jnp.zeros((LANES - QK_ROPE, PAGE_SIZE), F32)], axis=0).T

    def update(b, kcat, valid_rows):
        half = kcat.shape[0] // 2
        if valid_rows is None:
            st = jnp.concatenate([_dot_nt(kcat[:half], q_ref[b]), _dot_nt(kcat[half:], q_ref[b])], axis=0)
        else:
            st = _dot_nt(kcat, q_ref[b])
        if valid_rows is not None:
            st = jnp.where(lax.broadcasted_iota(jnp.int32, st.shape, 0) < valid_rows, st, NEG_INF)
        m_old = m_s[b]
        m_new = jnp.maximum(m_old, jnp.max(st, axis=0, keepdims=True))
        alpha = jnp.exp2(m_old - m_new)
        p = jnp.exp2(st - m_new)
        l_s[b] = alpha * l_s[b] + jnp.sum(p, axis=0, keepdims=True)
        acc_s[b] = alpha * acc_s[b] + _dot_tn(kcat[:, :KV_RANK], p.astype(BF16))
        m_s[b] = m_new

    for b in range(SEQS_PER_STEP):
        pages = slice(b * PAGES_PER_STEP, (b + 1) * PAGES_PER_STEP)
        update(b, keys(jnp.concatenate([r[...] for r in ckv_refs[pages]], axis=0),
                       jnp.concatenate([rotary_page(r[...]) for r in kr_refs[pages]], axis=0)), None)

    @pl.when(c == pl.num_programs(1) - 1)
    def _():
        for b in range(SEQS_PER_STEP):
            krn = jnp.concatenate([krn_ref[b], jnp.zeros((1, LANES - QK_ROPE), F32)], axis=1)
            update(b, keys(jnp.broadcast_to(ckvn_ref[b], (16, KV_RANK)), jnp.broadcast_to(krn, (16, LANES))), 1)
            o_t = acc_s[b] / l_s[b]
            o_ref[b] = o_t.T[:MLA_HEADS, :]


def _attn_decode(page_table, q_pad, cache_ckv, cache_krope, ckv_new, krope_new):
    nb, n_pages = page_table.shape
    steps = n_pages // PAGES_PER_STEP
    sq = SEQS_PER_STEP

    def page_spec(rows, width, b, i):
        return pl.BlockSpec((None, rows, width),
                            lambda g, c, pt: (pt[g * sq + b, c * PAGES_PER_STEP + i], 0, 0))

    def seq_spec(*dims):
        return pl.BlockSpec((sq,) + dims, lambda g, c, pt: (g,) + (0,) * len(dims))

    in_specs = [seq_spec(LANES, 2 * LANES)]
    in_specs += [page_spec(PAGE_SIZE, KV_RANK, b, i) for b in range(sq) for i in range(PAGES_PER_STEP)]
    in_specs += [page_spec(QK_ROPE, PAGE_SIZE, b, i) for b in range(sq) for i in range(PAGES_PER_STEP)]
    in_specs += [seq_spec(1, KV_RANK), seq_spec(1, QK_ROPE)]
    n_pg = sq * PAGES_PER_STEP
    return pl.pallas_call(
        _decode_kernel,
        grid_spec=pltpu.PrefetchScalarGridSpec(
            num_scalar_prefetch=1,
            grid=(nb // sq, steps),
            in_specs=in_specs,
            out_specs=seq_spec(MLA_HEADS, KV_RANK),
            scratch_shapes=[pltpu.VMEM((sq, 1, LANES), F32), pltpu.VMEM((sq, 1, LANES), F32),
                            pltpu.VMEM((sq, KV_RANK, LANES), F32)],
        ),
        out_shape=jax.ShapeDtypeStruct((nb, MLA_HEADS, KV_RANK), F32),
        compiler_params=_params("parallel", "arbitrary"),
        name="attn_decode",
    )(page_table, q_pad, *([cache_ckv] * n_pg), *([jnp.swapaxes(cache_krope, 1, 2)] * n_pg),
      ckv_new.reshape(nb, 1, KV_RANK), krope_new.reshape(nb, 1, QK_ROPE))


def _memkv_kernel(mem_ref, wk_ref, wv_ref, mk_ref, mv_ref):
    m = mem_ref[...].astype(BF16)
    mk_ref[...] = _dot(m, wk_ref[...])
    mv_ref[...] = _dot(m, wv_ref[...])


def _memkv(mem, w_mk, w_mv):
    n = mem.shape[0]
    tm = 512
    return pl.pallas_call(
        _memkv_kernel,
        grid=(n // tm,),
        in_specs=[pl.BlockSpec((tm, D_MODEL), lambda i: (i, 0)), _full(w_mk.shape), _full(w_mv.shape)],
        out_specs=[pl.BlockSpec((tm, X_WIDTH), lambda i: (i, 0))] * 2,
        out_shape=[jax.ShapeDtypeStruct((n, X_WIDTH), F32)] * 2,
        compiler_params=_params("parallel"),
        name="memkv",
    )(mem, w_mk, w_mv)


POST_ROWS = 512


def _softmax_rows(s):
    e = jnp.exp(s - jnp.max(s, axis=-1, keepdims=True))
    return e / jnp.sum(e, axis=-1, keepdims=True)


def _mix_and_ln1(x, a_bf, o_gm, gmog, w_out, ln1g, ln1b):
    gm_n = _rms_norm(o_gm, gmog)
    y = jnp.concatenate([a_bf, gm_n.astype(BF16)], axis=1)
    return _layer_norm(ALPHA * x + _dot(y, w_out), ln1g, ln1b)


def _post_prompt_kernel(x_ref, a_ref, u_ref, v_ref, ws_ref, bias_ref, gmog_ref, wout_ref, ln1g_ref, ln1b_ref,
                        wxq_ref, mk_ref, mv_ref, wxo_ref, ln2g_ref, ln2b_ref, x2_ref, x2t_ref):
    tril = (lax.broadcasted_iota(jnp.int32, (GM_CHUNK, GM_CHUNK), 0)
            >= lax.broadcasted_iota(jnp.int32, (GM_CHUNK, GM_CHUNK), 1))
    w_s = [jnp.where(tril, ws_ref[g], 0.0).astype(BF16) for g in range(GM_GROUPS)]
    chunks = []
    for c in range(POST_ROWS // GM_CHUNK):
        rows = slice(c * GM_CHUNK, (c + 1) * GM_CHUNK)
        v_c = v_ref[rows, :].astype(BF16)
        s = jnp.concatenate([_dot(w_s[g], v_c[:, g * LANES:(g + 1) * LANES]) for g in range(GM_GROUPS)],
                            axis=1) + bias_ref[...]
        chunks.append(u_ref[rows, :] * s)
    o_gm = jnp.concatenate(chunks, axis=0)
    x1 = _mix_and_ln1(x_ref[...], a_ref[...], o_gm, gmog_ref[...], wout_ref[...], ln1g_ref[...], ln1b_ref[...])

    q = _dot(x1.astype(BF16), wxq_ref[...]).astype(BF16)
    mk, mv = mk_ref[...].astype(BF16), mv_ref[...].astype(BF16)
    heads = []
    for h in range(X_HEADS):
        cs = slice(h * X_HEAD_DIM, (h + 1) * X_HEAD_DIM)
        p = _softmax_rows(_dot_nt(q[:, cs], mk[:, cs]) * X_SCALE)
        heads.append(_dot(p.astype(BF16), mv[:, cs]))
    o = jnp.concatenate(heads, axis=1).astype(BF16)
    x2 = _layer_norm(ALPHA * x1 + _dot(o, wxo_ref[...]), ln2g_ref[...], ln2b_ref[...])
    x2_ref[...] = x2
    x2t_ref[...] = x2.T.astype(BF16)


def _post_prompt(x, a, u, v, mk, mv, wts, *, batch, seq):
    n = batch * seq
    nb = seq // POST_ROWS
    row = lambda w: pl.BlockSpec((POST_ROWS, w), lambda b, i: (b * nb + i, 0))
    mem = pl.BlockSpec((MEM_TOKENS, X_WIDTH), lambda b, i: (b, 0))
    return pl.pallas_call(
        _post_prompt_kernel,
        grid=(batch, nb),
        in_specs=[row(D_MODEL), row(MLA_WIDTH), row(GM_WIDTH), row(GM_WIDTH)]
                 + [_full(w.shape) for w in wts[:6]] + [_full(wts[6].shape), mem, mem]
                 + [_full(w.shape) for w in wts[7:]],
        out_specs=[row(D_MODEL), pl.BlockSpec((None, D_MODEL, POST_ROWS), lambda b, i: (b * nb + i, 0, 0))],
        out_shape=[jax.ShapeDtypeStruct((n, D_MODEL), F32),
                   jax.ShapeDtypeStruct((n // POST_ROWS, D_MODEL, POST_ROWS), BF16)],
        compiler_params=_params("parallel", "parallel"),
        name="post_prompt",
    )(x, a, u, v, *wts[:7], mk, mv, *wts[7:])


SAMPLE_ROWS = 8


def _post_sample_kernel(x_ref, o_ref, u_ref, v_ref, wuv_ref, ag_ref, ws0_ref, bs0_ref, gmog_ref, wout_ref,
                        ln1g_ref, ln1b_ref, wxq_ref, mk_ref, mv_ref, wxo_ref, ln2g_ref, ln2b_ref, x2_ref):
    o_mla = _dot(o_ref[...].astype(BF16), wuv_ref[...])
    a = _rms_norm(o_mla, ag_ref[...]).astype(BF16)
    o_gm = u_ref[...] * (ws0_ref[...] * v_ref[...] + bs0_ref[...])
    x1 = _mix_and_ln1(x_ref[...], a, o_gm, gmog_ref[...], wout_ref[...], ln1g_ref[...], ln1b_ref[...])

    q = _dot(x1.astype(BF16), wxq_ref[...])
    lane_head = lax.broadcasted_iota(jnp.int32, (LANES, X_WIDTH), 1) // X_HEAD_DIM
    on_head = lane_head == lax.broadcasted_iota(jnp.int32, (LANES, X_WIDTH), 0)
    rows = []
    for j in range(SAMPLE_ROWS):
        q_bd = jnp.where(on_head, q[j:j + 1, :], 0.0).astype(BF16)
        s = _dot_nt(mk_ref[j].astype(BF16), q_bd) * X_SCALE
        e = jnp.exp(s - jnp.max(s, axis=0, keepdims=True))
        p = e / jnp.sum(e, axis=0, keepdims=True)
        o_all = _dot_tn(p.astype(BF16), mv_ref[j].astype(BF16))
        rows.append(jnp.sum(jnp.where(on_head, o_all, 0.0), axis=0, keepdims=True))
    o = jnp.concatenate(rows, axis=0).astype(BF16)
    x2_ref[...] = _layer_norm(ALPHA * x1 + _dot(o, wxo_ref[...]), ln2g_ref[...], ln2b_ref[...])


def _post_sample(x, o_lat, u, v, mk, mv, wts):
    n = x.shape[0]
    row = lambda w: pl.BlockSpec((SAMPLE_ROWS, w), lambda i: (i, 0))
    mem = pl.BlockSpec((SAMPLE_ROWS, MEM_TOKENS, X_WIDTH), lambda i: (i, 0, 0))
    return pl.pallas_call(
        _post_sample_kernel,
        grid=(n // SAMPLE_ROWS,),
        in_specs=[row(D_MODEL), row(MLA_HEADS * KV_RANK), row(GM_WIDTH), row(GM_WIDTH)]
                 + [_full(w.shape) for w in wts[:9]] + [mem, mem] + [_full(w.shape) for w in wts[9:]],
        out_specs=row(D_MODEL),
        out_shape=jax.ShapeDtypeStruct((n, D_MODEL), F32),
        compiler_params=_params("parallel"),
        name="post_sample",
    )(x, o_lat, u, v, *wts[:9], mk, mv, *wts[9:])


def _top16(val, row_id, *, break_ties):
    rank = jnp.full(val.shape, 127.0, F32)
    tops = []
    for k in range(PEER_TOPK):
        m = jnp.max(val, axis=0, keepdims=True)
        hit = val == m
        if break_ties:
            hit = row_id == jnp.min(jnp.where(hit, row_id, 1e9), axis=0, keepdims=True)
        val = jnp.where(hit, NEG_INF, val)
        rank = jnp.where(hit, float(k), rank)
        tops.append(m)
    return tops, rank


def _tied(rank):
    marked = jnp.sum(jnp.where(rank < float(PEER_TOPK), 1.0, 0.0), axis=0, keepdims=True)
    return marked - float(PEER_TOPK)


def _packed(x):
    return pltpu.bitcast(x.astype(BF16), jnp.int32)


def _unpacked(w):
    return pltpu.bitcast(w, BF16)


def _bf16_pair(x):
    return pltpu.bitcast(pltpu.pack_elementwise([x, x], packed_dtype=BF16), jnp.int32)


def _peer_topk_kernel(x2t_ref, wpqt_ref, keys_ref, flat_ref, rb_ref, na_ref, ea_ref, eb_ref,
                      qt_s, s_s, rank_s, top_s, cand_s, sel_s):
    nsub = x2t_ref.shape[0] * x2t_ref.shape[2] // LANES
    qt = _dot(wpqt_ref[...], _x2t_block(x2t_ref)).astype(BF16)
    for sub in range(nsub):
        qt_s[sub] = qt[:, sub * LANES:(sub + 1) * LANES]
    key_id = lax.broadcasted_iota(jnp.int32, (N_KEYS, LANES), 0).astype(F32)
    flat = flat_ref[...]

    def keep_level1(hc, sub, tops, rank):
        rank_s[hc, sub] = rank
        for k in range(PEER_TOPK):
            top_s[hc, sub, k:k + 1, :] = tops[k]

    def level2(h, sub, u):
        sa, sb = top_s[2 * h, sub], top_s[2 * h + 1, sub]
        ea_r = jnp.exp(sa - sa[0:1, :])
        eb_r = jnp.exp(sb - sb[0:1, :])
        for ka in range(PEER_TOPK):
            cand_s[u, CAND_OFF[ka]:CAND_OFF[ka] + CAND_NB[ka], :] = sa[ka:ka + 1, :] + sb[0:CAND_NB[ka], :]
        cand_s[u, CAND_N:CAND_ROWS, :] = jnp.full((CAND_ROWS - CAND_N, LANES), NEG_INF, F32)
        _, crank = _top16(cand_s[u], flat, break_ties=True)
        sel_s[u] = jnp.where(crank < float(PEER_TOPK), 1.0, 0.0)
        n_a, z = [], jnp.zeros((1, LANES), F32)
        for ka in range(PEER_TOPK):
            sel_ka = sel_s[u, CAND_OFF[ka]:CAND_OFF[ka] + CAND_NB[ka], :]
            n_a.append(jnp.sum(sel_ka, axis=0, keepdims=True))
            z = z + ea_r[ka:ka + 1, :] * jnp.sum(sel_ka * eb_r[0:CAND_NB[ka], :], axis=0, keepdims=True)
        rank_a = rank_s[2 * h, sub]
        na = jnp.zeros((N_KEYS, LANES), F32)
        for ka in range(PEER_TOPK):
            na = jnp.where(rank_a == float(ka), n_a[ka], na)
        na_ref[h, sub] = _bf16_pair(na)
        rb_ref[h, sub] = _packed(rank_s[2 * h + 1, sub])
        ea_ref[h, sub] = _bf16_pair(jnp.exp(s_s[2 * h, sub] - sa[0:1, :]))
        eb_ref[h, sub] = _packed(jnp.exp(s_s[2 * h + 1, sub] - sb[0:1, :]) / z)

    def per_subtile(sub, carry):
        def quick(h, tied):
            for hc in [LEVEL1_CHAINS * h + u for u in range(LEVEL1_CHAINS)]:
                q_blk = qt_s[sub, pl.ds(pl.multiple_of(hc * PEER_HALF, PEER_HALF), PEER_HALF), :]
                s = _dot(keys_ref[hc], q_blk)
                s_s[hc, sub] = s
                tops, rank = _top16(s, key_id, break_ties=False)
                keep_level1(hc, sub, tops, rank)
                tied = jnp.maximum(tied, _tied(rank))
            return tied

        tied = lax.fori_loop(0, 2 * PEER_HEADS // LEVEL1_CHAINS, quick, jnp.zeros((1, LANES), F32))

        @pl.when(jnp.max(tied) > 0.0)
        def _():
            def careful(hc, c):
                keep_level1(hc, sub, *_top16(s_s[hc, sub], key_id, break_ties=True))
                return c
            lax.fori_loop(0, 2 * PEER_HEADS, careful, 0)

        def heads(q, c):
            for u in range(LEVEL2_CHAINS):
                level2(q * LEVEL2_CHAINS + u, sub, u)
            return c

        lax.fori_loop(0, PEER_HEADS // LEVEL2_CHAINS, heads, 0)
        return carry

    lax.fori_loop(0, nsub, per_subtile, 0)


KEY_TABLE_ROWS = (N_KEYS // 2, N_KEYS, N_KEYS, N_KEYS // 2)


def _key_spec(nsub, rows, index_map):
    return pl.BlockSpec((PEER_HEADS, nsub, rows, LANES), index_map)


def _x2t_spec(x2t, tt, index_map):
    return pl.BlockSpec((tt // x2t.shape[2], D_MODEL, x2t.shape[2]), index_map)


def _x2t_block(x2t_ref):
    return jnp.concatenate([x2t_ref[t] for t in range(x2t_ref.shape[0])], axis=1)


def _peer_topk(x2t, wpqt, keys, flat, *, tt):
    n = x2t.shape[0] * x2t.shape[2]
    nsub = tt // LANES
    hc = 2 * PEER_HEADS
    return pl.pallas_call(
        _peer_topk_kernel,
        grid=(n // tt,),
        in_specs=[_x2t_spec(x2t, tt, lambda i: (i, 0, 0)), _full(wpqt.shape), _full(keys.shape),
                  _full(flat.shape)],
        out_specs=[_key_spec(nsub, rows, lambda i: (0, i, 0, 0)) for rows in KEY_TABLE_ROWS],
        out_shape=[jax.ShapeDtypeStruct((PEER_HEADS, n // LANES, rows, LANES), jnp.int32)
                   for rows in KEY_TABLE_ROWS],
        scratch_shapes=[pltpu.VMEM((nsub, PEER_HEADS * PEER_DK, LANES), BF16),
                        pltpu.VMEM((hc, nsub, N_KEYS, LANES), F32),
                        pltpu.VMEM((hc, nsub, N_KEYS, LANES), F32),
                        pltpu.VMEM((hc, nsub, PEER_TOPK, LANES), F32),
                        pltpu.VMEM((LEVEL2_CHAINS, CAND_ROWS, LANES), F32),
                        pltpu.VMEM((LEVEL2_CHAINS, CAND_ROWS, LANES), F32)],
        compiler_params=_params("parallel"),
        name="peer_topk",
    )(x2t, wpqt, keys, flat)


EXP_BLK = 1024
EXP_GROUPS = EXP_BLK // N_KEYS
SUBS_PER_PIECE = 2
EXP_STEPS = N_EXPERTS // (2 * EXP_BLK)


def _peer_experts_kernel(x2t_ref, x2_ref, rb_ref, na_ref, ea_ref, eb_ref, u_ref, vt_ref, ln3g_ref, ln3b_ref,
                         y_ref, acc_s, h0_s, h1_s, w0_s, w1_s):
    j = pl.program_id(1)
    nsub = x2t_ref.shape[0] * x2t_ref.shape[2] // LANES
    pieces = max(nsub // SUBS_PER_PIECE, 1)
    piece_subs = nsub // pieces
    d_rows = D_MODEL // pieces
    e_rows = EXP_BLK // pieces

    def key_row(ref, h, sub, ia):
        word = jnp.broadcast_to(ref[h, sub, pl.ds(ia, 1), :], (8, LANES))
        return jnp.tile(pltpu.bitcast(word, BF16), (N_KEYS // 16, 1))

    def gate_piece(h_s, w_s, blk, sub):
        for g in range(EXP_GROUPS):
            ia = blk * EXP_GROUPS + g
            gate = jnp.zeros((N_KEYS, LANES), BF16)
            for h in range(PEER_HEADS):
                eb = _unpacked(eb_ref[h, sub])
                keep = _unpacked(rb_ref[h, sub]) < key_row(na_ref, h, sub, ia)
                gate = gate + jnp.where(keep, eb, jnp.zeros_like(eb)) * key_row(ea_ref, h, sub, ia)
            act = _gelu_tanh(h_s[sub, g * N_KEYS:(g + 1) * N_KEYS, :]).astype(BF16) * gate
            w_s[sub, g * (N_KEYS // 2):(g + 1) * (N_KEYS // 2), :] = pltpu.bitcast(act, jnp.int32)

    def half_step(w_done, h_done, w_next, h_next, half, blk, *, down=True, gate=True, up=True):
        experts = slice(half * EXP_BLK, (half + 1) * EXP_BLK)

        def piece(i, carry):
            if up:
                u_rows = pl.ds(pl.multiple_of((half * EXP_BLK + i * e_rows) // 2, e_rows // 2), e_rows // 2)
                h_new = _dot(_unpacked(u_ref[u_rows, :]), _x2t_block(x2t_ref))
                for s in range(nsub):
                    h_next[s, pl.ds(pl.multiple_of(i * e_rows, e_rows), e_rows), :] = (
                        h_new[:, s * LANES:(s + 1) * LANES])
            if down:
                rows = pl.ds(pl.multiple_of(i * d_rows, d_rows), d_rows)
                w = jnp.concatenate([_unpacked(w_done[s]) for s in range(nsub)], axis=1)
                v_rows = pl.ds(pl.multiple_of(i * (d_rows // 2), d_rows // 2), d_rows // 2)
                acc_s[rows, :] += _dot(_unpacked(vt_ref[v_rows, experts]), w)
            if gate:
                for k in range(piece_subs):
                    gate_piece(h_done, w_next, blk, i * piece_subs + k)
            return carry

        lax.fori_loop(0, pieces, piece, 0)

    last = pl.num_programs(1) - 1

    @pl.when(j == 0)
    def _():
        acc_s[...] = jnp.zeros(acc_s.shape, F32)
        half_step(w0_s, h1_s, w1_s, h0_s, 0, 2 * j - 1, down=False, gate=False)
        half_step(w1_s, h0_s, w0_s, h1_s, 1, 2 * j, down=False)

    @pl.when((j > 0) & (j < last))
    def _():
        half_step(w0_s, h1_s, w1_s, h0_s, 0, 2 * j - 1)
        half_step(w1_s, h0_s, w0_s, h1_s, 1, 2 * j)

    @pl.when(j == last)
    def _():
        half_step(w0_s, h1_s, w1_s, h0_s, 0, 2 * j - 1, up=False)
        half_step(w1_s, h0_s, w0_s, h1_s, 1, 2 * j, gate=False, up=False)
        y_ref[...] = _layer_norm(ALPHA * x2_ref[...] + acc_s[...].T, ln3g_ref[...], ln3b_ref[...])


def _peer_experts(x2t, x2, key_arrs, u_bf, vt_bf, ln3g, ln3b, *, tt):
    n = x2.shape[0]
    nsub = tt // LANES
    return pl.pallas_call(
        _peer_experts_kernel,
        grid=(n // tt, EXP_STEPS + 1),
        in_specs=[_x2t_spec(x2t, tt, lambda i, j: (i, 0, 0)),
                  pl.BlockSpec((tt, D_MODEL), lambda i, j: (i, 0)),
                  *[_key_spec(nsub, rows, lambda i, j: (0, i, 0, 0)) for rows in KEY_TABLE_ROWS],
                  pl.BlockSpec((EXP_BLK, D_MODEL), lambda i, j: (jnp.minimum(j, EXP_STEPS - 1), 0)),
                  pl.BlockSpec((None, D_MODEL // 2, 2 * EXP_BLK), lambda i, j: (jnp.maximum(j - 1, 0), 0, 0)),
                  _full(ln3g.shape), _full(ln3b.shape)],
        out_specs=pl.BlockSpec((tt, D_MODEL), lambda i, j: (i, 0)),
        out_shape=jax.ShapeDtypeStruct((n, D_MODEL), F32),
        scratch_shapes=[pltpu.VMEM((D_MODEL, tt), F32),
                        pltpu.VMEM((nsub, EXP_BLK, LANES), F32), pltpu.VMEM((nsub, EXP_BLK, LANES), F32),
                        pltpu.VMEM((nsub, EXP_BLK // 2, LANES), jnp.int32),
                        pltpu.VMEM((nsub, EXP_BLK // 2, LANES), jnp.int32)],
        compiler_params=_params("parallel", "arbitrary"),
        name="peer_experts",
    )(x2t, x2, *key_arrs, u_bf, vt_bf, ln3g, ln3b)


def _peer(x2t, x2, peer_wts, *, tt):
    wpqt, keys, flat, u_bf, vt_bf, ln3g, ln3b = peer_wts
    key_arrs = _peer_topk(x2t, wpqt, keys, flat, tt=tt)
    return _peer_experts(x2t, x2, key_arrs, u_bf, vt_bf, ln3g, ln3b, tt=tt)


def _rope_tables(pos):
    inv = ROPE_THETA ** (-jnp.arange(0, QK_ROPE, 2, dtype=F32) / QK_ROPE)
    ang = pos.astype(F32)[:, None] * inv[None, :]
    cos, sin, zero = jnp.cos(ang), jnp.sin(ang), jnp.zeros_like(ang)
    pad = jnp.zeros((pos.shape[0], LANES - QK_ROPE), F32)
    return (jnp.concatenate([cos, cos, pad], axis=1),
            jnp.concatenate([-sin, zero, pad], axis=1),
            jnp.concatenate([zero, sin, pad], axis=1))


def _row(v):
    return v.reshape(1, -1).astype(F32)


def _pack_experts_kernel(u_ref, v_ref, up_ref, vtp_ref):
    up_ref[...] = _packed(u_ref[...])
    vtp_ref[...] = _packed(v_ref[...].T)


def _pack_experts(peer_u, peer_v):
    blk = 2 * EXP_BLK
    return pl.pallas_call(
        _pack_experts_kernel,
        grid=(EXP_STEPS,),
        in_specs=[pl.BlockSpec((blk, D_MODEL), lambda i: (i, 0))] * 2,
        out_specs=[pl.BlockSpec((blk // 2, D_MODEL), lambda i: (i, 0)),
                   pl.BlockSpec((None, D_MODEL // 2, blk), lambda i: (i, 0, 0))],
        out_shape=[jax.ShapeDtypeStruct((N_EXPERTS // 2, D_MODEL), jnp.int32),
                   jax.ShapeDtypeStruct((EXP_STEPS, D_MODEL // 2, blk), jnp.int32)],
        compiler_params=_params("parallel"),
        name="pack_experts",
    )(peer_u, peer_v)


def kernel(x_prompt, x_sample, mem_prompt, cache_ckv, cache_krope, cache_mem_k, cache_mem_v, page_table,
           w_in, q_norm_g, kv_norm_g, w_uq, w_uk, w_uv, gm_norm_g, gm_norm_b, gm_ws, gm_bs, attn_out_g,
           gm_out_g, w_out, ln1_g, ln1_b, w_xq, w_mk, w_mv, w_xo, ln2_g, ln2_b, w_pq, peer_keys, peer_u,
           peer_v, ln3_g, ln3_b):
    batch, seq = x_prompt.shape[:2]
    nb = x_sample.shape[0]
    past_len = page_table.shape[1] * PAGE_SIZE

    kr_pad = jnp.zeros((D_MODEL, LANES - QK_ROPE), F32)
    w_in_x = jnp.concatenate([w_in[:, :Q_RANK + KV_RANK + QK_ROPE], kr_pad,
                              w_in[:, Q_RANK + KV_RANK + QK_ROPE:]], axis=1).astype(BF16)
    uq_nope = w_uq[:, :, :QK_NOPE].reshape(Q_RANK, MLA_HEADS * QK_NOPE)
    uq_rope = jnp.pad(w_uq[:, :, QK_NOPE:], ((0, 0), (0, 0), (0, LANES - QK_ROPE)))
    w_uq_x = jnp.concatenate([uq_nope, uq_rope.reshape(Q_RANK, MLA_HEADS * LANES)], axis=1).astype(BF16)
    eye = jnp.eye(MLA_HEADS, dtype=F32)
    w_uk_bd = jnp.einsum('rhd,hg->hdgr', w_uk, eye).reshape(MLA_HEADS * QK_NOPE, MLA_HEADS * KV_RANK).astype(BF16)
    w_uv_bd = jnp.einsum('rhd,hg->hrgd', w_uv, eye).reshape(MLA_HEADS * KV_RANK, MLA_WIDTH).astype(BF16)
    w_uv_t = jnp.transpose(w_uv, (1, 2, 0)).astype(BF16)
    proj_wts = (w_in_x, _row(q_norm_g), _row(kv_norm_g), w_uq_x, w_uk_bd, _row(gm_norm_g), _row(gm_norm_b))
    g_attn_col = jnp.broadcast_to(attn_out_g.astype(F32)[:, None], (MLA_WIDTH, Q_BLK))
    bias_tile = jnp.repeat(gm_bs.T, GM_WIDTH // GM_GROUPS, axis=1).astype(F32)
    w_out_bf, w_xq_bf = w_out.astype(BF16), w_xq.reshape(D_MODEL, X_WIDTH).astype(BF16)
    w_xo_bf = w_xo.reshape(X_WIDTH, D_MODEL).astype(BF16)
    post_tail = (w_xo_bf, _row(ln2_g), _row(ln2_b))
    post_wts = (gm_ws.astype(F32), bias_tile, _row(gm_out_g), w_out_bf, _row(ln1_g), _row(ln1_b), w_xq_bf) + post_tail
    ws0 = jnp.repeat(gm_ws[:, 0, 0], GM_WIDTH // GM_GROUPS)
    bs0 = jnp.repeat(gm_bs[:, 0], GM_WIDTH // GM_GROUPS)
    sample_wts = (w_uv_bd, _row(attn_out_g), _row(ws0), _row(bs0), _row(gm_out_g), w_out_bf, _row(ln1_g),
                  _row(ln1_b), w_xq_bf) + post_tail
    flat = np.full((CAND_ROWS,), 1e8, np.float32)
    for ka in range(PEER_TOPK):
        flat[CAND_OFF[ka]:CAND_OFF[ka] + CAND_NB[ka]] = ka * PEER_TOPK + np.arange(CAND_NB[ka])
    flat = jnp.asarray(np.broadcast_to(flat[:, None], (CAND_ROWS, LANES)))
    peer_wts = (w_pq.reshape(D_MODEL, PEER_HEADS * PEER_DK).T.astype(BF16),
                peer_keys.reshape(2 * PEER_HEADS, N_KEYS, PEER_HALF).astype(BF16), flat,
                *_pack_experts(peer_u, peer_v), _row(ln3_g), _row(ln3_b))

    n_p = batch * seq
    xp = x_prompt.reshape(n_p, D_MODEL)
    tm = 512
    qlat, qrope, kcat, ckv_p, krope_p, u_p, v_p, ckvt = _proj(
        xp, _rope_tables(jnp.arange(seq)), proj_wts, tm=tm, seq_blocks=seq // tm, emit_kt=True)
    a_p = _attn_prompt(qlat, qrope, kcat, ckvt, w_uv_t, g_attn_col, batch=batch, seq=seq)
    mk_p, mv_p = _memkv(mem_prompt.reshape(batch * MEM_TOKENS, D_MODEL),
                        w_mk.reshape(D_MODEL, X_WIDTH).astype(BF16), w_mv.reshape(D_MODEL, X_WIDTH).astype(BF16))
    x2_p, x2t_p = _post_prompt(xp, a_p, u_p, v_p, mk_p, mv_p, post_wts, batch=batch, seq=seq)
    y_p = _peer(x2t_p, x2_p, peer_wts, tt=512)

    xs = x_sample.reshape(nb, D_MODEL)
    pos_s = jnp.full((nb,), past_len, jnp.int32)
    qlat_s, qrope_s, _, ckv_s, krope_s, u_s, v_s = _proj(xs, _rope_tables(pos_s), proj_wts, tm=nb, seq_blocks=1,
                                                          emit_kt=False)
    q_cat = jnp.concatenate([qlat_s.reshape(nb, MLA_HEADS, KV_RANK), qrope_s.reshape(nb, MLA_HEADS, LANES)], axis=2)
    q_pad = jnp.pad(q_cat, ((0, 0), (0, LANES - MLA_HEADS), (0, 0)))
    o_lat_s = _attn_decode(page_table, q_pad, cache_ckv, cache_krope, ckv_s, krope_s)
    x2_s = _post_sample(xs, o_lat_s.reshape(nb, MLA_HEADS * KV_RANK), u_s, v_s,
                        cache_mem_k.reshape(nb, MEM_TOKENS, X_WIDTH), cache_mem_v.reshape(nb, MEM_TOKENS, X_WIDTH),
                        sample_wts)
    y_s = _peer(x2_s.T.astype(BF16).reshape(1, D_MODEL, nb), x2_s, peer_wts, tt=nb)

    return (y_p.reshape(batch, seq, D_MODEL), y_s.reshape(nb, 1, D_MODEL),
            ckv_p.reshape(batch, seq, KV_RANK), krope_p.reshape(batch, seq, QK_ROPE),
            mk_p.reshape(batch, MEM_TOKENS, X_HEADS, X_HEAD_DIM), mv_p.reshape(batch, MEM_TOKENS, X_HEADS, X_HEAD_DIM),
            ckv_s.reshape(nb, 1, KV_RANK), krope_s.reshape(nb, 1, QK_ROPE), v_s.reshape(nb, 1, GM_WIDTH))
```
